```python
import jax
import jax.numpy as jnp
from jax import lax
import numpy as np

D_MODEL = 1024
BATCH = 16
SEQ = 2048
DEPTH = 1
DEC_BATCH = 32
DEC_SEQ = 1
PAST_LEN = 16384
PAGE_SIZE = 128

POOL_WINDOWS = (2, 4, 8, 16)
POOL_GROUP = D_MODEL // 8
POOL_WIDTH = POOL_GROUP * len(POOL_WINDOWS)
POOL_STATE = max(POOL_WINDOWS) - 1
HEAD_DIM = 64
HEADS_PER_GROUP = 4
ATTN_GROUPS = ((128, 1), (512, 4), (2048, 16))
N_ATTN_GROUPS = len(ATTN_GROUPS)
GROUP_WIDTH = HEADS_PER_GROUP * HEAD_DIM
QKV_WIDTH = N_ATTN_GROUPS * 3 * GROUP_WIDTH
IN_WIDTH = POOL_WIDTH + QKV_WIDTH + 2 * D_MODEL
FFN_DIM = ((8 * D_MODEL // 3 + 127) // 128) * 128
ROPE_THETA = 10000.0
RMS_EPS = 1e-6
Q_BLOCK = 128
NEG_INF = -1e30

kernel_name = "hybrid_pool_dilated_attn_macaron_step"


def rmsnorm(x, g):
    xf = x.astype(jnp.float32)
    y = xf * lax.rsqrt(jnp.mean(xf * xf, axis=-1, keepdims=True) + RMS_EPS)
    return (y * g.astype(jnp.float32)).astype(x.dtype)


def swiglu(x, w_gate, w_up, w_down):
    return (jax.nn.silu(x @ w_gate) * (x @ w_up)) @ w_down


def apply_rope(t, pos):
    half = HEAD_DIM // 2
    inv = ROPE_THETA ** (-jnp.arange(half, dtype=jnp.float32) * (2.0 / HEAD_DIM))
    ang = pos.astype(jnp.float32)[:, None] * inv[None, :]
    cos = jnp.cos(ang)[None, :, None, :]
    sin = jnp.sin(ang)[None, :, None, :]
    tf = t.astype(jnp.float32)
    t1, t2 = tf[..., :half], tf[..., half:]
    return jnp.concatenate([t1 * cos - t2 * sin, t1 * sin + t2 * cos], axis=-1).astype(t.dtype)


def pool_mix(u_ext, pos, w_grp, scale):
    n, le, c = u_ext.shape
    L = le - POOL_STATE
    uf = u_ext.astype(jnp.float32)
    cs = jnp.concatenate([jnp.zeros((n, 1, c), jnp.float32), jnp.cumsum(uf, axis=1)], axis=1)
    tok = uf[:, POOL_STATE:]
    end = cs[:, POOL_STATE + 1:]
    groups = []
    for gi, w in enumerate(POOL_WINDOWS):
        lo, hi = gi * POOL_GROUP, (gi + 1) * POOL_GROUP
        begin = cs[:, POOL_STATE + 1 - w: POOL_STATE + 1 - w + L, lo:hi]
        cnt = jnp.minimum(pos + 1, w).astype(jnp.float32)[None, :, None]
        groups.append((end[..., lo:hi] - begin) / cnt - tok[..., lo:hi])
    p = jnp.stack(groups, axis=2)
    p = jnp.einsum('nlgc,gcd->nlgd', p, w_grp.astype(jnp.float32)).reshape(n, L, POOL_WIDTH)
    return (p * scale.astype(jnp.float32)).astype(u_ext.dtype)


def masked_softmax(sc, mask):
    sc = jnp.where(mask, sc, NEG_INF)
    mx = jnp.max(sc, axis=-1, keepdims=True)
    e = jnp.exp(sc - mx)
    den = jnp.sum(e, axis=-1, keepdims=True)
    return e / den, (mx + jnp.log(den))[..., 0]


def dilated_band_attention(q, k, v, window, dil):
    b, s, h, dh = q.shape
    nk = window // dil
    m = -(-s // dil)
    blk = min(Q_BLOCK, m)
    nb = -(-m // blk)
    mp = nb * blk

    def to_res(t):
        t = jnp.pad(t, ((0, 0), (0, m * dil - s), (0, 0), (0, 0)))
        return t.reshape(b, m, dil, h, dh).swapaxes(1, 2).reshape(b * dil, m, h, dh)

    qr = jnp.pad(to_res(q), ((0, 0), (0, mp - m), (0, 0), (0, 0)))
    kr = jnp.pad(to_res(k), ((0, 0), (nk, mp - m), (0, 0), (0, 0)))
    vr = jnp.pad(to_res(v), ((0, 0), (nk, mp - m), (0, 0), (0, 0)))
    kidx = jnp.arange(nb)[:, None] * blk + jnp.arange(blk + nk)[None, :]
    qb = qr.reshape(b * dil, nb, blk, h, dh).astype(jnp.float32)
    kb = kr[:, kidx].astype(jnp.float32)
    vb = vr[:, kidx].astype(jnp.float32)
    sc = jnp.einsum('nbqhd,nbkhd->nbhqk', qb, kb) * (HEAD_DIM ** -0.5)
    dist = jnp.arange(blk)[:, None] + nk - jnp.arange(blk + nk)[None, :]
    mask = ((dist >= 0) & (dist <= nk))[None] & (kidx >= nk)[:, None, :]
    p, lse = masked_softmax(sc, mask[None, :, None])
    o = jnp.einsum('nbhqk,nbkhd->nbqhd', p, vb)
    lse = lse.swapaxes(2, 3)

    def from_res(t):
        t = t.reshape((b, dil, mp) + t.shape[3:])[:, :, :m]
        t = t.swapaxes(1, 2).reshape((b, m * dil) + t.shape[3:])
        return t[:, :s]

    return from_res(o), from_res(lse)


def dilated_cached_attention(q, k_ext, v_ext, window, dil):
    t = q.shape[1]
    n_past = k_ext.shape[1] - t
    nk = window // dil
    kidx = n_past + jnp.arange(t)[:, None] - dil * jnp.arange(nk + 1)[None, :]
    mask = kidx >= 0
    kidx = jnp.maximum(kidx, 0)
    kb = k_ext[:, kidx].astype(jnp.float32)
    vb = v_ext[:, kidx].astype(jnp.float32)
    sc = jnp.einsum('nthd,ntkhd->nthk', q.astype(jnp.float32), kb) * (HEAD_DIM ** -0.5)
    p, lse = masked_softmax(sc, mask[None, :, None, :])
    o = jnp.einsum('nthk,ntkhd->nthd', p, vb)
    return o, lse


def decoder_layer(x, pos, pool_prefix, kv_prefix, lp):
    x = x + 0.5 * swiglu(rmsnorm(x, lp['norm_ffn1']), lp['ffn1_w_gate'], lp['ffn1_w_up'], lp['ffn1_w_down'])
    h = rmsnorm(x, lp['norm_mix'])
    n, L, _ = h.shape
    z = h @ lp['w_in']
    u = z[..., :POOL_WIDTH]
    qkv = z[..., POOL_WIDTH:POOL_WIDTH + QKV_WIDTH].reshape(n, L, N_ATTN_GROUPS, 3, HEADS_PER_GROUP, HEAD_DIM)
    gates = jax.nn.sigmoid(z[..., POOL_WIDTH + QKV_WIDTH:].astype(jnp.float32)).reshape(n, L, 2, D_MODEL)

    u_ext = jnp.concatenate([pool_prefix.astype(u.dtype), u], axis=1)
    pool = pool_mix(u_ext, pos, lp['w_pool_grp'], lp['pool_scale'])
    new_pool = u_ext[:, -POOL_STATE:]

    outs, lses, new_kv = [], [], []
    for g, (window, dil) in enumerate(ATTN_GROUPS):
        q = apply_rope(qkv[:, :, g, 0], pos)
        k = apply_rope(qkv[:, :, g, 1], pos)
        v = qkv[:, :, g, 2]
        if kv_prefix is None:
            o, lse = dilated_band_attention(q, k, v, window, dil)
            k_ext, v_ext = k, v
        else:
            buf = kv_prefix[g].astype(k.dtype)
            k_ext = jnp.concatenate([buf[:, :, 0], k], axis=1)
            v_ext = jnp.concatenate([buf[:, :, 1], v], axis=1)
            o, lse = dilated_cached_attention(q, k_ext, v_ext, window, dil)
        keep = min(window, k_ext.shape[1])
        new_kv.append(jnp.stack([k_ext[:, -keep:], v_ext[:, -keep:]], axis=2))
        outs.append(o)
        lses.append(lse)
    wts = jax.nn.softmax(jnp.stack(lses, axis=0), axis=0)
    attn = jnp.einsum('gnlh,gnlhd->nlhd', wts, jnp.stack(outs, axis=0))

    pool_br = (pool @ lp['w_pool_br']).astype(jnp.float32)
    attn_br = (attn.reshape(n, L, GROUP_WIDTH).astype(x.dtype) @ lp['w_attn_br']).astype(jnp.float32)
    merged = (gates[:, :, 0] * pool_br + gates[:, :, 1] * attn_br).astype(x.dtype)
    x = x + merged @ lp['w_o']
    x = x + 0.5 * swiglu(rmsnorm(x, lp['norm_ffn2']), lp['ffn2_w_gate'], lp['ffn2_w_up'], lp['ffn2_w_down'])
    return x, new_pool, new_kv


def setup_inputs(seed: int = 0) -> dict:
    key = jax.random.key(seed)
    ks = iter(jax.random.split(key, 32))
    f32 = jnp.float32

    def nrm(shape, scale):
        return scale * jax.random.normal(next(ks), shape, f32)

    def gain(shape):
        return 1.0 + 0.02 * jax.random.normal(next(ks), shape, f32)

    kv_shape = lambda w: (DEPTH, DEC_BATCH, min(w, PAST_LEN), 2, HEADS_PER_GROUP, HEAD_DIM)
    return {
        'x_prompt': nrm((BATCH, SEQ, D_MODEL), 1.0),
        'x_sample': nrm((DEC_BATCH, DEC_SEQ, D_MODEL), 1.0),
        'state_pool': nrm((DEPTH, DEC_BATCH, POOL_STATE, POOL_WIDTH), 1.0),
        'cache_kv_w128': nrm(kv_shape(ATTN_GROUPS[0][0]), 1.0),
        'cache_kv_w512': nrm(kv_shape(ATTN_GROUPS[1][0]), 1.0),
        'cache_kv_w2048': nrm(kv_shape(ATTN_GROUPS[2][0]), 1.0),
        'norm_ffn1': gain((DEPTH, D_MODEL)),
        'ffn1_w_gate': nrm((DEPTH, D_MODEL, FFN_DIM), D_MODEL ** -0.5),
        'ffn1_w_up': nrm((DEPTH, D_MODEL, FFN_DIM), D_MODEL ** -0.5),
        'ffn1_w_down': nrm((DEPTH, FFN_DIM, D_MODEL), FFN_DIM ** -0.5),
        'norm_mix': gain((DEPTH, D_MODEL)),
        'w_in': nrm((DEPTH, D_MODEL, IN_WIDTH), D_MODEL ** -0.5),
        'w_pool_grp': nrm((DEPTH, len(POOL_WINDOWS), POOL_GROUP, POOL_GROUP), POOL_GROUP ** -0.5),
        'pool_scale': gain((DEPTH, POOL_WIDTH)),
        'w_pool_br': nrm((DEPTH, POOL_WIDTH, D_MODEL), POOL_WIDTH ** -0.5),
        'w_attn_br': nrm((DEPTH, GROUP_WIDTH, D_MODEL), GROUP_WIDTH ** -0.5),
        'w_o': nrm((DEPTH, D_MODEL, D_MODEL), D_MODEL ** -0.5),
        'norm_ffn2': gain((DEPTH, D_MODEL)),
        'ffn2_w_gate': nrm((DEPTH, D_MODEL, FFN_DIM), D_MODEL ** -0.5),
        'ffn2_w_up': nrm((DEPTH, D_MODEL, FFN_DIM), D_MODEL ** -0.5),
        'ffn2_w_down': nrm((DEPTH, FFN_DIM, D_MODEL), FFN_DIM ** -0.5),
        'norm_final': gain((D_MODEL,)),
    }


def reference(x_prompt, x_sample, state_pool, cache_kv_w128, cache_kv_w512, cache_kv_w2048,
              norm_ffn1, ffn1_w_gate, ffn1_w_up, ffn1_w_down, norm_mix, w_in, w_pool_grp, pool_scale,
              w_pool_br, w_attn_br, w_o, norm_ffn2, ffn2_w_gate, ffn2_w_up, ffn2_w_down, norm_final):
    pos_p = jnp.arange(x_prompt.shape[1], dtype=jnp.int32)
    pos_s = PAST_LEN + jnp.arange(x_sample.shape[1], dtype=jnp.int32)
    hp, hs = x_prompt, x_sample
    pool_p, pool_s = [], []
    kv_p = [[] for _ in ATTN_GROUPS]
    kv_s = [[] for _ in ATTN_GROUPS]
    for l in range(DEPTH):
        lp = {
            'norm_ffn1': norm_ffn1[l], 'ffn1_w_gate': ffn1_w_gate[l], 'ffn1_w_up': ffn1_w_up[l],
            'ffn1_w_down': ffn1_w_down[l], 'norm_mix': norm_mix[l], 'w_in': w_in[l],
            'w_pool_grp': w_pool_grp[l], 'pool_scale': pool_scale[l], 'w_pool_br': w_pool_br[l],
            'w_attn_br': w_attn_br[l], 'w_o': w_o[l], 'norm_ffn2': norm_ffn2[l],
            'ffn2_w_gate': ffn2_w_gate[l], 'ffn2_w_up': ffn2_w_up[l], 'ffn2_w_down': ffn2_w_down[l],
        }
        zeros_pool = jnp.zeros((hp.shape[0], POOL_STATE, POOL_WIDTH), hp.dtype)
        hp, new_pool_p, new_kv_p = decoder_layer(hp, pos_p, zeros_pool, None, lp)
        hs, new_pool_s, new_kv_s = decoder_layer(
            hs, pos_s, state_pool[l], (cache_kv_w128[l], cache_kv_w512[l], cache_kv_w2048[l]), lp)
        pool_p.append(new_pool_p)
        pool_s.append(new_pool_s)
        for g in range(N_ATTN_GROUPS):
            kv_p[g].append(new_kv_p[g])
            kv_s[g].append(new_kv_s[g])
    y_prompt = rmsnorm(hp, norm_final)
    y_sample = rmsnorm(hs, norm_final)
    return (y_prompt, y_sample,
            jnp.stack(pool_p), jnp.stack(kv_p[0]), jnp.stack(kv_p[1]), jnp.stack(kv_p[2]),
            jnp.stack(pool_s), jnp.stack(kv_s[0]), jnp.stack(kv_s[1]), jnp.stack(kv_s[2]))
```

```python
import functools

import jax
import jax.numpy as jnp
from jax import lax
from jax.experimental import pallas as pl
from jax.experimental.pallas import tpu as pltpu

F32 = jnp.float32
BF16 = jnp.bfloat16

PAST_LEN = 16384
POOL_WINDOWS = (2, 4, 8, 16)
POOL_GROUP = 128
POOL_WIDTH = POOL_GROUP * len(POOL_WINDOWS)
POOL_STATE = max(POOL_WINDOWS) - 1
POOL_CARRY = 16
HEAD_DIM = 64
HEADS = 4
GROUP_WIDTH = HEADS * HEAD_DIM
ATTN_GROUPS = ((128, 1), (512, 4), (2048, 16))
KEYS_PER_QUERY = 128
QKV_GROUP_WIDTH = 3 * GROUP_WIDTH
ROPE_THETA = 10000.0
RMS_EPS = 1e-6
NEG_INF = -1e30
ATTN_SCALE = HEAD_DIM ** -0.5

V7X_VMEM_BYTES = 64 * 1024 * 1024
V7X_LANES = 128
V7X_MXU_COLS = 256

PROMPT_ROW_TILE = 512
FFN_CHUNK = 4 * V7X_MXU_COLS
SAMPLE_ATTN_TILE = 8


def _vmem_limit(pipelined_bytes, resident_bytes, temp_bytes):
    need = 2 * pipelined_bytes + resident_bytes + temp_bytes
    return int(min(need, V7X_VMEM_BYTES - (4 << 20)))


def _nbytes(shape, dtype):
    n = 1
    for s in shape:
        n *= s
    return n * jnp.dtype(dtype).itemsize


def _resident(shape):
    zeros = (0,) * len(shape)
    return pl.BlockSpec(shape, lambda *_: zeros, pipeline_mode=pl.Buffered(1))


def _rmsnorm(x, g):
    ms = jnp.mean(x * x, axis=-1, keepdims=True)
    return x * lax.rsqrt(ms + RMS_EPS) * g


def _ffn_chunks(width):
    return [(lo, min(lo + FFN_CHUNK, width)) for lo in range(0, width, FFN_CHUNK)]


def _swiglu(xn, wg_ref, wu_ref, wd_ref):
    acc = None
    for lo, hi in _ffn_chunks(wg_ref.shape[1]):
        gate = jnp.dot(xn, wg_ref[:, lo:hi], preferred_element_type=F32)
        up = jnp.dot(xn, wu_ref[:, lo:hi], preferred_element_type=F32)
        hid = (gate * jax.nn.sigmoid(gate) * up).astype(BF16)
        part = jnp.dot(hid, wd_ref[lo:hi, :], preferred_element_type=F32)
        acc = part if acc is None else acc + part
    return acc


def _macaron_ffn(x, g_ref, wg_ref, wu_ref, wd_ref):
    xn = _rmsnorm(x, g_ref[...]).astype(BF16)
    return x + 0.5 * _swiglu(xn, wg_ref, wu_ref, wd_ref)


def _ffn_kernel(x_ref, g_ref, wg_ref, wu_ref, wd_ref, o_ref):
    o_ref[...] = _macaron_ffn(x_ref[...], g_ref, wg_ref, wu_ref, wd_ref)


def _ffn_call(x, g, wg, wu, wd, tm):
    m, d = x.shape
    f = wg.shape[1]
    row = pl.BlockSpec((tm, d), lambda i: (i, 0))
    limit = _vmem_limit(2 * _nbytes((tm, d), F32),
                        3 * _nbytes((d, f), BF16),
                        4 * _nbytes((tm, FFN_CHUNK), F32) + 2 * _nbytes((tm, d), F32))
    return pl.pallas_call(
        _ffn_kernel,
        grid=(m // tm,),
        in_specs=[row, _resident((1, d)), _resident((d, f)), _resident((d, f)), _resident((f, d))],
        out_specs=row,
        out_shape=jax.ShapeDtypeStruct((m, d), F32),
        compiler_params=pltpu.CompilerParams(dimension_semantics=("arbitrary",), vmem_limit_bytes=limit),
        name="ffn",
    )(x, g, wg, wu, wd)


def _rope(t, cos, sin_signed, first_half):
    outs = []
    for c in range(t.shape[1] // V7X_LANES):
        tc = t[:, c * V7X_LANES:(c + 1) * V7X_LANES]
        partner = jnp.where(first_half,
                            pltpu.roll(tc, V7X_LANES - HEAD_DIM // 2, 1),
                            pltpu.roll(tc, HEAD_DIM // 2, 1))
        outs.append(tc * cos + partner * sin_signed)
    return jnp.concatenate(outs, axis=1)


def _first_half_mask():
    lane = lax.broadcasted_iota(jnp.int32, (1, V7X_LANES), 1)
    return (lane % HEAD_DIM) < (HEAD_DIM // 2)


def _qkv_group(h, win_ref, g, cos, sin_signed, first_half):
    base = POOL_WIDTH + g * QKV_GROUP_WIDTH
    qkv = jnp.dot(h, win_ref[:, base:base + QKV_GROUP_WIDTH], preferred_element_type=F32)
    q = _rope(qkv[:, :GROUP_WIDTH], cos, sin_signed, first_half) * ATTN_SCALE
    k = _rope(qkv[:, GROUP_WIDTH:2 * GROUP_WIDTH], cos, sin_signed, first_half)
    v = qkv[:, 2 * GROUP_WIDTH:]
    return q, jnp.concatenate([k, v], axis=1)


def _pool_branch(window_means_minus_tok, wgrp_ref, pscale_ref, wpbr_ref):
    mixed = [jnp.dot(p.astype(BF16), wgrp_ref[gi], preferred_element_type=F32)
             for gi, p in enumerate(window_means_minus_tok)]
    pool = jnp.concatenate(mixed, axis=1) * pscale_ref[...]
    return jnp.dot(pool.astype(BF16), wpbr_ref[...], preferred_element_type=F32)


def _gates(h, win_ref):
    d = win_ref.shape[0]
    base = POOL_WIDTH + len(ATTN_GROUPS) * QKV_GROUP_WIDTH
    gate_pool = jax.nn.sigmoid(jnp.dot(h, win_ref[:, base:base + d], preferred_element_type=F32))
    gate_attn = jax.nn.sigmoid(jnp.dot(h, win_ref[:, base + d:base + 2 * d], preferred_element_type=F32))
    return gate_pool, gate_attn


def _kv_keep_rows(window, seq):
    return min(window, seq)


def _prompt_mix_in_kernel(x_ref, g_ref, win_ref, cos_ref, sin_ref, wgrp_ref, pscale_ref, wpbr_ref,
                          q0_ref, q1_ref, q2_ref, kvb0_ref, kvb1_ref, kvb2_ref,
                          kvf0_ref, kvf1_ref, kvf2_ref, tail_ref, pp_ref, ga_ref, uext_ref,
                          *, ts, seq_tiles):
    j = pl.program_id(1)
    h = _rmsnorm(x_ref[0], g_ref[...]).astype(BF16)

    u = jnp.dot(h, win_ref[:, :POOL_WIDTH], preferred_element_type=F32)

    @pl.when(j == 0)
    def _():
        uext_ref[0:POOL_CARRY, :] = jnp.zeros((POOL_CARRY, POOL_WIDTH), F32)

    @pl.when(j > 0)
    def _():
        uext_ref[0:POOL_CARRY, :] = uext_ref[ts:ts + POOL_CARRY, :]

    uext_ref[POOL_CARRY:POOL_CARRY + ts, :] = u
    pos = j * ts + lax.broadcasted_iota(jnp.int32, (ts, 1), 0)
    pooled = []
    for gi, w in enumerate(POOL_WINDOWS):
        lo, hi = gi * POOL_GROUP, (gi + 1) * POOL_GROUP
        tok = u[:, lo:hi]
        total = tok
        for back in range(1, w):
            total = total + uext_ref[POOL_CARRY - back:POOL_CARRY - back + ts, lo:hi]
        cnt = jnp.minimum(pos + 1, w).astype(F32)
        pooled.append(total / cnt - tok)
    pool_br = _pool_branch(pooled, wgrp_ref, pscale_ref, wpbr_ref)
    gate_pool, gate_attn = _gates(h, win_ref)
    pp_ref[0] = gate_pool * pool_br
    ga_ref[0] = gate_attn

    @pl.when(j == seq_tiles - 1)
    def _():
        tail_ref[0] = u[ts - POOL_CARRY:, :]

    cos = cos_ref[...]
    sin_signed = sin_ref[...]
    first_half = _first_half_mask()
    outs = ((q0_ref, kvb0_ref, kvf0_ref), (q1_ref, kvb1_ref, kvf1_ref), (q2_ref, kvb2_ref, kvf2_ref))
    for g, (q_ref, kvb_ref, kvf_ref) in enumerate(outs):
        q, kv = _qkv_group(h, win_ref, g, cos, sin_signed, first_half)
        q_ref[0] = q.astype(BF16)
        kvb_ref[0] = kv.astype(BF16)
        keep = kvf_ref.shape[1]
        if keep == ts:
            kvf_ref[0] = kv
        else:
            @pl.when(j == seq_tiles - 1)
            def _(kvf_ref=kvf_ref, kv=kv, keep=keep):
                kvf_ref[0] = kv[ts - keep:, :]


def _prompt_mix_in_call(x, g, win, cos, sin, wgrp, pscale, wpbr, ts):
    b, s, d = x.shape
    seq_tiles = s // ts
    zw = win.shape[1]

    def tile(width):
        return pl.BlockSpec((1, ts, width), lambda bi, j: (bi, j, 0))

    kvf_specs, kvf_shapes = [], []
    for window, _ in ATTN_GROUPS:
        keep = _kv_keep_rows(window, s)
        rows = min(keep, ts)
        first = seq_tiles - keep // rows if keep >= ts else 0
        kvf_specs.append(pl.BlockSpec(
            (1, rows, 2 * GROUP_WIDTH),
            (lambda bi, j, first=first: (bi, jnp.maximum(j - first, 0), 0)) if keep >= ts
            else (lambda bi, j: (bi, 0, 0))))
        kvf_shapes.append(jax.ShapeDtypeStruct((b, keep, 2 * GROUP_WIDTH), F32))
    n_g = len(ATTN_GROUPS)
    out_specs = ([tile(GROUP_WIDTH)] * n_g + [tile(2 * GROUP_WIDTH)] * n_g + kvf_specs
                 + [pl.BlockSpec((1, POOL_CARRY, POOL_WIDTH), lambda bi, j: (bi, 0, 0)), tile(d), tile(d)])
    out_shape = ([jax.ShapeDtypeStruct((b, s, GROUP_WIDTH), BF16)] * n_g
                 + [jax.ShapeDtypeStruct((b, s, 2 * GROUP_WIDTH), BF16)] * n_g + kvf_shapes
                 + [jax.ShapeDtypeStruct((b, POOL_CARRY, POOL_WIDTH), F32),
                    jax.ShapeDtypeStruct((b, s, d), F32), jax.ShapeDtypeStruct((b, s, d), F32)])
    table = pl.BlockSpec((ts, V7X_LANES), lambda bi, j: (j, 0))
    pipelined = (3 * _nbytes((ts, d), F32) + n_g * _nbytes((ts, 3 * GROUP_WIDTH), BF16)
                 + n_g * _nbytes((ts, 2 * GROUP_WIDTH), F32) + 2 * _nbytes((ts, V7X_LANES), F32))
    resident = _nbytes((d, zw), BF16) + _nbytes(wpbr.shape, BF16) + _nbytes(wgrp.shape, BF16)
    temps = (_nbytes((ts + POOL_CARRY, POOL_WIDTH), F32) + 6 * _nbytes((ts, d), F32))
    return pl.pallas_call(
        functools.partial(_prompt_mix_in_kernel, ts=ts, seq_tiles=seq_tiles),
        grid=(b, seq_tiles),
        in_specs=[tile(d), _resident((1, d)), _resident((d, zw)), table, table,
                  _resident(wgrp.shape), _resident((1, POOL_WIDTH)), _resident(wpbr.shape)],
        out_specs=out_specs,
        out_shape=out_shape,
        scratch_shapes=[pltpu.VMEM((ts + POOL_CARRY, POOL_WIDTH), F32)],
        compiler_params=pltpu.CompilerParams(dimension_semantics=("arbitrary", "arbitrary"),
                                             vmem_limit_bytes=_vmem_limit(pipelined, resident, temps)),
        name="prompt_mix_in",
    )(x, g, win, cos, sin, wgrp, pscale, wpbr)


def _head_lane_masks():
    lane = lax.broadcasted_iota(jnp.int32, (1, GROUP_WIDTH), 1)
    return [(lane // HEAD_DIM) == hd for hd in range(HEADS)]


def _per_head_columns_to_lanes(cols, head_masks):
    t = cols.shape[0] // HEADS
    out = jnp.broadcast_to(cols[(HEADS - 1) * t:], (t, GROUP_WIDTH))
    for hd in range(HEADS - 2, -1, -1):
        out = jnp.where(head_masks[hd], jnp.broadcast_to(cols[hd * t:(hd + 1) * t], (t, GROUP_WIDTH)), out)
    return out


def _band_attn_kernel(q_ref, kv_ref, o_ref, lse_ref, *, dil, blocks):
    blk = KEYS_PER_QUERY
    head_masks = _head_lane_masks()
    head_ones = [jnp.where(m, 1.0, 0.0).astype(BF16) for m in head_masks]
    qi = lax.broadcasted_iota(jnp.int32, (HEADS * blk, 1), 0) % blk
    first_mask = lax.broadcasted_iota(jnp.int32, (1, blk), 1) <= qi
    dist = qi + blk - lax.broadcasted_iota(jnp.int32, (1, 2 * blk), 1)
    band_mask = (dist >= 0) & (dist <= blk)
    for r in range(dil):
        qc = slice(r * GROUP_WIDTH, (r + 1) * GROUP_WIDTH)
        kc = slice(2 * r * GROUP_WIDTH, (2 * r + 1) * GROUP_WIDTH)
        vc = slice((2 * r + 1) * GROUP_WIDTH, (2 * r + 2) * GROUP_WIDTH)
        for i in range(blocks):
            rows = slice(i * blk, (i + 1) * blk)
            krows = slice(max(i - 1, 0) * blk, (i + 1) * blk)
            q = q_ref[0, rows, qc]
            k = kv_ref[0, krows, kc]
            v = kv_ref[0, krows, vc]
            q_heads = jnp.concatenate([q * one for one in head_ones], axis=0)
            sc = lax.dot_general(q_heads, k, (((1,), (1,)), ((), ())), preferred_element_type=F32)
            sc = jnp.where(first_mask if i == 0 else band_mask, sc, NEG_INF)
            mx = jnp.max(sc, axis=1, keepdims=True)
            e = jnp.exp(sc - mx)
            den = jnp.sum(e, axis=1, keepdims=True)
            eb = e.astype(BF16)
            e_cat = jnp.concatenate([eb[hd * blk:(hd + 1) * blk] for hd in range(HEADS)], axis=1)
            v_heads = jnp.concatenate([v * one for one in head_ones], axis=0)
            o_un = jnp.dot(e_cat, v_heads, preferred_element_type=F32)
            o_ref[0, rows, qc] = o_un / _per_head_columns_to_lanes(den, head_masks)
            lse_ref[0, rows, qc] = _per_head_columns_to_lanes(mx + jnp.log(den), head_masks)


def _band_attn_call(q, kv, dil):
    b, s, _ = q.shape
    m = s // dil
    qv = q.reshape(b, m, dil * GROUP_WIDTH)
    kvv = kv.reshape(b, m, 2 * dil * GROUP_WIDTH)
    qspec = pl.BlockSpec((1, m, dil * GROUP_WIDTH), lambda bi: (bi, 0, 0))
    kvspec = pl.BlockSpec((1, m, 2 * dil * GROUP_WIDTH), lambda bi: (bi, 0, 0))
    pipelined = _nbytes((s, 3 * GROUP_WIDTH), BF16) + 2 * _nbytes((s, GROUP_WIDTH), F32)
    o, lse = pl.pallas_call(
        functools.partial(_band_attn_kernel, dil=dil, blocks=m // KEYS_PER_QUERY),
        grid=(b,),
        in_specs=[qspec, kvspec],
        out_specs=[qspec, qspec],
        out_shape=[jax.ShapeDtypeStruct(qv.shape, F32)] * 2,
        compiler_params=pltpu.CompilerParams(dimension_semantics=("arbitrary",),
                                             vmem_limit_bytes=_vmem_limit(pipelined, 0, 16 << 20)),
        name=f"band_attn_d{dil}",
    )(qv, kvv)
    return o.reshape(b * s, GROUP_WIDTH), lse.reshape(b * s, GROUP_WIDTH)


def _mix_out_kernel(x_ref, pp_ref, ga_ref, o0_ref, o1_ref, o2_ref, l0_ref, l1_ref, l2_ref,
                    wabr_ref, wo_ref, g2_ref, wg_ref, wu_ref, wd_ref, gf_ref, y_ref):
    lses = [l0_ref[...], l1_ref[...], l2_ref[...]]
    outs = [o0_ref[...], o1_ref[...], o2_ref[...]]
    mx = jnp.maximum(jnp.maximum(lses[0], lses[1]), lses[2])
    ws = [jnp.exp(l - mx) for l in lses]
    attn = (ws[0] * outs[0] + ws[1] * outs[1] + ws[2] * outs[2]) / (ws[0] + ws[1] + ws[2])
    attn_br = jnp.dot(attn.astype(BF16), wabr_ref[...], preferred_element_type=F32)
    merged = pp_ref[...] + ga_ref[...] * attn_br
    x = x_ref[...] + jnp.dot(merged.astype(BF16), wo_ref[...], preferred_element_type=F32)
    x = _macaron_ffn(x, g2_ref, wg_ref, wu_ref, wd_ref)
    y_ref[...] = _rmsnorm(x, gf_ref[...])


def _mix_out_call(x, pp, ga, outs, lses, wabr, wo, g2, wg, wu, wd, gf, tm):
    m, d = x.shape
    f = wg.shape[1]
    row = pl.BlockSpec((tm, d), lambda i: (i, 0))
    grp = pl.BlockSpec((tm, GROUP_WIDTH), lambda i: (i, 0))
    n_g = len(ATTN_GROUPS)
    pipelined = 4 * _nbytes((tm, d), F32) + 2 * n_g * _nbytes((tm, GROUP_WIDTH), F32)
    resident = 3 * _nbytes((d, f), BF16) + _nbytes((d, d), BF16) + _nbytes((GROUP_WIDTH, d), BF16)
    temps = 4 * _nbytes((tm, FFN_CHUNK), F32) + 4 * _nbytes((tm, d), F32)
    return pl.pallas_call(
        _mix_out_kernel,
        grid=(m // tm,),
        in_specs=[row, row, row] + [grp] * (2 * n_g)
                 + [_resident((GROUP_WIDTH, d)), _resident((d, d)), _resident((1, d)),
                    _resident((d, f)), _resident((d, f)), _resident((f, d)), _resident((1, d))],
        out_specs=row,
        out_shape=jax.ShapeDtypeStruct((m, d), F32),
        compiler_params=pltpu.CompilerParams(dimension_semantics=("arbitrary",),
                                             vmem_limit_bytes=_vmem_limit(pipelined, resident, temps)),
        name="mix_out",
    )(x, pp, ga, *outs, *lses, wabr, wo, g2, wg, wu, wd, gf)


def _sample_mix_in_kernel(x_ref, g_ref, win_ref, state_ref, cos_ref, sin_ref, wgrp_ref, pscale_ref, wpbr_ref,
                          q0_ref, q1_ref, q2_ref, kvn0_ref, kvn1_ref, kvn2_ref, u_ref, pp_ref, ga_ref):
    h = _rmsnorm(x_ref[...], g_ref[...]).astype(BF16)
    u = jnp.dot(h, win_ref[:, :POOL_WIDTH], preferred_element_type=F32)
    u_ref[...] = u
    pooled = []
    for gi, w in enumerate(POOL_WINDOWS):
        lo, hi = gi * POOL_GROUP, (gi + 1) * POOL_GROUP
        tok = u[:, lo:hi]
        total = tok
        for back in range(1, w):
            total = total + state_ref[POOL_STATE - back, :, lo:hi]
        pooled.append(total / float(min(PAST_LEN + 1, w)) - tok)
    pool_br = _pool_branch(pooled, wgrp_ref, pscale_ref, wpbr_ref)
    gate_pool, gate_attn = _gates(h, win_ref)
    pp_ref[...] = gate_pool * pool_br
    ga_ref[...] = gate_attn
    first_half = _first_half_mask()
    outs = ((q0_ref, kvn0_ref), (q1_ref, kvn1_ref), (q2_ref, kvn2_ref))
    for g, (q_ref, kvn_ref) in enumerate(outs):
        q, kv = _qkv_group(h, win_ref, g, cos_ref[...], sin_ref[...], first_half)
        q_ref[...] = q.astype(BF16)
        kvn_ref[...] = kv


def _sample_mix_in_call(x, g, win, state_t, cos, sin, wgrp, pscale, wpbr):
    n, d = x.shape
    n_g = len(ATTN_GROUPS)

    def full(shape):
        zeros = (0,) * len(shape)
        return pl.BlockSpec(shape, lambda i: zeros)

    out_shape = ([jax.ShapeDtypeStruct((n, GROUP_WIDTH), BF16)] * n_g
                 + [jax.ShapeDtypeStruct((n, 2 * GROUP_WIDTH), F32)] * n_g
                 + [jax.ShapeDtypeStruct((n, POOL_WIDTH), F32),
                    jax.ShapeDtypeStruct((n, d), F32), jax.ShapeDtypeStruct((n, d), F32)])
    ins = (x, g, win, state_t, cos, sin, wgrp, pscale, wpbr)
    resident = sum(_nbytes(a.shape, a.dtype) for a in ins)
    return pl.pallas_call(
        _sample_mix_in_kernel,
        grid=(1,),
        in_specs=[full(a.shape) for a in ins],
        out_specs=[full(o.shape) for o in out_shape],
        out_shape=out_shape,
        compiler_params=pltpu.CompilerParams(dimension_semantics=("arbitrary",),
                                             vmem_limit_bytes=_vmem_limit(resident, 0, 8 << 20)),
        name="sample_mix_in",
    )(*ins)


def _cached_attn_kernel(q0_ref, q1_ref, q2_ref, kvn0_ref, kvn1_ref, kvn2_ref, c0_ref, c1_ref, c2_ref,
                        o0_ref, o1_ref, o2_ref, l0_ref, l1_ref, l2_ref):
    rows = 2 * HEADS
    lane = lax.broadcasted_iota(jnp.int32, (rows, GROUP_WIDTH), 1)
    head_rows = (lane // HEAD_DIM) == lax.broadcasted_iota(jnp.int32, (rows, GROUP_WIDTH), 0)
    groups = ((q0_ref, kvn0_ref, c0_ref, o0_ref, l0_ref), (q1_ref, kvn1_ref, c1_ref, o1_ref, l1_ref),
              (q2_ref, kvn2_ref, c2_ref, o2_ref, l2_ref))
    for q_ref, kvn_ref, c_ref, o_ref, l_ref in groups:
        for n in range(q_ref.shape[0]):
            q = jnp.broadcast_to(q_ref[n:n + 1, :].astype(F32), (rows, GROUP_WIDTH))
            q_heads = jnp.where(head_rows, q, 0.0)
            kc = c_ref[n, :, :GROUP_WIDTH].astype(BF16)
            vc = c_ref[n, :, GROUP_WIDTH:].astype(BF16)
            k_new = kvn_ref[n:n + 1, :GROUP_WIDTH].astype(BF16).astype(F32)
            v_new = kvn_ref[n:n + 1, GROUP_WIDTH:].astype(BF16).astype(F32)
            sc = lax.dot_general(q_heads.astype(BF16), kc, (((1,), (1,)), ((), ())),
                                 preferred_element_type=F32)
            sc_new = jnp.sum(q_heads * k_new, axis=1, keepdims=True)
            mx = jnp.maximum(jnp.max(sc, axis=1, keepdims=True), sc_new)
            e = jnp.exp(sc - mx)
            e_new = jnp.exp(sc_new - mx)
            den = jnp.sum(e, axis=1, keepdims=True) + e_new
            o = (jnp.dot(e.astype(BF16), vc, preferred_element_type=F32) + e_new * v_new) / den
            lse = jnp.broadcast_to(mx + jnp.log(den), (rows, GROUP_WIDTH))
            o_ref[n:n + 1, :] = jnp.sum(jnp.where(head_rows, o, 0.0), axis=0, keepdims=True)
            l_ref[n:n + 1, :] = jnp.sum(jnp.where(head_rows, lse, 0.0), axis=0, keepdims=True)


def _cached_attn_call(qs, kvns, caches):
    n = qs[0].shape[0]
    tn = SAMPLE_ATTN_TILE
    views = []
    for cache, (window, dil) in zip(caches, ATTN_GROUPS):
        views.append(cache.reshape(n, window // dil, dil * 2 * GROUP_WIDTH))
    qspec = pl.BlockSpec((tn, GROUP_WIDTH), lambda i: (i, 0))
    kvnspec = pl.BlockSpec((tn, 2 * GROUP_WIDTH), lambda i: (i, 0))
    cspec = pl.BlockSpec((tn, KEYS_PER_QUERY, 2 * GROUP_WIDTH), lambda i: (i, 0, 0))
    n_g = len(ATTN_GROUPS)
    pipelined = n_g * (_nbytes((tn, KEYS_PER_QUERY, 2 * GROUP_WIDTH), F32) + _nbytes((tn, 5 * GROUP_WIDTH), F32))
    res = pl.pallas_call(
        _cached_attn_kernel,
        grid=(n // tn,),
        in_specs=[qspec] * n_g + [kvnspec] * n_g + [cspec] * n_g,
        out_specs=[qspec] * (2 * n_g),
        out_shape=[jax.ShapeDtypeStruct((n, GROUP_WIDTH), F32)] * (2 * n_g),
        compiler_params=pltpu.CompilerParams(dimension_semantics=("arbitrary",),
                                             vmem_limit_bytes=_vmem_limit(pipelined, 0, 8 << 20)),
        name="cached_attn",
    )(*qs, *kvns, *views)
    return res[:n_g], res[n_g:]


def _rope_tables(pos):
    half = HEAD_DIM // 2
    inv = ROPE_THETA ** (-jnp.arange(half, dtype=F32) * (2.0 / HEAD_DIM))
    ang = pos.astype(F32)[:, None] * inv[None, :]
    cos, sin = jnp.cos(ang), jnp.sin(ang)
    reps = V7X_LANES // HEAD_DIM
    return (jnp.tile(jnp.concatenate([cos, cos], axis=1), (1, reps)),
            jnp.tile(jnp.concatenate([-sin, sin], axis=1), (1, reps)))


def kernel(x_prompt, x_sample, state_pool, cache_kv_w128, cache_kv_w512, cache_kv_w2048, norm_ffn1, ffn1_w_gate, ffn1_w_up, ffn1_w_down, norm_mix, w_in, w_pool_grp, pool_scale, w_pool_br, w_attn_br, w_o, norm_ffn2, ffn2_w_gate, ffn2_w_up, ffn2_w_down, norm_final):
    b, s, d = x_prompt.shape
    n, dec_seq, _ = x_sample.shape
    caches = (cache_kv_w128, cache_kv_w512, cache_kv_w2048)
    assert norm_ffn1.shape[0] == 1 and dec_seq == 1, "single layer, single decode token"
    for cache, (window, dil) in zip(caches, ATTN_GROUPS):
        assert window // dil == KEYS_PER_QUERY and s % (dil * KEYS_PER_QUERY) == 0
        assert cache.shape[2] == window, "decode path expects full windows of history"
    ts = PROMPT_ROW_TILE
    assert s % ts == 0 and ts >= POOL_CARRY and n % SAMPLE_ATTN_TILE == 0

    row = lambda v: v.reshape(1, -1)
    bf = lambda w: w.astype(BF16)
    g1, g2, gm, gf = row(norm_ffn1[0]), row(norm_ffn2[0]), row(norm_mix[0]), row(norm_final)
    f1 = (bf(ffn1_w_gate[0]), bf(ffn1_w_up[0]), bf(ffn1_w_down[0]))
    f2 = (bf(ffn2_w_gate[0]), bf(ffn2_w_up[0]), bf(ffn2_w_down[0]))
    win, wgrp, wpbr, wabr, wo = bf(w_in[0]), bf(w_pool_grp[0]), bf(w_pool_br[0]), bf(w_attn_br[0]), bf(w_o[0])
    pscale = row(pool_scale[0])

    cos_p, sin_p = _rope_tables(jnp.arange(s, dtype=jnp.int32))
    x1 = _ffn_call(x_prompt.reshape(b * s, d), g1, *f1, tm=ts)
    (q0, q1, q2, kvb0, kvb1, kvb2, kvf0, kvf1, kvf2, tail, pp, ga) = _prompt_mix_in_call(
        x1.reshape(b, s, d), gm, win, cos_p, sin_p, wgrp, pscale, wpbr, ts)
    outs, lses = [], []
    for q, kv, (_, dil) in zip((q0, q1, q2), (kvb0, kvb1, kvb2), ATTN_GROUPS):
        o, lse = _band_attn_call(q, kv, dil)
        outs.append(o)
        lses.append(lse)
    y_prompt = _mix_out_call(x1, pp.reshape(b * s, d), ga.reshape(b * s, d), outs, lses,
                             wabr, wo, g2, *f2, gf, tm=ts).reshape(b, s, d)
    kv_shape = lambda a: a.reshape(1, a.shape[0], a.shape[1], 2, HEADS, HEAD_DIM)
    pool_prompt = tail[None, :, POOL_CARRY - POOL_STATE:, :]

    cos_s, sin_s = _rope_tables(PAST_LEN + jnp.arange(dec_seq, dtype=jnp.int32))
    xs1 = _ffn_call(x_sample.reshape(n, d), g1, *f1, tm=n)
    state_t = jnp.swapaxes(state_pool[0], 0, 1)
    (sq0, sq1, sq2, kvn0, kvn1, kvn2, u_new, spp, sga) = _sample_mix_in_call(
        xs1, gm, win, state_t, cos_s, sin_s, wgrp, pscale, wpbr)
    s_outs, s_lses = _cached_attn_call((sq0, sq1, sq2), (kvn0, kvn1, kvn2), [c[0] for c in caches])
    y_sample = _mix_out_call(xs1, spp, sga, s_outs, s_lses, wabr, wo, g2, *f2, gf, tm=n).reshape(n, dec_seq, d)
    pool_sample = jnp.concatenate([state_pool[0][:, 1:], u_new[:, None, :]], axis=1)[None]
    kv_sample = []
    for cache, kvn in zip(caches, (kvn0, kvn1, kvn2)):
        new_row = kvn.reshape(n, 1, 2, HEADS, HEAD_DIM)
        kv_sample.append(jnp.concatenate([cache[0][:, 1:], new_row], axis=1)[None])

    return (y_prompt, y_sample, pool_prompt, kv_shape(kvf0), kv_shape(kvf1), kv_shape(kvf2),
            pool_sample, kv_sample[0], kv_sample[1], kv_sample[2])
```

```python
import functools

import jax
import jax.numpy as jnp
from jax import lax
from jax.experimental import pallas as pl
from jax.experimental.pallas import tpu as pltpu

F32 = jnp.float32
BF16 = jnp.bfloat16

PAST_LEN = 16384
POOL_WINDOWS = (2, 4, 8, 16)
POOL_GROUP = 128
POOL_WIDTH = POOL_GROUP * len(POOL_WINDOWS)
POOL_STATE = max(POOL_WINDOWS) - 1
POOL_CARRY = 16
HEAD_DIM = 64
HEADS = 4
GROUP_WIDTH = HEADS * HEAD_DIM
ATTN_GROUPS = ((128, 1), (512, 4), (2048, 16))
KEYS_PER_QUERY = 128
QKV_GROUP_WIDTH = 3 * GROUP_WIDTH
ROPE_THETA = 10000.0
RMS_EPS = 1e-6
NEG_INF = -1e30
ATTN_SCALE = HEAD_DIM ** -0.5

V7X_VMEM_BYTES = 64 * 1024 * 1024
V7X_LANES = 128
V7X_MXU_COLS = 256

PROMPT_ROW_TILE = 512
FFN_CHUNK = 4 * V7X_MXU_COLS


def _vmem_limit(pipelined_bytes, resident_bytes, temp_bytes):
    need = 2 * pipelined_bytes + resident_bytes + temp_bytes
    return int(min(need, V7X_VMEM_BYTES - (4 << 20)))


def _nbytes(shape, dtype):
    n = 1
    for s in shape:
        n *= s
    return n * jnp.dtype(dtype).itemsize


def _resident(shape):
    zeros = (0,) * len(shape)
    return pl.BlockSpec(shape, lambda *_: zeros, pipeline_mode=pl.Buffered(1))


def _rmsnorm(x, g):
    ms = jnp.mean(x * x, axis=-1, keepdims=True)
    return x * lax.rsqrt(ms + RMS_EPS) * g


def _ffn_chunks(width):
    return [(lo, min(lo + FFN_CHUNK, width)) for lo in range(0, width, FFN_CHUNK)]


def _swiglu(xn, wg_ref, wu_ref, wd_ref):
    acc = None
    for lo, hi in _ffn_chunks(wg_ref.shape[1]):
        gate = jnp.dot(xn, wg_ref[:, lo:hi], preferred_element_type=F32)
        up = jnp.dot(xn, wu_ref[:, lo:hi], preferred_element_type=F32)
        hid = (gate * jax.nn.sigmoid(gate) * up).astype(BF16)
        part = jnp.dot(hid, wd_ref[lo:hi, :], preferred_element_type=F32)
        acc = part if acc is None else acc + part
    return acc


def _macaron_ffn(x, g_ref, wg_ref, wu_ref, wd_ref):
    xn = _rmsnorm(x, g_ref[...]).astype(BF16)
    return x + 0.5 * _swiglu(xn, wg_ref, wu_ref, wd_ref)


def _ffn_kernel(x_ref, g_ref, wg_ref, wu_ref, wd_ref, o_ref):
    o_ref[...] = _macaron_ffn(x_ref[...], g_ref, wg_ref, wu_ref, wd_ref)


def _ffn_call(x, g, wg, wu, wd, tm):
    m, d = x.shape
    f = wg.shape[1]
    row = pl.BlockSpec((tm, d), lambda i: (i, 0))
    limit = _vmem_limit(2 * _nbytes((tm, d), F32),
                        3 * _nbytes((d, f), BF16),
                        4 * _nbytes((tm, FFN_CHUNK), F32) + 2 * _nbytes((tm, d), F32))
    return pl.pallas_call(
        _ffn_kernel,
        grid=(m // tm,),
        in_specs=[row, _resident((1, d)), _resident((d, f)), _resident((d, f)), _resident((f, d))],
        out_specs=row,
        out_shape=jax.ShapeDtypeStruct((m, d), F32),
        compiler_params=pltpu.CompilerParams(dimension_semantics=("arbitrary",), vmem_limit_bytes=limit),
        name="ffn",
    )(x, g, wg, wu, wd)


def _rope(t, cos, sin_signed, first_half):
    outs = []
    for c in range(t.shape[1] // V7X_LANES):
        tc = t[:, c * V7X_LANES:(c + 1) * V7X_LANES]
        partner = jnp.where(first_half,
                            pltpu.roll(tc, V7X_LANES - HEAD_DIM // 2, 1),
                            pltpu.roll(tc, HEAD_DIM // 2, 1))
        outs.append(tc * cos + partner * sin_signed)
    return jnp.concatenate(outs, axis=1)


def _first_half_mask():
    lane = lax.broadcasted_iota(jnp.int32, (1, V7X_LANES), 1)
    return (lane % HEAD_DIM) < (HEAD_DIM // 2)


def _qkv_group(h, win_ref, g, cos, sin_signed, first_half):
    base = POOL_WIDTH + g * QKV_GROUP_WIDTH
    qkv = jnp.dot(h, win_ref[:, base:base + QKV_GROUP_WIDTH], preferred_element_type=F32)
    q = _rope(qkv[:, :GROUP_WIDTH], cos, sin_signed, first_half) * ATTN_SCALE
    k = _rope(qkv[:, GROUP_WIDTH:2 * GROUP_WIDTH], cos, sin_signed, first_half)
    return jnp.concatenate([q, k, qkv[:, 2 * GROUP_WIDTH:]], axis=1)


def _pool_branch(window_means_minus_tok, wgrp_ref, pscale_ref, wpbr_ref):
    mixed = [jnp.dot(p.astype(BF16), wgrp_ref[gi], preferred_element_type=F32)
             for gi, p in enumerate(window_means_minus_tok)]
    pool = jnp.concatenate(mixed, axis=1) * pscale_ref[...]
    return jnp.dot(pool.astype(BF16), wpbr_ref[...], preferred_element_type=F32)


def _gates(h, win_ref):
    d = win_ref.shape[0]
    base = POOL_WIDTH + len(ATTN_GROUPS) * QKV_GROUP_WIDTH
    gate_pool = jax.nn.sigmoid(jnp.dot(h, win_ref[:, base:base + d], preferred_element_type=F32))
    gate_attn = jax.nn.sigmoid(jnp.dot(h, win_ref[:, base + d:base + 2 * d], preferred_element_type=F32))
    return gate_pool, gate_attn


def _prompt_mix_in_kernel(x_ref, g_ref, win_ref, cos_ref, sin_ref, wgrp_ref, pscale_ref, wpbr_ref,
                          q0_ref, q1_ref, q2_ref, kv0_ref, kv1_ref, kv2_ref,
                          kvt0_ref, kvt1_ref, kvt2_ref, tail_ref, pp_ref, ga_ref, uext_ref, qkv_ref,
                          *, ts, seq_tiles):
    j = pl.program_id(1)
    h = _rmsnorm(x_ref[0], g_ref[...]).astype(BF16)

    u = jnp.dot(h, win_ref[:, :POOL_WIDTH], preferred_element_type=F32)

    @pl.when(j == 0)
    def _():
        uext_ref[0:POOL_CARRY, :] = jnp.zeros((POOL_CARRY, POOL_WIDTH), F32)

    @pl.when(j > 0)
    def _():
        uext_ref[0:POOL_CARRY, :] = uext_ref[ts:ts + POOL_CARRY, :]

    uext_ref[POOL_CARRY:POOL_CARRY + ts, :] = u
    pos = j * ts + lax.broadcasted_iota(jnp.int32, (ts, 1), 0)
    pooled = []
    for gi, w in enumerate(POOL_WINDOWS):
        lo, hi = gi * POOL_GROUP, (gi + 1) * POOL_GROUP
        tok = u[:, lo:hi]
        total = tok
        for back in range(1, w):
            total = total + uext_ref[POOL_CARRY - back:POOL_CARRY - back + ts, lo:hi]
        cnt = jnp.minimum(pos + 1, w).astype(F32)
        pooled.append(total / cnt - tok)
    pool_br = _pool_branch(pooled, wgrp_ref, pscale_ref, wpbr_ref)
    gate_pool, gate_attn = _gates(h, win_ref)
    pp_ref[0] = gate_pool * pool_br
    ga_ref[0] = gate_attn

    @pl.when(j == seq_tiles - 1)
    def _():
        tail_ref[0] = u[ts - POOL_CARRY:, :]

    cos = cos_ref[...]
    sin_signed = sin_ref[...]
    first_half = _first_half_mask()
    outs = ((q0_ref, kv0_ref, kvt0_ref), (q1_ref, kv1_ref, kvt1_ref), (q2_ref, kv2_ref, kvt2_ref))
    for g, (q_ref, kv_ref, kvt_ref) in enumerate(outs):
        dil = ATTN_GROUPS[g][1]
        qkv = _qkv_group(h, win_ref, g, cos, sin_signed, first_half)
        if dil == 1:
            q_ref[0] = qkv[:, :GROUP_WIDTH].astype(BF16)
            kv_ref[0] = qkv[:, GROUP_WIDTH:].astype(BF16)
        else:
            planes = QKV_GROUP_WIDTH // V7X_LANES
            for c in range(planes):
                qkv_ref[c] = qkv[:, c * V7X_LANES:(c + 1) * V7X_LANES]
            for r in range(dil):
                rows = jnp.concatenate([qkv_ref[c, pl.ds(r, ts // dil, stride=dil), :] for c in range(planes)],
                                       axis=1)
                q_ref[0, :, r * GROUP_WIDTH:(r + 1) * GROUP_WIDTH] = rows[:, :GROUP_WIDTH].astype(BF16)
                kv_ref[0, :, 2 * r * GROUP_WIDTH:2 * (r + 1) * GROUP_WIDTH] = rows[:, GROUP_WIDTH:].astype(BF16)
        keep = kvt_ref.shape[2]
        if keep == ts:
            kvt_ref[0] = qkv[:, GROUP_WIDTH:].T
        else:
            @pl.when(j == seq_tiles - 1)
            def _(kvt_ref=kvt_ref, qkv=qkv, keep=keep):
                kvt_ref[0] = qkv[ts - keep:, GROUP_WIDTH:].T


def _prompt_mix_in_call(x, g, win, cos, sin, wgrp, pscale, wpbr, ts):
    b, s, d = x.shape
    seq_tiles = s // ts
    zw = win.shape[1]

    def tile(width):
        return pl.BlockSpec((1, ts, width), lambda bi, j: (bi, j, 0))

    q_specs, kv_specs, q_shapes, kv_shapes, kvt_specs, kvt_shapes = [], [], [], [], [], []
    for window, dil in ATTN_GROUPS:
        q_specs.append(pl.BlockSpec((1, ts // dil, dil * GROUP_WIDTH), lambda bi, j: (bi, j, 0)))
        kv_specs.append(pl.BlockSpec((1, ts // dil, 2 * dil * GROUP_WIDTH), lambda bi, j: (bi, j, 0)))
        q_shapes.append(jax.ShapeDtypeStruct((b, s // dil, dil * GROUP_WIDTH), BF16))
        kv_shapes.append(jax.ShapeDtypeStruct((b, s // dil, 2 * dil * GROUP_WIDTH), BF16))
        keep = min(window, s)
        if keep >= ts:
            first = seq_tiles - keep // ts
            kvt_specs.append(pl.BlockSpec((1, 2 * GROUP_WIDTH, ts),
                                          lambda bi, j, first=first: (bi, 0, jnp.maximum(j - first, 0))))
        else:
            kvt_specs.append(pl.BlockSpec((1, 2 * GROUP_WIDTH, keep), lambda bi, j: (bi, 0, 0)))
        kvt_shapes.append(jax.ShapeDtypeStruct((b, 2 * GROUP_WIDTH, keep), F32))
    n_g = len(ATTN_GROUPS)
    out_specs = (q_specs + kv_specs + kvt_specs
                 + [pl.BlockSpec((1, POOL_CARRY, POOL_WIDTH), lambda bi, j: (bi, 0, 0)), tile(d), tile(d)])
    out_shape = (q_shapes + kv_shapes + kvt_shapes
                 + [jax.ShapeDtypeStruct((b, POOL_CARRY, POOL_WIDTH), F32),
                    jax.ShapeDtypeStruct((b, s, d), F32), jax.ShapeDtypeStruct((b, s, d), F32)])
    table = pl.BlockSpec((ts, V7X_LANES), lambda bi, j: (j, 0))
    pipelined = (3 * _nbytes((ts, d), F32) + n_g * _nbytes((ts, QKV_GROUP_WIDTH), BF16)
                 + n_g * _nbytes((ts, 2 * GROUP_WIDTH), F32) + 2 * _nbytes((ts, V7X_LANES), F32))
    resident = _nbytes((d, zw), BF16) + _nbytes(wpbr.shape, BF16) + _nbytes(wgrp.shape, BF16)
    scratch = [pltpu.VMEM((ts + POOL_CARRY, POOL_WIDTH), F32),
               pltpu.VMEM((QKV_GROUP_WIDTH // V7X_LANES, ts, V7X_LANES), F32)]
    temps = (_nbytes((ts + POOL_CARRY, POOL_WIDTH), F32) + _nbytes((ts, QKV_GROUP_WIDTH), F32)
             + 6 * _nbytes((ts, d), F32))
    return pl.pallas_call(
        functools.partial(_prompt_mix_in_kernel, ts=ts, seq_tiles=seq_tiles),
        grid=(b, seq_tiles),
        in_specs=[tile(d), _resident((1, d)), _resident((d, zw)), table, table,
                  _resident(wgrp.shape), _resident((1, POOL_WIDTH)), _resident(wpbr.shape)],
        out_specs=out_specs,
        out_shape=out_shape,
        scratch_shapes=scratch,
        compiler_params=pltpu.CompilerParams(dimension_semantics=("arbitrary", "arbitrary"),
                                             vmem_limit_bytes=_vmem_limit(pipelined, resident, temps)),
        name="prompt_mix_in",
    )(x, g, win, cos, sin, wgrp, pscale, wpbr)


def _head_lane_masks():
    lane = lax.broadcasted_iota(jnp.int32, (1, GROUP_WIDTH), 1)
    return [(lane // HEAD_DIM) == hd for hd in range(HEADS)]


def _per_head_columns_to_lanes(cols, head_masks):
    t = cols.shape[0] // HEADS
    out = jnp.broadcast_to(cols[(HEADS - 1) * t:], (t, GROUP_WIDTH))
    for hd in range(HEADS - 2, -1, -1):
        out = jnp.where(head_masks[hd], jnp.broadcast_to(cols[hd * t:(hd + 1) * t], (t, GROUP_WIDTH)), out)
    return out


def _band_block(q, k, v, mask, head_masks, head_ones):
    blk = q.shape[0]
    q_heads = jnp.concatenate([q * one for one in head_ones], axis=0)
    sc = lax.dot_general(q_heads, k, (((1,), (1,)), ((), ())), preferred_element_type=F32)
    sc = jnp.where(mask, sc, NEG_INF)
    mx = jnp.max(sc, axis=1, keepdims=True)
    e = jnp.exp(sc - mx)
    den = jnp.sum(e, axis=1, keepdims=True)
    eb = e.astype(BF16)
    e_cat = jnp.concatenate([eb[hd * blk:(hd + 1) * blk] for hd in range(HEADS)], axis=1)
    v_heads = jnp.concatenate([v * one for one in head_ones], axis=0)
    o_un = jnp.dot(e_cat, v_heads, preferred_element_type=F32)
    out = o_un / _per_head_columns_to_lanes(den, head_masks)
    lse = _per_head_columns_to_lanes(mx + jnp.log(den), head_masks)
    return out, lse


def _prompt_attn_kernel(q0_ref, q1_ref, q2_ref, kv0_ref, kv1_ref, kv2_ref, attn_ref, o_scr, l_scr, *, seq):
    blk = KEYS_PER_QUERY
    head_masks = _head_lane_masks()
    head_ones = [jnp.where(m, 1.0, 0.0).astype(BF16) for m in head_masks]
    qi = lax.broadcasted_iota(jnp.int32, (HEADS * blk, 1), 0) % blk
    first_mask = lax.broadcasted_iota(jnp.int32, (1, blk), 1) <= qi
    dist = qi + blk - lax.broadcasted_iota(jnp.int32, (1, 2 * blk), 1)
    band_mask = (dist >= 0) & (dist <= blk)
    refs = ((q0_ref, kv0_ref), (q1_ref, kv1_ref), (q2_ref, kv2_ref))

    def block(g, r, i):
        q_ref, kv_ref = refs[g]
        qc = slice(r * GROUP_WIDTH, (r + 1) * GROUP_WIDTH)
        kc = slice(2 * r * GROUP_WIDTH, (2 * r + 1) * GROUP_WIDTH)
        vc = slice((2 * r + 1) * GROUP_WIDTH, (2 * r + 2) * GROUP_WIDTH)
        krows = slice(max(i - 1, 0) * blk, (i + 1) * blk)
        return _band_block(q_ref[0, i * blk:(i + 1) * blk, qc], kv_ref[0, krows, kc], kv_ref[0, krows, vc],
                           first_mask if i == 0 else band_mask, head_masks, head_ones)

    undilated = [g for g, (_, dil) in enumerate(ATTN_GROUPS) if dil == 1]
    dilated = [g for g, (_, dil) in enumerate(ATTN_GROUPS) if dil != 1]
    assert len(undilated) == 1
    planes = GROUP_WIDTH // V7X_LANES
    for slot, g in enumerate(dilated):
        dil = ATTN_GROUPS[g][1]
        for r in range(dil):
            for i in range(seq // (dil * blk)):
                out, lse = block(g, r, i)
                rows = pl.ds(i * blk * dil + r, blk, stride=dil)
                for c in range(planes):
                    lanes = slice(c * V7X_LANES, (c + 1) * V7X_LANES)
                    o_scr[slot * planes + c, rows, :] = out[:, lanes]
                    l_scr[slot * planes + c, rows, :] = lse[:, lanes]
    for i in range(seq // blk):
        rows = slice(i * blk, (i + 1) * blk)
        out, lse = block(undilated[0], 0, i)
        outs, lses = [out], [lse]
        for slot in range(len(dilated)):
            outs.append(jnp.concatenate([o_scr[slot * planes + c, rows, :] for c in range(planes)], axis=1))
            lses.append(jnp.concatenate([l_scr[slot * planes + c, rows, :] for c in range(planes)], axis=1))
        attn_ref[0, rows, :] = _merge_groups(outs, lses).astype(BF16)


def _merge_groups(outs, lses):
    mx = functools.reduce(jnp.maximum, lses)
    ws = [jnp.exp(l - mx) for l in lses]
    num = functools.reduce(lambda a, b: a + b, [w * o for w, o in zip(ws, outs)])
    return num / functools.reduce(lambda a, b: a + b, ws)


def _prompt_attn_call(qs, kvs, seq):
    b = qs[0].shape[0]
    n_g = len(ATTN_GROUPS)

    def whole(a):
        return pl.BlockSpec((1,) + a.shape[1:], lambda bi: (bi, 0, 0))

    pipelined = n_g * _nbytes((seq, QKV_GROUP_WIDTH), BF16) + _nbytes((seq, GROUP_WIDTH), BF16)
    scratch_bytes = 2 * (n_g - 1) * _nbytes((seq, GROUP_WIDTH), F32)
    return pl.pallas_call(
        functools.partial(_prompt_attn_kernel, seq=seq),
        grid=(b,),
        in_specs=[whole(a) for a in qs] + [whole(a) for a in kvs],
        out_specs=pl.BlockSpec((1, seq, GROUP_WIDTH), lambda bi: (bi, 0, 0)),
        out_shape=jax.ShapeDtypeStruct((b, seq, GROUP_WIDTH), BF16),
        scratch_shapes=[pltpu.VMEM(((n_g - 1) * GROUP_WIDTH // V7X_LANES, seq, V7X_LANES), F32)] * 2,
        compiler_params=pltpu.CompilerParams(dimension_semantics=("arbitrary",),
                                             vmem_limit_bytes=_vmem_limit(pipelined, scratch_bytes, 24 << 20)),
        name="prompt_attn",
    )(*qs, *kvs)


def _mix_out_kernel(x_ref, pp_ref, ga_ref, attn_ref, wabr_ref, wo_ref, g2_ref, wg_ref, wu_ref, wd_ref, gf_ref,
                    y_ref):
    attn_br = jnp.dot(attn_ref[...], wabr_ref[...], preferred_element_type=F32)
    merged = pp_ref[...] + ga_ref[...] * attn_br
    x = x_ref[...] + jnp.dot(merged.astype(BF16), wo_ref[...], preferred_element_type=F32)
    x = _macaron_ffn(x, g2_ref, wg_ref, wu_ref, wd_ref)
    y_ref[...] = _rmsnorm(x, gf_ref[...])


def _mix_out_call(x, pp, ga, attn, wabr, wo, g2, wg, wu, wd, gf, tm):
    m, d = x.shape
    f = wg.shape[1]
    row = pl.BlockSpec((tm, d), lambda i: (i, 0))
    grp = pl.BlockSpec((tm, GROUP_WIDTH), lambda i: (i, 0))
    pipelined = 4 * _nbytes((tm, d), F32) + _nbytes((tm, GROUP_WIDTH), BF16)
    resident = 3 * _nbytes((d, f), BF16) + _nbytes((d, d), BF16) + _nbytes((GROUP_WIDTH, d), BF16)
    temps = 4 * _nbytes((tm, FFN_CHUNK), F32) + 4 * _nbytes((tm, d), F32)
    return pl.pallas_call(
        _mix_out_kernel,
        grid=(m // tm,),
        in_specs=[row, row, row, grp,
                  _resident((GROUP_WIDTH, d)), _resident((d, d)), _resident((1, d)),
                  _resident((d, f)), _resident((d, f)), _resident((f, d)), _resident((1, d))],
        out_specs=row,
        out_shape=jax.ShapeDtypeStruct((m, d), F32),
        compiler_params=pltpu.CompilerParams(dimension_semantics=("arbitrary",),
                                             vmem_limit_bytes=_vmem_limit(pipelined, resident, temps)),
        name="mix_out",
    )(x, pp, ga, attn, wabr, wo, g2, wg, wu, wd, gf)


def _sample_mix_in_kernel(x_ref, g_ref, win_ref, state_ref, cos_ref, sin_ref, wgrp_ref, pscale_ref, wpbr_ref,
                          qkv0_ref, qkv1_ref, qkv2_ref, u_ref, pp_ref, ga_ref):
    h = _rmsnorm(x_ref[...], g_ref[...]).astype(BF16)
    u = jnp.dot(h, win_ref[:, :POOL_WIDTH], preferred_element_type=F32)
    u_ref[...] = u
    pooled = []
    for gi, w in enumerate(POOL_WINDOWS):
        lo, hi = gi * POOL_GROUP, (gi + 1) * POOL_GROUP
        tok = u[:, lo:hi]
        total = tok
        for back in range(1, w):
            total = total + state_ref[POOL_STATE - back, :, lo:hi]
        pooled.append(total / float(min(PAST_LEN + 1, w)) - tok)
    pool_br = _pool_branch(pooled, wgrp_ref, pscale_ref, wpbr_ref)
    gate_pool, gate_attn = _gates(h, win_ref)
    pp_ref[...] = gate_pool * pool_br
    ga_ref[...] = gate_attn
    first_half = _first_half_mask()
    for g, qkv_ref in enumerate((qkv0_ref, qkv1_ref, qkv2_ref)):
        qkv_ref[...] = _qkv_group(h, win_ref, g, cos_ref[...], sin_ref[...], first_half)


def _sample_mix_in_call(x, g, win, state_t, cos, sin, wgrp, pscale, wpbr):
    n, d = x.shape
    n_g = len(ATTN_GROUPS)

    def full(shape):
        zeros = (0,) * len(shape)
        return pl.BlockSpec(shape, lambda i: zeros)

    out_shape = ([jax.ShapeDtypeStruct((n, QKV_GROUP_WIDTH), F32)] * n_g
                 + [jax.ShapeDtypeStruct((n, POOL_WIDTH), F32),
                    jax.ShapeDtypeStruct((n, d), F32), jax.ShapeDtypeStruct((n, d), F32)])
    ins = (x, g, win, state_t, cos, sin, wgrp, pscale, wpbr)
    resident = sum(_nbytes(a.shape, a.dtype) for a in ins)
    return pl.pallas_call(
        _sample_mix_in_kernel,
        grid=(1,),
        in_specs=[full(a.shape) for a in ins],
        out_specs=[full(o.shape) for o in out_shape],
        out_shape=out_shape,
        compiler_params=pltpu.CompilerParams(dimension_semantics=("arbitrary",),
                                             vmem_limit_bytes=_vmem_limit(resident, 0, 8 << 20)),
        name="sample_mix_in",
    )(*ins)


def _head_sums(x):
    return jnp.concatenate([jnp.sum(x[hd * HEAD_DIM:(hd + 1) * HEAD_DIM], axis=0, keepdims=True)
                            for hd in range(HEADS)], axis=0)


def _head_expand(y):
    return jnp.concatenate([jnp.broadcast_to(y[hd:hd + 1], (HEAD_DIM, y.shape[1])) for hd in range(HEADS)],
                           axis=0)


def _cache_step_kernel(qkv0_ref, qkv1_ref, qkv2_ref, c0_ref, c1_ref, c2_ref,
                       n0_ref, n1_ref, n2_ref, attn_ref):
    n = pl.program_id(0)
    outs, lses = [], []
    groups = ((qkv0_ref, c0_ref, n0_ref), (qkv1_ref, c1_ref, n1_ref), (qkv2_ref, c2_ref, n2_ref))
    for (qkv_ref, c_ref, new_ref), (window, dil) in zip(groups, ATTN_GROUPS):
        q = qkv_ref[0, :GROUP_WIDTH, :]
        k_new = qkv_ref[0, GROUP_WIDTH:2 * GROUP_WIDTH, :]
        v_new = qkv_ref[0, 2 * GROUP_WIDTH:, :]
        kt = c_ref[0, :GROUP_WIDTH, :]
        vt = c_ref[0, GROUP_WIDTH:, :]
        lane = lax.broadcasted_iota(jnp.int32, (1, window), 1)
        sc = jnp.where(lane % dil == 0, _head_sums(kt * q), NEG_INF)
        sc_new = _head_sums(k_new * q)
        mx = jnp.maximum(jnp.max(sc, axis=1, keepdims=True), sc_new)
        e = jnp.exp(sc - mx)
        e_new = jnp.exp(sc_new - mx)
        den = jnp.sum(e, axis=1, keepdims=True) + e_new
        o_un = jnp.sum(vt * _head_expand(e), axis=1, keepdims=True) + _head_expand(e_new) * v_new
        outs.append(o_un / _head_expand(den))
        lses.append(_head_expand(mx + jnp.log(den)))
        shifted = pltpu.roll(c_ref[0], window - 1, 1)
        new_ref[0] = jnp.where(lane == window - 1, qkv_ref[0, GROUP_WIDTH:, :], shifted)
    attn_col = _merge_groups(outs, lses)
    col = lax.broadcasted_iota(jnp.int32, attn_ref.shape, 1)

    @pl.when(n == 0)
    def _():
        attn_ref[...] = jnp.zeros(attn_ref.shape, F32)

    attn_ref[...] = jnp.where(col == n, attn_col, attn_ref[...])


def _cache_step_call(qkv_cols, windows_t):
    n = qkv_cols[0].shape[0]
    qspec = pl.BlockSpec((1, QKV_GROUP_WIDTH, 1), lambda i: (i, 0, 0))
    cspecs = [pl.BlockSpec((1,) + c.shape[1:], lambda i: (i, 0, 0)) for c in windows_t]
    pipelined = 2 * sum(_nbytes(c.shape[1:], F32) for c in windows_t) + 3 * _nbytes((QKV_GROUP_WIDTH, V7X_LANES), F32)
    res = pl.pallas_call(
        _cache_step_kernel,
        grid=(n,),
        in_specs=[qspec] * len(qkv_cols) + cspecs,
        out_specs=cspecs + [pl.BlockSpec((GROUP_WIDTH, n), lambda i: (0, 0))],
        out_shape=[jax.ShapeDtypeStruct(c.shape, F32) for c in windows_t]
                  + [jax.ShapeDtypeStruct((GROUP_WIDTH, n), F32)],
        compiler_params=pltpu.CompilerParams(dimension_semantics=("arbitrary",),
                                             vmem_limit_bytes=_vmem_limit(pipelined, 0, 16 << 20)),
        name="cache_step",
    )(*qkv_cols, *windows_t)
    return res[:-1], res[-1]


def _rope_tables(pos):
    half = HEAD_DIM // 2
    inv = ROPE_THETA ** (-jnp.arange(half, dtype=F32) * (2.0 / HEAD_DIM))
    ang = pos.astype(F32)[:, None] * inv[None, :]
    cos, sin = jnp.cos(ang), jnp.sin(ang)
    reps = V7X_LANES // HEAD_DIM
    return (jnp.tile(jnp.concatenate([cos, cos], axis=1), (1, reps)),
            jnp.tile(jnp.concatenate([-sin, sin], axis=1), (1, reps)))


def _window_to_lanes(w):
    n, width = w.shape[:2]
    return jnp.transpose(w, (0, 2, 3, 4, 1)).reshape(n, 2 * GROUP_WIDTH, width)


def _window_from_lanes(wt):
    n, _, width = wt.shape
    return jnp.transpose(wt.reshape(n, 2, HEADS, HEAD_DIM, width), (0, 4, 1, 2, 3))[None]


def kernel(x_prompt, x_sample, state_pool, cache_kv_w128, cache_kv_w512, cache_kv_w2048, norm_ffn1, ffn1_w_gate, ffn1_w_up, ffn1_w_down, norm_mix, w_in, w_pool_grp, pool_scale, w_pool_br, w_attn_br, w_o, norm_ffn2, ffn2_w_gate, ffn2_w_up, ffn2_w_down, norm_final):
    b, s, d = x_prompt.shape
    n, dec_seq, _ = x_sample.shape
    caches = (cache_kv_w128, cache_kv_w512, cache_kv_w2048)
    assert norm_ffn1.shape[0] == 1 and dec_seq == 1, "single layer, single decode token"
    ts = PROMPT_ROW_TILE
    assert s % ts == 0 and ts >= POOL_CARRY
    for cache, (window, dil) in zip(caches, ATTN_GROUPS):
        assert window // dil == KEYS_PER_QUERY and s % (dil * KEYS_PER_QUERY) == 0 and ts % (16 * dil) == 0
        assert cache.shape[2] == window, "decode path expects full windows of history"

    row = lambda v: v.reshape(1, -1)
    bf = lambda w: w.astype(BF16)
    g1, g2, gm, gf = row(norm_ffn1[0]), row(norm_ffn2[0]), row(norm_mix[0]), row(norm_final)
    f1 = (bf(ffn1_w_gate[0]), bf(ffn1_w_up[0]), bf(ffn1_w_down[0]))
    f2 = (bf(ffn2_w_gate[0]), bf(ffn2_w_up[0]), bf(ffn2_w_down[0]))
    win, wgrp, wpbr, wabr, wo = bf(w_in[0]), bf(w_pool_grp[0]), bf(w_pool_br[0]), bf(w_attn_br[0]), bf(w_o[0])
    pscale = row(pool_scale[0])

    cos_p, sin_p = _rope_tables(jnp.arange(s, dtype=jnp.int32))
    x1 = _ffn_call(x_prompt.reshape(b * s, d), g1, *f1, tm=ts)
    (q0, q1, q2, kv0, kv1, kv2, kvt0, kvt1, kvt2, tail, pp, ga) = _prompt_mix_in_call(
        x1.reshape(b, s, d), gm, win, cos_p, sin_p, wgrp, pscale, wpbr, ts)
    attn = _prompt_attn_call((q0, q1, q2), (kv0, kv1, kv2), s)
    y_prompt = _mix_out_call(x1, pp.reshape(b * s, d), ga.reshape(b * s, d), attn.reshape(b * s, GROUP_WIDTH),
                             wabr, wo, g2, *f2, gf, tm=ts).reshape(b, s, d)
    pool_prompt = tail[None, :, POOL_CARRY - POOL_STATE:, :]

    cos_s, sin_s = _rope_tables(PAST_LEN + jnp.arange(dec_seq, dtype=jnp.int32))
    xs1 = _ffn_call(x_sample.reshape(n, d), g1, *f1, tm=n)
    state_t = jnp.swapaxes(state_pool[0], 0, 1)
    (sqkv0, sqkv1, sqkv2, u_new, spp, sga) = _sample_mix_in_call(
        xs1, gm, win, state_t, cos_s, sin_s, wgrp, pscale, wpbr)
    new_windows, attn_t = _cache_step_call([a[:, :, None] for a in (sqkv0, sqkv1, sqkv2)],
                                           [_window_to_lanes(c[0]) for c in caches])
    y_sample = _mix_out_call(xs1, spp, sga, attn_t.T.astype(BF16), wabr, wo, g2, *f2, gf,
                             tm=n).reshape(n, dec_seq, d)
    pool_sample = jnp.swapaxes(jnp.concatenate([state_t[1:], u_new[None]], axis=0), 0, 1)[None]

    return (y_prompt, y_sample, pool_prompt,
            _window_from_lanes(kvt0), _window_from_lanes(kvt1), _window_from_lanes(kvt2),
            pool_sample, *[_window_from_lanes(w) for w in new_windows])
```

```python
import functools

import jax
import jax.numpy as jnp
from jax import lax
from jax.experimental import pallas as pl
from jax.experimental.pallas import tpu as pltpu

F32 = jnp.float32
BF16 = jnp.bfloat16

PAST_LEN = 16384
POOL_WINDOWS = (2, 4, 8, 16)
POOL_GROUP = 128
POOL_WIDTH = POOL_GROUP * len(POOL_WINDOWS)
POOL_STATE = max(POOL_WINDOWS) - 1
POOL_CARRY = 16
HEAD_DIM = 64
HEADS = 4
GROUP_WIDTH = HEADS * HEAD_DIM
ATTN_GROUPS = ((128, 1), (512, 4), (2048, 16))
KEYS_PER_QUERY = 128
QKV_GROUP_WIDTH = 3 * GROUP_WIDTH
ROPE_THETA = 10000.0
RMS_EPS = 1e-6
NEG_INF = -1e30
ATTN_SCALE = HEAD_DIM ** -0.5

V7X_VMEM_BYTES = 64 * 1024 * 1024
V7X_LANES = 128
V7X_MXU_COLS = 256

PROMPT_ROW_TILE = 512
FFN_CHUNK = 4 * V7X_MXU_COLS


def _vmem_limit(pipelined_bytes, resident_bytes, temp_bytes):
    need = 2 * pipelined_bytes + resident_bytes + temp_bytes
    return int(min(need, V7X_VMEM_BYTES - (4 << 20)))


def _nbytes(shape, dtype):
    n = 1
    for s in shape:
        n *= s
    return n * jnp.dtype(dtype).itemsize


def _resident(shape):
    zeros = (0,) * len(shape)
    return pl.BlockSpec(shape, lambda *_: zeros, pipeline_mode=pl.Buffered(1))


def _rmsnorm(x, g):
    ms = jnp.mean(x * x, axis=-1, keepdims=True)
    return x * lax.rsqrt(ms + RMS_EPS) * g


def _ffn_chunks(width):
    return [(lo, min(lo + FFN_CHUNK, width)) for lo in range(0, width, FFN_CHUNK)]


def _swiglu(xn, wg_ref, wu_ref, wd_ref):
    acc = None
    for lo, hi in _ffn_chunks(wg_ref.shape[1]):
        gate = jnp.dot(xn, wg_ref[:, lo:hi], preferred_element_type=F32)
        up = jnp.dot(xn, wu_ref[:, lo:hi], preferred_element_type=F32)
        hid = (gate * jax.nn.sigmoid(gate) * up).astype(BF16)
        part = jnp.dot(hid, wd_ref[lo:hi, :], preferred_element_type=F32)
        acc = part if acc is None else acc + part
    return acc


def _macaron_ffn(x, g_ref, wg_ref, wu_ref, wd_ref):
    xn = _rmsnorm(x, g_ref[...]).astype(BF16)
    return x + 0.5 * _swiglu(xn, wg_ref, wu_ref, wd_ref)


def _ffn_kernel(x_ref, g_ref, wg_ref, wu_ref, wd_ref, o_ref):
    o_ref[...] = _macaron_ffn(x_ref[...], g_ref, wg_ref, wu_ref, wd_ref)


def _ffn_call(x, g, wg, wu, wd, tm):
    m, d = x.shape
    f = wg.shape[1]
    row = pl.BlockSpec((tm, d), lambda i: (i, 0))
    limit = _vmem_limit(2 * _nbytes((tm, d), F32),
                        3 * _nbytes((d, f), BF16),
                        4 * _nbytes((tm, FFN_CHUNK), F32) + 2 * _nbytes((tm, d), F32))
    return pl.pallas_call(
        _ffn_kernel,
        grid=(m // tm,),
        in_specs=[row, _resident((1, d)), _resident((d, f)), _resident((d, f)), _resident((f, d))],
        out_specs=row,
        out_shape=jax.ShapeDtypeStruct((m, d), F32),
        compiler_params=pltpu.CompilerParams(dimension_semantics=("arbitrary",), vmem_limit_bytes=limit),
        name="ffn",
    )(x, g, wg, wu, wd)


def _rope(t, cos, sin_signed, first_half):
    outs = []
    for c in range(t.shape[1] // V7X_LANES):
        tc = t[:, c * V7X_LANES:(c + 1) * V7X_LANES]
        partner = jnp.where(first_half,
                            pltpu.roll(tc, V7X_LANES - HEAD_DIM // 2, 1),
                            pltpu.roll(tc, HEAD_DIM // 2, 1))
        outs.append(tc * cos + partner * sin_signed)
    return jnp.concatenate(outs, axis=1)


def _first_half_mask():
    lane = lax.broadcasted_iota(jnp.int32, (1, V7X_LANES), 1)
    return (lane % HEAD_DIM) < (HEAD_DIM // 2)


def _qkv_group(h, win_ref, g, cos, sin_signed, first_half):
    base = POOL_WIDTH + g * QKV_GROUP_WIDTH
    qkv = jnp.dot(h, win_ref[:, base:base + QKV_GROUP_WIDTH], preferred_element_type=F32)
    q = _rope(qkv[:, :GROUP_WIDTH], cos, sin_signed, first_half) * ATTN_SCALE
    k = _rope(qkv[:, GROUP_WIDTH:2 * GROUP_WIDTH], cos, sin_signed, first_half)
    return jnp.concatenate([q, k, qkv[:, 2 * GROUP_WIDTH:]], axis=1)


def _pool_branch(window_means_minus_tok, wgrp_ref, pscale_ref, wpbr_ref):
    mixed = [jnp.dot(p.astype(BF16), wgrp_ref[gi], preferred_element_type=F32)
             for gi, p in enumerate(window_means_minus_tok)]
    pool = jnp.concatenate(mixed, axis=1) * pscale_ref[...]
    return jnp.dot(pool.astype(BF16), wpbr_ref[...], preferred_element_type=F32)


def _gates(h, win_ref):
    d = win_ref.shape[0]
    base = POOL_WIDTH + len(ATTN_GROUPS) * QKV_GROUP_WIDTH
    gate_pool = jax.nn.sigmoid(jnp.dot(h, win_ref[:, base:base + d], preferred_element_type=F32))
    gate_attn = jax.nn.sigmoid(jnp.dot(h, win_ref[:, base + d:base + 2 * d], preferred_element_type=F32))
    return gate_pool, gate_attn


def _prompt_mix_in_kernel(x_ref, g_ref, win_ref, cos_ref, sin_ref, wgrp_ref, pscale_ref, wpbr_ref,
                          q0_ref, q1_ref, q2_ref, kv0_ref, kv1_ref, kv2_ref,
                          kvt0_ref, kvt1_ref, kvt2_ref, tail_ref, pp_ref, ga_ref, uext_ref, qkv_ref,
                          *, ts):
    j = pl.program_id(1)

    @pl.when(j == 0)
    def _():
        uext_ref[0:POOL_CARRY, :] = jnp.zeros((POOL_CARRY, POOL_WIDTH), F32)

    h = _rmsnorm(x_ref[0], g_ref[...]).astype(BF16)

    u = jnp.dot(h, win_ref[:, :POOL_WIDTH], preferred_element_type=F32)
    uext_ref[POOL_CARRY:POOL_CARRY + ts, :] = u
    pos = j * ts + lax.broadcasted_iota(jnp.int32, (ts, 1), 0)
    pooled = []
    for gi, w in enumerate(POOL_WINDOWS):
        lo, hi = gi * POOL_GROUP, (gi + 1) * POOL_GROUP
        tok = u[:, lo:hi]
        total = tok
        for back in range(1, w):
            total = total + uext_ref[POOL_CARRY - back:POOL_CARRY - back + ts, lo:hi]
        cnt = jnp.minimum(pos + 1, w).astype(F32)
        pooled.append(total / cnt - tok)
    pool_br = _pool_branch(pooled, wgrp_ref, pscale_ref, wpbr_ref)
    gate_pool, gate_attn = _gates(h, win_ref)
    pp_ref[0] = gate_pool * pool_br
    ga_ref[0] = gate_attn
    uext_ref[0:POOL_CARRY, :] = u[ts - POOL_CARRY:, :]
    tail_ref[0] = u[ts - POOL_CARRY:, :]

    cos = cos_ref[...]
    sin_signed = sin_ref[...]
    first_half = _first_half_mask()
    outs = ((q0_ref, kv0_ref, kvt0_ref), (q1_ref, kv1_ref, kvt1_ref), (q2_ref, kv2_ref, kvt2_ref))
    for g, (q_ref, kv_ref, kvt_ref) in enumerate(outs):
        dil = ATTN_GROUPS[g][1]
        qkv = _qkv_group(h, win_ref, g, cos, sin_signed, first_half)
        if dil == 1:
            q_ref[0] = qkv[:, :GROUP_WIDTH].astype(BF16)
            kv_ref[0] = qkv[:, GROUP_WIDTH:].astype(BF16)
        else:
            planes = QKV_GROUP_WIDTH // V7X_LANES
            for c in range(planes):
                qkv_ref[c] = qkv[:, c * V7X_LANES:(c + 1) * V7X_LANES]
            for r in range(dil):
                rows = jnp.concatenate([qkv_ref[c, pl.ds(r, ts // dil, stride=dil), :] for c in range(planes)],
                                       axis=1)
                q_ref[0, :, r * GROUP_WIDTH:(r + 1) * GROUP_WIDTH] = rows[:, :GROUP_WIDTH].astype(BF16)
                kv_ref[0, :, 2 * r * GROUP_WIDTH:2 * (r + 1) * GROUP_WIDTH] = rows[:, GROUP_WIDTH:].astype(BF16)
        keep = kvt_ref.shape[2]
        kvt_ref[0] = qkv[ts - keep:, GROUP_WIDTH:].T


def _prompt_mix_in_call(x, g, win, cos, sin, wgrp, pscale, wpbr, ts):
    b, s, d = x.shape
    seq_tiles = s // ts
    zw = win.shape[1]

    def tile(width):
        return pl.BlockSpec((1, ts, width), lambda bi, j: (bi, j, 0))

    q_specs, kv_specs, q_shapes, kv_shapes, kvt_specs, kvt_shapes = [], [], [], [], [], []
    for window, dil in ATTN_GROUPS:
        q_specs.append(pl.BlockSpec((1, ts // dil, dil * GROUP_WIDTH), lambda bi, j: (bi, j, 0)))
        kv_specs.append(pl.BlockSpec((1, ts // dil, 2 * dil * GROUP_WIDTH), lambda bi, j: (bi, j, 0)))
        q_shapes.append(jax.ShapeDtypeStruct((b, s // dil, dil * GROUP_WIDTH), BF16))
        kv_shapes.append(jax.ShapeDtypeStruct((b, s // dil, 2 * dil * GROUP_WIDTH), BF16))
        keep = min(window, s)
        if keep >= ts:
            first = seq_tiles - keep // ts
            kvt_specs.append(pl.BlockSpec((1, 2 * GROUP_WIDTH, ts),
                                          lambda bi, j, first=first: (bi, 0, jnp.maximum(j - first, 0))))
        else:
            kvt_specs.append(pl.BlockSpec((1, 2 * GROUP_WIDTH, keep), lambda bi, j: (bi, 0, 0)))
        kvt_shapes.append(jax.ShapeDtypeStruct((b, 2 * GROUP_WIDTH, keep), F32))
    n_g = len(ATTN_GROUPS)
    out_specs = (q_specs + kv_specs + kvt_specs
                 + [pl.BlockSpec((1, POOL_CARRY, POOL_WIDTH), lambda bi, j: (bi, 0, 0)), tile(d), tile(d)])
    out_shape = (q_shapes + kv_shapes + kvt_shapes
                 + [jax.ShapeDtypeStruct((b, POOL_CARRY, POOL_WIDTH), F32),
                    jax.ShapeDtypeStruct((b, s, d), F32), jax.ShapeDtypeStruct((b, s, d), F32)])
    table = pl.BlockSpec((ts, V7X_LANES), lambda bi, j: (j, 0))
    pipelined = (3 * _nbytes((ts, d), F32) + n_g * _nbytes((ts, QKV_GROUP_WIDTH), BF16)
                 + n_g * _nbytes((ts, 2 * GROUP_WIDTH), F32) + 2 * _nbytes((ts, V7X_LANES), F32))
    resident = _nbytes((d, zw), BF16) + _nbytes(wpbr.shape, BF16) + _nbytes(wgrp.shape, BF16)
    scratch = [pltpu.VMEM((ts + POOL_CARRY, POOL_WIDTH), F32),
               pltpu.VMEM((QKV_GROUP_WIDTH // V7X_LANES, ts, V7X_LANES), F32)]
    temps = (_nbytes((ts + POOL_CARRY, POOL_WIDTH), F32) + _nbytes((ts, QKV_GROUP_WIDTH), F32)
             + 6 * _nbytes((ts, d), F32))
    return pl.pallas_call(
        functools.partial(_prompt_mix_in_kernel, ts=ts),
        grid=(b, seq_tiles),
        in_specs=[tile(d), _resident((1, d)), _resident((d, zw)), table, table,
                  _resident(wgrp.shape), _resident((1, POOL_WIDTH)), _resident(wpbr.shape)],
        out_specs=out_specs,
        out_shape=out_shape,
        scratch_shapes=scratch,
        compiler_params=pltpu.CompilerParams(dimension_semantics=("arbitrary", "arbitrary"),
                                             vmem_limit_bytes=_vmem_limit(pipelined, resident, temps)),
        name="prompt_mix_in",
    )(x, g, win, cos, sin, wgrp, pscale, wpbr)


def _head_lane_masks():
    lane = lax.broadcasted_iota(jnp.int32, (1, GROUP_WIDTH), 1)
    return [(lane // HEAD_DIM) == hd for hd in range(HEADS)]


def _per_head_columns_to_lanes(cols, head_masks):
    t = cols.shape[0] // HEADS
    out = jnp.broadcast_to(cols[(HEADS - 1) * t:], (t, GROUP_WIDTH))
    for hd in range(HEADS - 2, -1, -1):
        out = jnp.where(head_masks[hd], jnp.broadcast_to(cols[hd * t:(hd + 1) * t], (t, GROUP_WIDTH)), out)
    return out


def _band_block(q, k, v, mask, head_masks, head_ones):
    blk = q.shape[0]
    q_heads = jnp.concatenate([q * one for one in head_ones], axis=0)
    sc = lax.dot_general(q_heads, k, (((1,), (1,)), ((), ())), preferred_element_type=F32)
    sc = jnp.where(mask, sc, NEG_INF)
    mx = jnp.max(sc, axis=1, keepdims=True)
    e = jnp.exp(sc - mx)
    den = jnp.sum(e, axis=1, keepdims=True)
    eb = e.astype(BF16)
    e_cat = jnp.concatenate([eb[hd * blk:(hd + 1) * blk] for hd in range(HEADS)], axis=1)
    v_heads = jnp.concatenate([v * one for one in head_ones], axis=0)
    o_un = jnp.dot(e_cat, v_heads, preferred_element_type=F32)
    out = o_un / _per_head_columns_to_lanes(den, head_masks)
    lse = _per_head_columns_to_lanes(mx + jnp.log(den), head_masks)
    return out, lse


def _prompt_attn_kernel(q0_ref, q1_ref, q2_ref, kv0_ref, kv1_ref, kv2_ref, attn_ref, o_scr, l_scr, *, seq):
    blk = KEYS_PER_QUERY
    head_masks = _head_lane_masks()
    head_ones = [jnp.where(m, 1.0, 0.0).astype(BF16) for m in head_masks]
    qi = lax.broadcasted_iota(jnp.int32, (HEADS * blk, 1), 0) % blk
    first_mask = lax.broadcasted_iota(jnp.int32, (1, blk), 1) <= qi
    dist = qi + blk - lax.broadcasted_iota(jnp.int32, (1, 2 * blk), 1)
    band_mask = (dist >= 0) & (dist <= blk)
    refs = ((q0_ref, kv0_ref), (q1_ref, kv1_ref), (q2_ref, kv2_ref))

    def block(g, r, i):
        q_ref, kv_ref = refs[g]
        qc = slice(r * GROUP_WIDTH, (r + 1) * GROUP_WIDTH)
        kc = slice(2 * r * GROUP_WIDTH, (2 * r + 1) * GROUP_WIDTH)
        vc = slice((2 * r + 1) * GROUP_WIDTH, (2 * r + 2) * GROUP_WIDTH)
        krows = slice(max(i - 1, 0) * blk, (i + 1) * blk)
        return _band_block(q_ref[0, i * blk:(i + 1) * blk, qc], kv_ref[0, krows, kc], kv_ref[0, krows, vc],
                           first_mask if i == 0 else band_mask, head_masks, head_ones)

    undilated = [g for g, (_, dil) in enumerate(ATTN_GROUPS) if dil == 1]
    dilated = [g for g, (_, dil) in enumerate(ATTN_GROUPS) if dil != 1]
    assert len(undilated) == 1
    planes = GROUP_WIDTH // V7X_LANES
    for slot, g in enumerate(dilated):
        dil = ATTN_GROUPS[g][1]
        for r in range(dil):
            for i in range(seq // (dil * blk)):
                out, lse = block(g, r, i)
                rows = pl.ds(i * blk * dil + r, blk, stride=dil)
                for c in range(planes):
                    lanes = slice(c * V7X_LANES, (c + 1) * V7X_LANES)
                    o_scr[slot * planes + c, rows, :] = out[:, lanes]
                    l_scr[slot * planes + c, rows, :] = lse[:, lanes]
    for i in range(seq // blk):
        rows = slice(i * blk, (i + 1) * blk)
        out, lse = block(undilated[0], 0, i)
        outs, lses = [out], [lse]
        for slot in range(len(dilated)):
            outs.append(jnp.concatenate([o_scr[slot * planes + c, rows, :] for c in range(planes)], axis=1))
            lses.append(jnp.concatenate([l_scr[slot * planes + c, rows, :] for c in range(planes)], axis=1))
        attn_ref[0, rows, :] = _merge_groups(outs, lses).astype(BF16)


def _merge_groups(outs, lses):
    mx = functools.reduce(jnp.maximum, lses)
    ws = [jnp.exp(l - mx) for l in lses]
    num = functools.reduce(lambda a, b: a + b, [w * o for w, o in zip(ws, outs)])
    return num / functools.reduce(lambda a, b: a + b, ws)


def _prompt_attn_call(qs, kvs, seq):
    b = qs[0].shape[0]
    n_g = len(ATTN_GROUPS)

    def whole(a):
        return pl.BlockSpec((1,) + a.shape[1:], lambda bi: (bi, 0, 0))

    pipelined = n_g * _nbytes((seq, QKV_GROUP_WIDTH), BF16) + _nbytes((seq, GROUP_WIDTH), BF16)
    scratch_bytes = 2 * (n_g - 1) * _nbytes((seq, GROUP_WIDTH), F32)
    return pl.pallas_call(
        functools.partial(_prompt_attn_kernel, seq=seq),
        grid=(b,),
        in_specs=[whole(a) for a in qs] + [whole(a) for a in kvs],
        out_specs=pl.BlockSpec((1, seq, GROUP_WIDTH), lambda bi: (bi, 0, 0)),
        out_shape=jax.ShapeDtypeStruct((b, seq, GROUP_WIDTH), BF16),
        scratch_shapes=[pltpu.VMEM(((n_g - 1) * GROUP_WIDTH // V7X_LANES, seq, V7X_LANES), F32)] * 2,
        compiler_params=pltpu.CompilerParams(dimension_semantics=("arbitrary",),
                                             vmem_limit_bytes=_vmem_limit(pipelined, scratch_bytes, 24 << 20)),
        name="prompt_attn",
    )(*qs, *kvs)


def _mix_out_kernel(x_ref, pp_ref, ga_ref, attn_ref, wabr_ref, wo_ref, g2_ref, wg_ref, wu_ref, wd_ref, gf_ref,
                    y_ref):
    attn_br = jnp.dot(attn_ref[...], wabr_ref[...], preferred_element_type=F32)
    merged = pp_ref[...] + ga_ref[...] * attn_br
    x = x_ref[...] + jnp.dot(merged.astype(BF16), wo_ref[...], preferred_element_type=F32)
    x = _macaron_ffn(x, g2_ref, wg_ref, wu_ref, wd_ref)
    y_ref[...] = _rmsnorm(x, gf_ref[...])


def _mix_out_call(x, pp, ga, attn, wabr, wo, g2, wg, wu, wd, gf, tm):
    m, d = x.shape
    f = wg.shape[1]
    row = pl.BlockSpec((tm, d), lambda i: (i, 0))
    grp = pl.BlockSpec((tm, GROUP_WIDTH), lambda i: (i, 0))
    pipelined = 4 * _nbytes((tm, d), F32) + _nbytes((tm, GROUP_WIDTH), BF16)
    resident = 3 * _nbytes((d, f), BF16) + _nbytes((d, d), BF16) + _nbytes((GROUP_WIDTH, d), BF16)
    temps = 4 * _nbytes((tm, FFN_CHUNK), F32) + 4 * _nbytes((tm, d), F32)
    return pl.pallas_call(
        _mix_out_kernel,
        grid=(m // tm,),
        in_specs=[row, row, row, grp,
                  _resident((GROUP_WIDTH, d)), _resident((d, d)), _resident((1, d)),
                  _resident((d, f)), _resident((d, f)), _resident((f, d)), _resident((1, d))],
        out_specs=row,
        out_shape=jax.ShapeDtypeStruct((m, d), F32),
        compiler_params=pltpu.CompilerParams(dimension_semantics=("arbitrary",),
                                             vmem_limit_bytes=_vmem_limit(pipelined, resident, temps)),
        name="mix_out",
    )(x, pp, ga, attn, wabr, wo, g2, wg, wu, wd, gf)


def _sample_mix_in_kernel(x_ref, g_ref, win_ref, state_ref, cos_ref, sin_ref, wgrp_ref, pscale_ref, wpbr_ref,
                          qkv0_ref, qkv1_ref, qkv2_ref, u_ref, pp_ref, ga_ref):
    h = _rmsnorm(x_ref[...], g_ref[...]).astype(BF16)
    u = jnp.dot(h, win_ref[:, :POOL_WIDTH], preferred_element_type=F32)
    u_ref[...] = u
    pooled = []
    for gi, w in enumerate(POOL_WINDOWS):
        lo, hi = gi * POOL_GROUP, (gi + 1) * POOL_GROUP
        tok = u[:, lo:hi]
        total = tok
        for back in range(1, w):
            total = total + state_ref[POOL_STATE - back, :, lo:hi]
        pooled.append(total / float(min(PAST_LEN + 1, w)) - tok)
    pool_br = _pool_branch(pooled, wgrp_ref, pscale_ref, wpbr_ref)
    gate_pool, gate_attn = _gates(h, win_ref)
    pp_ref[...] = gate_pool * pool_br
    ga_ref[...] = gate_attn
    first_half = _first_half_mask()
    for g, qkv_ref in enumerate((qkv0_ref, qkv1_ref, qkv2_ref)):
        qkv_ref[...] = _qkv_group(h, win_ref, g, cos_ref[...], sin_ref[...], first_half)


def _sample_mix_in_call(x, g, win, state_t, cos, sin, wgrp, pscale, wpbr):
    n, d = x.shape
    n_g = len(ATTN_GROUPS)

    def full(shape):
        zeros = (0,) * len(shape)
        return pl.BlockSpec(shape, lambda i: zeros)

    out_shape = ([jax.ShapeDtypeStruct((n, QKV_GROUP_WIDTH), F32)] * n_g
                 + [jax.ShapeDtypeStruct((n, POOL_WIDTH), F32),
                    jax.ShapeDtypeStruct((n, d), F32), jax.ShapeDtypeStruct((n, d), F32)])
    ins = (x, g, win, state_t, cos, sin, wgrp, pscale, wpbr)
    resident = sum(_nbytes(a.shape, a.dtype) for a in ins)
    return pl.pallas_call(
        _sample_mix_in_kernel,
        grid=(1,),
        in_specs=[full(a.shape) for a in ins],
        out_specs=[full(o.shape) for o in out_shape],
        out_shape=out_shape,
        compiler_params=pltpu.CompilerParams(dimension_semantics=("arbitrary",),
                                             vmem_limit_bytes=_vmem_limit(resident, 0, 8 << 20)),
        name="sample_mix_in",
    )(*ins)


def _head_sums(x):
    return jnp.concatenate([jnp.sum(x[hd * HEAD_DIM:(hd + 1) * HEAD_DIM], axis=0, keepdims=True)
                            for hd in range(HEADS)], axis=0)


def _head_expand(y):
    return jnp.concatenate([jnp.broadcast_to(y[hd:hd + 1], (HEAD_DIM, y.shape[1])) for hd in range(HEADS)],
                           axis=0)


def _cache_step_kernel(qkv0_ref, qkv1_ref, qkv2_ref, c0_ref, c1_ref, c2_ref,
                       n0_ref, n1_ref, n2_ref, attn_ref):
    n = pl.program_id(0)
    outs, lses = [], []
    groups = ((qkv0_ref, c0_ref, n0_ref), (qkv1_ref, c1_ref, n1_ref), (qkv2_ref, c2_ref, n2_ref))
    for (qkv_ref, c_ref, new_ref), (window, dil) in zip(groups, ATTN_GROUPS):
        q = qkv_ref[0, :GROUP_WIDTH, :]
        k_new = qkv_ref[0, GROUP_WIDTH:2 * GROUP_WIDTH, :]
        v_new = qkv_ref[0, 2 * GROUP_WIDTH:, :]
        kt = c_ref[0, :GROUP_WIDTH, :]
        vt = c_ref[0, GROUP_WIDTH:, :]
        lane = lax.broadcasted_iota(jnp.int32, (1, window), 1)
        sc = jnp.where(lane % dil == 0, _head_sums(kt * q), NEG_INF)
        sc_new = _head_sums(k_new * q)
        mx = jnp.maximum(jnp.max(sc, axis=1, keepdims=True), sc_new)
        e = jnp.exp(sc - mx)
        e_new = jnp.exp(sc_new - mx)
        den = jnp.sum(e, axis=1, keepdims=True) + e_new
        o_un = jnp.sum(vt * _head_expand(e), axis=1, keepdims=True) + _head_expand(e_new) * v_new
        outs.append(o_un / _head_expand(den))
        lses.append(_head_expand(mx + jnp.log(den)))
        shifted = pltpu.roll(c_ref[0], window - 1, 1)
        new_ref[0] = jnp.where(lane == window - 1, qkv_ref[0, GROUP_WIDTH:, :], shifted)
    attn_col = _merge_groups(outs, lses)
    col = lax.broadcasted_iota(jnp.int32, attn_ref.shape, 1)

    @pl.when(n == 0)
    def _():
        attn_ref[...] = jnp.zeros(attn_ref.shape, F32)

    attn_ref[...] = jnp.where(col == n, attn_col, attn_ref[...])


def _cache_step_call(qkv_cols, windows_t):
    n = qkv_cols[0].shape[0]
    qspec = pl.BlockSpec((1, QKV_GROUP_WIDTH, 1), lambda i: (i, 0, 0))
    cspecs = [pl.BlockSpec((1,) + c.shape[1:], lambda i: (i, 0, 0)) for c in windows_t]
    pipelined = 2 * sum(_nbytes(c.shape[1:], F32) for c in windows_t) + 3 * _nbytes((QKV_GROUP_WIDTH, V7X_LANES), F32)
    res = pl.pallas_call(
        _cache_step_kernel,
        grid=(n,),
        in_specs=[qspec] * len(qkv_cols) + cspecs,
        out_specs=cspecs + [pl.BlockSpec((GROUP_WIDTH, n), lambda i: (0, 0))],
        out_shape=[jax.ShapeDtypeStruct(c.shape, F32) for c in windows_t]
                  + [jax.ShapeDtypeStruct((GROUP_WIDTH, n), F32)],
        compiler_params=pltpu.CompilerParams(dimension_semantics=("arbitrary",),
                                             vmem_limit_bytes=_vmem_limit(pipelined, 0, 16 << 20)),
        name="cache_step",
    )(*qkv_cols, *windows_t)
    return res[:-1], res[-1]


def _rope_tables(pos):
    half = HEAD_DIM // 2
    inv = ROPE_THETA ** (-jnp.arange(half, dtype=F32) * (2.0 / HEAD_DIM))
    ang = pos.astype(F32)[:, None] * inv[None, :]
    cos, sin = jnp.cos(ang), jnp.sin(ang)
    reps = V7X_LANES // HEAD_DIM
    return (jnp.tile(jnp.concatenate([cos, cos], axis=1), (1, reps)),
            jnp.tile(jnp.concatenate([-sin, sin], axis=1), (1, reps)))


def _window_to_lanes(w):
    n, width = w.shape[:2]
    return jnp.transpose(w, (0, 2, 3, 4, 1)).reshape(n, 2 * GROUP_WIDTH, width)


def _window_from_lanes(wt):
    n, _, width = wt.shape
    return jnp.transpose(wt.reshape(n, 2, HEADS, HEAD_DIM, width), (0, 4, 1, 2, 3))[None]


def kernel(x_prompt, x_sample, state_pool, cache_kv_w128, cache_kv_w512, cache_kv_w2048, norm_ffn1, ffn1_w_gate, ffn1_w_up, ffn1_w_down, norm_mix, w_in, w_pool_grp, pool_scale, w_pool_br, w_attn_br, w_o, norm_ffn2, ffn2_w_gate, ffn2_w_up, ffn2_w_down, norm_final):
    b, s, d = x_prompt.shape
    n, dec_seq, _ = x_sample.shape
    caches = (cache_kv_w128, cache_kv_w512, cache_kv_w2048)
    assert norm_ffn1.shape[0] == 1 and dec_seq == 1, "single layer, single decode token"
    ts = PROMPT_ROW_TILE
    assert s % ts == 0 and ts >= POOL_CARRY
    for cache, (window, dil) in zip(caches, ATTN_GROUPS):
        assert window // dil == KEYS_PER_QUERY and s % (dil * KEYS_PER_QUERY) == 0 and ts % (16 * dil) == 0
        assert cache.shape[2] == window, "decode path expects full windows of history"

    row = lambda v: v.reshape(1, -1)
    bf = lambda w: w.astype(BF16)
    g1, g2, gm, gf = row(norm_ffn1[0]), row(norm_ffn2[0]), row(norm_mix[0]), row(norm_final)
    f1 = (bf(ffn1_w_gate[0]), bf(ffn1_w_up[0]), bf(ffn1_w_down[0]))
    f2 = (bf(ffn2_w_gate[0]), bf(ffn2_w_up[0]), bf(ffn2_w_down[0]))
    win, wgrp, wpbr, wabr, wo = bf(w_in[0]), bf(w_pool_grp[0]), bf(w_pool_br[0]), bf(w_attn_br[0]), bf(w_o[0])
    pscale = row(pool_scale[0])

    cos_p, sin_p = _rope_tables(jnp.arange(s, dtype=jnp.int32))
    x1 = _ffn_call(x_prompt.reshape(b * s, d), g1, *f1, tm=ts)
    (q0, q1, q2, kv0, kv1, kv2, kvt0, kvt1, kvt2, tail, pp, ga) = _prompt_mix_in_call(
        x1.reshape(b, s, d), gm, win, cos_p, sin_p, wgrp, pscale, wpbr, ts)
    attn = _prompt_attn_call((q0, q1, q2), (kv0, kv1, kv2), s)
    y_prompt = _mix_out_call(x1, pp.reshape(b * s, d), ga.reshape(b * s, d), attn.reshape(b * s, GROUP_WIDTH),
                             wabr, wo, g2, *f2, gf, tm=ts).reshape(b, s, d)
    pool_prompt = tail[None, :, POOL_CARRY - POOL_STATE:, :]

    cos_s, sin_s = _rope_tables(PAST_LEN + jnp.arange(dec_seq, dtype=jnp.int32))
    xs1 = _ffn_call(x_sample.reshape(n, d), g1, *f1, tm=n)
    state_t = jnp.swapaxes(state_pool[0], 0, 1)
    (sqkv0, sqkv1, sqkv2, u_new, spp, sga) = _sample_mix_in_call(
        xs1, gm, win, state_t, cos_s, sin_s, wgrp, pscale, wpbr)
    new_windows, attn_t = _cache_step_call([a[:, :, None] for a in (sqkv0, sqkv1, sqkv2)],
                                           [_window_to_lanes(c[0]) for c in caches])
    y_sample = _mix_out_call(xs1, spp, sga, attn_t.T.astype(BF16), wabr, wo, g2, *f2, gf,
                             tm=n).reshape(n, dec_seq, d)
    pool_sample = jnp.swapaxes(jnp.concatenate([state_t[1:], u_new[None]], axis=0), 0, 1)[None]

    return (y_prompt, y_sample, pool_prompt,
            _window_from_lanes(kvt0), _window_from_lanes(kvt1), _window_from_lanes(kvt2),
            pool_sample, *[_window_from_lanes(w) for w in new_windows])
```

```python
import functools

import jax
import jax.numpy as jnp
from jax import lax
from jax.experimental import pallas as pl
from jax.experimental.pallas import tpu as pltpu

F32 = jnp.float32
BF16 = jnp.bfloat16

PAST_LEN = 16384
POOL_WINDOWS = (2, 4, 8, 16)
POOL_GROUP = 128
POOL_WIDTH = POOL_GROUP * len(POOL_WINDOWS)
POOL_STATE = max(POOL_WINDOWS) - 1
POOL_CARRY = 16
HEAD_DIM = 64
HEADS = 4
GROUP_WIDTH = HEADS * HEAD_DIM
ATTN_GROUPS = ((128, 1), (512, 4), (2048, 16))
KEYS_PER_QUERY = 128
QKV_GROUP_WIDTH = 3 * GROUP_WIDTH
ROPE_THETA = 10000.0
RMS_EPS = 1e-6
NEG_INF = -1e30
ATTN_SCALE = HEAD_DIM ** -0.5
LOG2_E = 1.4426950408889634

V7X_VMEM_BYTES = 64 * 1024 * 1024
V7X_LANES = 128
V7X_MXU_COLS = 256

PROMPT_ROW_TILE = 512
FFN_ROW_TILE = 1024
FFN_CHUNK = 4 * V7X_MXU_COLS


def _vmem_limit(pipelined_bytes, resident_bytes, temp_bytes):
    need = 2 * pipelined_bytes + resident_bytes + temp_bytes
    return int(min(need, V7X_VMEM_BYTES - (4 << 20)))


def _nbytes(shape, dtype):
    n = 1
    for s in shape:
        n *= s
    return n * jnp.dtype(dtype).itemsize


def _resident(shape):
    zeros = (0,) * len(shape)
    return pl.BlockSpec(shape, lambda *_: zeros, pipeline_mode=pl.Buffered(1))


def _rmsnorm(x, g):
    ms = jnp.mean(x * x, axis=-1, keepdims=True)
    return x * lax.rsqrt(ms + RMS_EPS) * g


def _ffn_chunks(width):
    return [(lo, min(lo + FFN_CHUNK, width)) for lo in range(0, width, FFN_CHUNK)]


def _swiglu(xn, wg_ref, wu_ref, wd_ref):
    acc = None
    for lo, hi in _ffn_chunks(wg_ref.shape[1]):
        gate = jnp.dot(xn, wg_ref[:, lo:hi], preferred_element_type=F32)
        up = jnp.dot(xn, wu_ref[:, lo:hi], preferred_element_type=F32)
        hid = (gate * jax.nn.sigmoid(gate) * up).astype(BF16)
        part = jnp.dot(hid, wd_ref[lo:hi, :], preferred_element_type=F32)
        acc = part if acc is None else acc + part
    return acc


def _macaron_ffn(x, g_ref, wg_ref, wu_ref, wd_ref):
    xn = _rmsnorm(x, g_ref[...]).astype(BF16)
    return x + 0.5 * _swiglu(xn, wg_ref, wu_ref, wd_ref)


def _ffn_kernel(x_ref, g_ref, wg_ref, wu_ref, wd_ref, o_ref, *, chains):
    rows = x_ref.shape[0] // chains
    for c in range(chains):
        o_ref[c * rows:(c + 1) * rows, :] = _macaron_ffn(x_ref[c * rows:(c + 1) * rows, :],
                                                         g_ref, wg_ref, wu_ref, wd_ref)


def _ffn_call(x, g, wg, wu, wd, tm, chains=1):
    m, d = x.shape
    f = wg.shape[1]
    row = pl.BlockSpec((tm, d), lambda i: (i, 0))
    limit = _vmem_limit(2 * _nbytes((tm, d), F32),
                        3 * _nbytes((d, f), BF16),
                        4 * _nbytes((tm, FFN_CHUNK), F32) + 2 * _nbytes((tm, d), F32))
    return pl.pallas_call(
        functools.partial(_ffn_kernel, chains=chains),
        grid=(m // tm,),
        in_specs=[row, _resident((1, d)), _resident((d, f)), _resident((d, f)), _resident((f, d))],
        out_specs=row,
        out_shape=jax.ShapeDtypeStruct((m, d), F32),
        compiler_params=pltpu.CompilerParams(dimension_semantics=("arbitrary",), vmem_limit_bytes=limit),
        name="ffn",
    )(x, g, wg, wu, wd)


def _rope(t, cos, sin_signed, first_half):
    outs = []
    for c in range(t.shape[1] // V7X_LANES):
        tc = t[:, c * V7X_LANES:(c + 1) * V7X_LANES]
        partner = jnp.where(first_half,
                            pltpu.roll(tc, V7X_LANES - HEAD_DIM // 2, 1),
                            pltpu.roll(tc, HEAD_DIM // 2, 1))
        outs.append(tc * cos + partner * sin_signed)
    return jnp.concatenate(outs, axis=1)


def _first_half_mask():
    lane = lax.broadcasted_iota(jnp.int32, (1, V7X_LANES), 1)
    return (lane % HEAD_DIM) < (HEAD_DIM // 2)


def _qkv_group(h, win_ref, g, cos, sin_signed, first_half, q_scale):
    base = POOL_WIDTH + g * QKV_GROUP_WIDTH
    qkv = jnp.dot(h, win_ref[:, base:base + QKV_GROUP_WIDTH], preferred_element_type=F32)
    q = _rope(qkv[:, :GROUP_WIDTH], cos, sin_signed, first_half) * q_scale
    k = _rope(qkv[:, GROUP_WIDTH:2 * GROUP_WIDTH], cos, sin_signed, first_half)
    return jnp.concatenate([q, k, qkv[:, 2 * GROUP_WIDTH:]], axis=1)


def _pool_branch(window_means_minus_tok, wgrp_ref, pscale_ref, wpbr_ref):
    mixed = [jnp.dot(p.astype(BF16), wgrp_ref[gi], preferred_element_type=F32)
             for gi, p in enumerate(window_means_minus_tok)]
    pool = jnp.concatenate(mixed, axis=1) * pscale_ref[...]
    return jnp.dot(pool.astype(BF16), wpbr_ref[...], preferred_element_type=F32)


def _gates(h, win_ref):
    d = win_ref.shape[0]
    base = POOL_WIDTH + len(ATTN_GROUPS) * QKV_GROUP_WIDTH
    gate_pool = jax.nn.sigmoid(jnp.dot(h, win_ref[:, base:base + d], preferred_element_type=F32))
    gate_attn = jax.nn.sigmoid(jnp.dot(h, win_ref[:, base + d:base + 2 * d], preferred_element_type=F32))
    return gate_pool, gate_attn


def _prompt_mix_in_kernel(x_ref, g_ref, win_ref, cos_ref, sin_ref, wgrp_ref, pscale_ref, wpbr_ref,
                          q0_ref, q1_ref, q2_ref, kv0_ref, kv1_ref, kv2_ref,
                          kvt0_ref, kvt1_ref, kvt2_ref, tail_ref, pp_ref, ga_ref, uext_ref, qkv_ref,
                          *, ts):
    j = pl.program_id(1)

    @pl.when(j == 0)
    def _():
        uext_ref[0:POOL_CARRY, :] = jnp.zeros((POOL_CARRY, POOL_WIDTH), F32)

    h = _rmsnorm(x_ref[0], g_ref[...]).astype(BF16)

    cos = cos_ref[...]
    sin_signed = sin_ref[...]
    first_half = _first_half_mask()
    outs = ((q0_ref, kv0_ref, kvt0_ref), (q1_ref, kv1_ref, kvt1_ref), (q2_ref, kv2_ref, kvt2_ref))
    planes = QKV_GROUP_WIDTH // V7X_LANES
    slot = 0
    for g in sorted(range(len(ATTN_GROUPS)), key=lambda g: -ATTN_GROUPS[g][1]):
        q_ref, kv_ref, kvt_ref = outs[g]
        dil = ATTN_GROUPS[g][1]
        qkv = _qkv_group(h, win_ref, g, cos, sin_signed, first_half, ATTN_SCALE * LOG2_E)
        if dil == 1:
            q_ref[0] = qkv[:, :GROUP_WIDTH].astype(BF16)
            kv_ref[0] = qkv[:, GROUP_WIDTH:].astype(BF16)
        else:
            for c in range(planes):
                qkv_ref[slot + c] = qkv[:, c * V7X_LANES:(c + 1) * V7X_LANES]
            for r in range(dil):
                rows = jnp.concatenate([qkv_ref[slot + c, pl.ds(r, ts // dil, stride=dil), :]
                                        for c in range(planes)], axis=1)
                q_ref[0, :, r * GROUP_WIDTH:(r + 1) * GROUP_WIDTH] = rows[:, :GROUP_WIDTH].astype(BF16)
                kv_ref[0, :, 2 * r * GROUP_WIDTH:2 * (r + 1) * GROUP_WIDTH] = rows[:, GROUP_WIDTH:].astype(BF16)
            slot += planes
        keep = kvt_ref.shape[2]
        kvt_ref[0] = qkv[ts - keep:, GROUP_WIDTH:].T

    u = jnp.dot(h, win_ref[:, :POOL_WIDTH], preferred_element_type=F32)
    uext_ref[POOL_CARRY:POOL_CARRY + ts, :] = u
    pos = j * ts + lax.broadcasted_iota(jnp.int32, (ts, 1), 0)
    pooled = []
    for gi, w in enumerate(POOL_WINDOWS):
        lo, hi = gi * POOL_GROUP, (gi + 1) * POOL_GROUP
        tok = u[:, lo:hi]
        total = tok
        for back in range(1, w):
            total = total + uext_ref[POOL_CARRY - back:POOL_CARRY - back + ts, lo:hi]
        cnt = jnp.minimum(pos + 1, w).astype(F32)
        pooled.append(total / cnt - tok)
    pool_br = _pool_branch(pooled, wgrp_ref, pscale_ref, wpbr_ref)
    uext_ref[0:POOL_CARRY, :] = u[ts - POOL_CARRY:, :]
    tail_ref[0] = u[ts - POOL_CARRY:, :]
    gate_pool, gate_attn = _gates(h, win_ref)
    pp_ref[0] = gate_pool * pool_br
    ga_ref[0] = gate_attn


def _prompt_mix_in_call(x, g, win, cos, sin, wgrp, pscale, wpbr, ts):
    b, s, d = x.shape
    seq_tiles = s // ts
    zw = win.shape[1]

    def tile(width):
        return pl.BlockSpec((1, ts, width), lambda bi, j: (bi, j, 0))

    q_specs, kv_specs, q_shapes, kv_shapes, kvt_specs, kvt_shapes = [], [], [], [], [], []
    for window, dil in ATTN_GROUPS:
        q_specs.append(pl.BlockSpec((1, ts // dil, dil * GROUP_WIDTH), lambda bi, j: (bi, j, 0)))
        kv_specs.append(pl.BlockSpec((1, ts // dil, 2 * dil * GROUP_WIDTH), lambda bi, j: (bi, j, 0)))
        q_shapes.append(jax.ShapeDtypeStruct((b, s // dil, dil * GROUP_WIDTH), BF16))
        kv_shapes.append(jax.ShapeDtypeStruct((b, s // dil, 2 * dil * GROUP_WIDTH), BF16))
        keep = min(window, s)
        if keep >= ts:
            first = seq_tiles - keep // ts
            kvt_specs.append(pl.BlockSpec((1, 2 * GROUP_WIDTH, ts),
                                          lambda bi, j, first=first: (bi, 0, jnp.maximum(j - first, 0))))
        else:
            kvt_specs.append(pl.BlockSpec((1, 2 * GROUP_WIDTH, keep), lambda bi, j: (bi, 0, 0)))
        kvt_shapes.append(jax.ShapeDtypeStruct((b, 2 * GROUP_WIDTH, keep), F32))
    n_g = len(ATTN_GROUPS)
    out_specs = (q_specs + kv_specs + kvt_specs
                 + [pl.BlockSpec((1, POOL_CARRY, POOL_WIDTH), lambda bi, j: (bi, 0, 0)), tile(d), tile(d)])
    out_shape = (q_shapes + kv_shapes + kvt_shapes
                 + [jax.ShapeDtypeStruct((b, POOL_CARRY, POOL_WIDTH), F32),
                    jax.ShapeDtypeStruct((b, s, d), F32), jax.ShapeDtypeStruct((b, s, d), F32)])
    table = pl.BlockSpec((ts, V7X_LANES), lambda bi, j: (j, 0))
    pipelined = (3 * _nbytes((ts, d), F32) + n_g * _nbytes((ts, QKV_GROUP_WIDTH), BF16)
                 + n_g * _nbytes((ts, 2 * GROUP_WIDTH), F32) + 2 * _nbytes((ts, V7X_LANES), F32))
    resident = _nbytes((d, zw), BF16) + _nbytes(wpbr.shape, BF16) + _nbytes(wgrp.shape, BF16)
    n_dilated = sum(1 for _, dil in ATTN_GROUPS if dil != 1)
    scratch = [pltpu.VMEM((ts + POOL_CARRY, POOL_WIDTH), F32),
               pltpu.VMEM((n_dilated * QKV_GROUP_WIDTH // V7X_LANES, ts, V7X_LANES), F32)]
    temps = (_nbytes((ts + POOL_CARRY, POOL_WIDTH), F32) + n_dilated * _nbytes((ts, QKV_GROUP_WIDTH), F32)
             + 6 * _nbytes((ts, d), F32))
    return pl.pallas_call(
        functools.partial(_prompt_mix_in_kernel, ts=ts),
        grid=(b, seq_tiles),
        in_specs=[tile(d), _resident((1, d)), _resident((d, zw)), table, table,
                  _resident(wgrp.shape), _resident((1, POOL_WIDTH)), _resident(wpbr.shape)],
        out_specs=out_specs,
        out_shape=out_shape,
        scratch_shapes=scratch,
        compiler_params=pltpu.CompilerParams(dimension_semantics=("arbitrary", "arbitrary"),
                                             vmem_limit_bytes=_vmem_limit(pipelined, resident, temps)),
        name="prompt_mix_in",
    )(x, g, win, cos, sin, wgrp, pscale, wpbr)


def _head_lane_masks():
    lane = lax.broadcasted_iota(jnp.int32, (1, GROUP_WIDTH), 1)
    return [(lane // HEAD_DIM) == hd for hd in range(HEADS)]


SOFTMAX_ROWS = 128


def _band_block(q, kt, v, bias_ref, head_ones, low_head):
    blk, keys = q.shape[0], v.shape[0]
    q_heads = jnp.concatenate([q * one for one in head_ones], axis=0)
    sc = jnp.dot(q_heads, kt, preferred_element_type=F32)
    outs, lses = [], []
    for hd in range(HEADS):
        es, mxs, dens = [], [], []
        for r0 in range(0, blk, SOFTMAX_ROWS):
            s_c = sc[hd * blk + r0:hd * blk + r0 + SOFTMAX_ROWS, :] + bias_ref[r0:r0 + SOFTMAX_ROWS, :keys]
            mx_c = jnp.max(s_c, axis=1, keepdims=True)
            e_c = jnp.exp2(s_c - mx_c)
            mxs.append(mx_c)
            dens.append(jnp.sum(e_c, axis=1, keepdims=True))
            es.append(e_c.astype(BF16))
        den = jnp.concatenate(dens, axis=0)
        tile = slice((hd * HEAD_DIM // V7X_LANES) * V7X_LANES, (hd * HEAD_DIM // V7X_LANES + 1) * V7X_LANES)
        o_un = jnp.dot(jnp.concatenate(es, axis=0), v[:, tile], preferred_element_type=F32)
        outs.append(o_un * (1.0 / den))
        lses.append(jnp.broadcast_to(jnp.concatenate(mxs, axis=0) + jnp.log2(den), (blk, V7X_LANES)))
    pairs = range(0, HEADS, V7X_LANES // HEAD_DIM)
    return ([jnp.where(low_head, outs[hd], outs[hd + 1]) for hd in pairs],
            [jnp.where(low_head, lses[hd], lses[hd + 1]) for hd in pairs])


ATTN_PLANES = GROUP_WIDTH // V7X_LANES


def _stream_attn_kernel(q_ref, kv_ref, o_ref, l_ref, first_bias_ref, band_bias_ref, *, dil, seq):
    blk = KEYS_PER_QUERY

    @pl.when(pl.program_id(0) == 0)
    def _():
        qi = lax.broadcasted_iota(jnp.int32, (blk, 1), 0)
        first = lax.broadcasted_iota(jnp.int32, (1, blk), 1) <= qi
        dist = qi + blk - lax.broadcasted_iota(jnp.int32, (1, 2 * blk), 1)
        first_bias_ref[...] = jnp.where(first, 0.0, NEG_INF)
        band_bias_ref[...] = jnp.where((dist >= 0) & (dist <= blk), 0.0, NEG_INF)

    head_ones = [jnp.where(m, 1.0, 0.0).astype(BF16) for m in _head_lane_masks()]
    low_head = lax.broadcasted_iota(jnp.int32, (1, V7X_LANES), 1) < HEAD_DIM
    for r in range(dil):
        qc = slice(r * GROUP_WIDTH, (r + 1) * GROUP_WIDTH)
        kc = slice(2 * r * GROUP_WIDTH, (2 * r + 1) * GROUP_WIDTH)
        vc = slice((2 * r + 1) * GROUP_WIDTH, (2 * r + 2) * GROUP_WIDTH)
        for i in range(seq // (dil * blk)):
            krows = slice(max(i - 1, 0) * blk, (i + 1) * blk)
            outs, lses = _band_block(q_ref[0, i * blk:(i + 1) * blk, qc], kv_ref[0, krows, kc].T, kv_ref[0, krows, vc],
                                     first_bias_ref if i == 0 else band_bias_ref, head_ones, low_head)
            for c in range(ATTN_PLANES):
                o_ref[0, c, r, i * blk:(i + 1) * blk, :] = outs[c]
                l_ref[0, c, r, i * blk:(i + 1) * blk, :] = lses[c]


def _merge_groups(outs, lses, exp_fn):
    mx = functools.reduce(jnp.maximum, lses)
    ws = [exp_fn(l - mx) for l in lses]
    num = functools.reduce(lambda a, b: a + b, [w * o for w, o in zip(ws, outs)])
    return num / functools.reduce(lambda a, b: a + b, ws)


def _stream_attn_call(q, kv, dil, seq):
    b = q.shape[0]

    def whole(a):
        zeros = (0,) * (len(a.shape) - 1)
        return pl.BlockSpec((1,) + tuple(a.shape[1:]), lambda bi: (bi,) + zeros)

    planes = jax.ShapeDtypeStruct((b, ATTN_PLANES, dil, seq // dil, V7X_LANES), F32)
    pipelined = _nbytes((seq, QKV_GROUP_WIDTH), BF16) + 2 * _nbytes((seq, GROUP_WIDTH), F32)
    scratch = [pltpu.VMEM((KEYS_PER_QUERY, KEYS_PER_QUERY), F32),
               pltpu.VMEM((KEYS_PER_QUERY, 2 * KEYS_PER_QUERY), F32)]
    scratch_bytes = 3 * _nbytes((KEYS_PER_QUERY, KEYS_PER_QUERY), F32)
    return pl.pallas_call(
        functools.partial(_stream_attn_kernel, dil=dil, seq=seq),
        grid=(b,),
        in_specs=[whole(q), whole(kv)],
        out_specs=[whole(planes), whole(planes)],
        out_shape=[planes, planes],
        scratch_shapes=scratch,
        compiler_params=pltpu.CompilerParams(dimension_semantics=("arbitrary",),
                                             vmem_limit_bytes=_vmem_limit(pipelined, scratch_bytes, 16 << 20)),
        name=f"stream_attn_d{dil}",
    )(q, kv)


def _token_order(ref, plane, stage_ref, slot):
    dil, rows = ref.shape[2], ref.shape[3]
    if dil == 1:
        return ref[0, plane, 0]
    for r in range(dil):
        stage_ref[slot, pl.ds(r, rows, stride=dil), :] = ref[0, plane, r]
    return stage_ref[slot]


def _mix_out_kernel(x_ref, pp_ref, ga_ref, *rest, n_groups):
    if n_groups:
        group_refs, rest = rest[:2 * n_groups], rest[2 * n_groups:]
        stage_ref, attn_ref = rest[-2:]

        @pl.when(pl.program_id(0) == 0)
        def _():
            attn_ref[...] = jnp.zeros(attn_ref.shape, BF16)
    else:
        attn_ref, rest = rest[0], rest[1:]
    wabr_ref, wo_ref, g2_ref, wg_ref, wu_ref, wd_ref, gf_ref, y_ref = rest[:8]
    attn_br = jnp.dot(attn_ref[...], wabr_ref[...], preferred_element_type=F32)
    merged = pp_ref[...] + ga_ref[...] * attn_br
    x = x_ref[...] + jnp.dot(merged.astype(BF16), wo_ref[...], preferred_element_type=F32)
    x = _macaron_ffn(x, g2_ref, wg_ref, wu_ref, wd_ref)
    y_ref[...] = _rmsnorm(x, gf_ref[...])
    if n_groups:
        for c in range(ATTN_PLANES):
            outs = [_token_order(group_refs[2 * g], c, stage_ref, 2 * g) for g in range(n_groups)]
            lses = [_token_order(group_refs[2 * g + 1], c, stage_ref, 2 * g + 1) for g in range(n_groups)]
            attn_ref[:, c * V7X_LANES:(c + 1) * V7X_LANES] = _merge_groups(outs, lses, jnp.exp2).astype(BF16)


def _mix_out_call(x, pp, ga, attn, wabr, wo, g2, wg, wu, wd, gf, tm, group_planes=()):
    m, d = x.shape
    f = wg.shape[1]
    n_tiles = m // tm
    pipelined = 4 * _nbytes((tm, d), F32)
    scratch = []
    if group_planes:
        steps = n_tiles + 1
        row = pl.BlockSpec((tm, d), lambda i: (jnp.maximum(i - 1, 0), 0))
        attn_ins, attn_specs = list(group_planes), []
        for p in group_planes:
            dil, seq_rows = p.shape[2], p.shape[3]
            tiles = seq_rows * dil // tm
            attn_specs.append(pl.BlockSpec(
                (1, ATTN_PLANES, dil, tm // dil, V7X_LANES),
                lambda i, tiles=tiles: (jnp.minimum(i, n_tiles - 1) // tiles, 0, 0,
                                        jnp.minimum(i, n_tiles - 1) % tiles, 0)))
        pipelined += len(group_planes) * _nbytes((tm, GROUP_WIDTH), F32)
        scratch = [pltpu.VMEM((len(group_planes), tm, V7X_LANES), F32), pltpu.VMEM((tm, GROUP_WIDTH), BF16)]
    else:
        steps = n_tiles
        row = pl.BlockSpec((tm, d), lambda i: (i, 0))
        attn_ins, attn_specs = [attn], [pl.BlockSpec((tm, GROUP_WIDTH), lambda i: (i, 0))]
        pipelined += _nbytes((tm, GROUP_WIDTH), BF16)
    resident = 3 * _nbytes((d, f), BF16) + _nbytes((d, d), BF16) + _nbytes((GROUP_WIDTH, d), BF16)
    temps = 4 * _nbytes((tm, FFN_CHUNK), F32) + 4 * _nbytes((tm, d), F32)
    return pl.pallas_call(
        functools.partial(_mix_out_kernel, n_groups=len(group_planes) // 2),
        grid=(steps,),
        in_specs=[row, row, row] + attn_specs
                 + [_resident((GROUP_WIDTH, d)), _resident((d, d)), _resident((1, d)),
                    _resident((d, f)), _resident((d, f)), _resident((f, d)), _resident((1, d))],
        out_specs=row,
        out_shape=jax.ShapeDtypeStruct((m, d), F32),
        scratch_shapes=scratch,
        compiler_params=pltpu.CompilerParams(dimension_semantics=("arbitrary",),
                                             vmem_limit_bytes=_vmem_limit(pipelined, resident, temps)),
        name="mix_out",
    )(x, pp, ga, *attn_ins, wabr, wo, g2, wg, wu, wd, gf)


def _sample_mix_in_kernel(x_ref, g_ref, win_ref, state_ref, cos_ref, sin_ref, wgrp_ref, pscale_ref, wpbr_ref,
                          qkv0_ref, qkv1_ref, qkv2_ref, u_ref, pp_ref, ga_ref):
    h = _rmsnorm(x_ref[...], g_ref[...]).astype(BF16)
    u = jnp.dot(h, win_ref[:, :POOL_WIDTH], preferred_element_type=F32)
    u_ref[...] = u
    pooled = []
    for gi, w in enumerate(POOL_WINDOWS):
        lo, hi = gi * POOL_GROUP, (gi + 1) * POOL_GROUP
        tok = u[:, lo:hi]
        total = tok
        for back in range(1, w):
            total = total + state_ref[POOL_STATE - back, :, lo:hi]
        pooled.append(total / float(min(PAST_LEN + 1, w)) - tok)
    pool_br = _pool_branch(pooled, wgrp_ref, pscale_ref, wpbr_ref)
    gate_pool, gate_attn = _gates(h, win_ref)
    pp_ref[...] = gate_pool * pool_br
    ga_ref[...] = gate_attn
    first_half = _first_half_mask()
    for g, qkv_ref in enumerate((qkv0_ref, qkv1_ref, qkv2_ref)):
        qkv_ref[...] = _qkv_group(h, win_ref, g, cos_ref[...], sin_ref[...], first_half, ATTN_SCALE)


def _sample_mix_in_call(x, g, win, state_t, cos, sin, wgrp, pscale, wpbr):
    n, d = x.shape
    n_g = len(ATTN_GROUPS)

    def full(shape):
        zeros = (0,) * len(shape)
        return pl.BlockSpec(shape, lambda i: zeros)

    out_shape = ([jax.ShapeDtypeStruct((n, QKV_GROUP_WIDTH), F32)] * n_g
                 + [jax.ShapeDtypeStruct((n, POOL_WIDTH), F32),
                    jax.ShapeDtypeStruct((n, d), F32), jax.ShapeDtypeStruct((n, d), F32)])
    ins = (x, g, win, state_t, cos, sin, wgrp, pscale, wpbr)
    resident = sum(_nbytes(a.shape, a.dtype) for a in ins)
    return pl.pallas_call(
        _sample_mix_in_kernel,
        grid=(1,),
        in_specs=[full(a.shape) for a in ins],
        out_specs=[full(o.shape) for o in out_shape],
        out_shape=out_shape,
        compiler_params=pltpu.CompilerParams(dimension_semantics=("arbitrary",),
                                             vmem_limit_bytes=_vmem_limit(resident, 0, 8 << 20)),
        name="sample_mix_in",
    )(*ins)


def _head_sums(x):
    return jnp.concatenate([jnp.sum(x[hd * HEAD_DIM:(hd + 1) * HEAD_DIM], axis=0, keepdims=True)
                            for hd in range(HEADS)], axis=0)


def _head_expand(y):
    return jnp.concatenate([jnp.broadcast_to(y[hd:hd + 1], (HEAD_DIM, y.shape[1])) for hd in range(HEADS)],
                           axis=0)


def _cache_step_kernel(qkv0_ref, qkv1_ref, qkv2_ref, c0_ref, c1_ref, c2_ref,
                       n0_ref, n1_ref, n2_ref, attn_ref):
    n = pl.program_id(0)
    outs, lses = [], []
    groups = ((qkv0_ref, c0_ref, n0_ref), (qkv1_ref, c1_ref, n1_ref), (qkv2_ref, c2_ref, n2_ref))
    for (qkv_ref, c_ref, new_ref), (window, dil) in zip(groups, ATTN_GROUPS):
        q = qkv_ref[0, :GROUP_WIDTH, :]
        k_new = qkv_ref[0, GROUP_WIDTH:2 * GROUP_WIDTH, :]
        v_new = qkv_ref[0, 2 * GROUP_WIDTH:, :]
        kt = c_ref[0, :GROUP_WIDTH, :]
        vt = c_ref[0, GROUP_WIDTH:, :]
        lane = lax.broadcasted_iota(jnp.int32, (1, window), 1)
        sc = jnp.where(lane % dil == 0, _head_sums(kt * q), NEG_INF)
        sc_new = _head_sums(k_new * q)
        mx = jnp.maximum(jnp.max(sc, axis=1, keepdims=True), sc_new)
        e = jnp.exp(sc - mx)
        e_new = jnp.exp(sc_new - mx)
        den = jnp.sum(e, axis=1, keepdims=True) + e_new
        o_un = jnp.sum(vt * _head_expand(e), axis=1, keepdims=True) + _head_expand(e_new) * v_new
        outs.append(o_un / _head_expand(den))
        lses.append(_head_expand(mx + jnp.log(den)))
        shifted = pltpu.roll(c_ref[0], window - 1, 1)
        new_ref[0] = jnp.where(lane == window - 1, qkv_ref[0, GROUP_WIDTH:, :], shifted)
    attn_col = _merge_groups(outs, lses, jnp.exp)
    col = lax.broadcasted_iota(jnp.int32, attn_ref.shape, 1)

    @pl.when(n == 0)
    def _():
        attn_ref[...] = jnp.zeros(attn_ref.shape, F32)

    attn_ref[...] = jnp.where(col == n, attn_col, attn_ref[...])


def _cache_step_call(qkv_cols, windows_t):
    n = qkv_cols[0].shape[0]
    qspec = pl.BlockSpec((1, QKV_GROUP_WIDTH, 1), lambda i: (i, 0, 0))
    cspecs = [pl.BlockSpec((1,) + c.shape[1:], lambda i: (i, 0, 0)) for c in windows_t]
    pipelined = 2 * sum(_nbytes(c.shape[1:], F32) for c in windows_t) + 3 * _nbytes((QKV_GROUP_WIDTH, V7X_LANES), F32)
    res = pl.pallas_call(
        _cache_step_kernel,
        grid=(n,),
        in_specs=[qspec] * len(qkv_cols) + cspecs,
        out_specs=cspecs + [pl.BlockSpec((GROUP_WIDTH, n), lambda i: (0, 0))],
        out_shape=[jax.ShapeDtypeStruct(c.shape, F32) for c in windows_t]
                  + [jax.ShapeDtypeStruct((GROUP_WIDTH, n), F32)],
        compiler_params=pltpu.CompilerParams(dimension_semantics=("arbitrary",),
                                             vmem_limit_bytes=_vmem_limit(pipelined, 0, 16 << 20)),
        name="cache_step",
    )(*qkv_cols, *windows_t)
    return res[:-1], res[-1]


def _rope_tables(pos):
    half = HEAD_DIM // 2
    inv = ROPE_THETA ** (-jnp.arange(half, dtype=F32) * (2.0 / HEAD_DIM))
    ang = pos.astype(F32)[:, None] * inv[None, :]
    cos, sin = jnp.cos(ang), jnp.sin(ang)
    reps = V7X_LANES // HEAD_DIM
    return (jnp.tile(jnp.concatenate([cos, cos], axis=1), (1, reps)),
            jnp.tile(jnp.concatenate([-sin, sin], axis=1), (1, reps)))


def _window_to_lanes(w):
    n, width = w.shape[:2]
    return jnp.transpose(w, (0, 2, 3, 4, 1)).reshape(n, 2 * GROUP_WIDTH, width)


def _window_from_lanes(wt):
    n, _, width = wt.shape
    return jnp.transpose(wt.reshape(n, 2, HEADS, HEAD_DIM, width), (0, 4, 1, 2, 3))[None]


def kernel(x_prompt, x_sample, state_pool, cache_kv_w128, cache_kv_w512, cache_kv_w2048, norm_ffn1, ffn1_w_gate, ffn1_w_up, ffn1_w_down, norm_mix, w_in, w_pool_grp, pool_scale, w_pool_br, w_attn_br, w_o, norm_ffn2, ffn2_w_gate, ffn2_w_up, ffn2_w_down, norm_final):
    b, s, d = x_prompt.shape
    n, dec_seq, _ = x_sample.shape
    caches = (cache_kv_w128, cache_kv_w512, cache_kv_w2048)
    assert norm_ffn1.shape[0] == 1 and dec_seq == 1, "single layer, single decode token"
    ts = PROMPT_ROW_TILE
    assert s % ts == 0 and ts >= POOL_CARRY
    for cache, (window, dil) in zip(caches, ATTN_GROUPS):
        assert window // dil == KEYS_PER_QUERY and s % (dil * KEYS_PER_QUERY) == 0 and ts % (16 * dil) == 0
        assert cache.shape[2] == window, "decode path expects full windows of history"

    row = lambda v: v.reshape(1, -1)
    bf = lambda w: w.astype(BF16)
    g1, g2, gm, gf = row(norm_ffn1[0]), row(norm_ffn2[0]), row(norm_mix[0]), row(norm_final)
    f1 = (bf(ffn1_w_gate[0]), bf(ffn1_w_up[0]), bf(ffn1_w_down[0]))
    f2 = (bf(ffn2_w_gate[0]), bf(ffn2_w_up[0]), bf(ffn2_w_down[0]))
    win, wgrp, wpbr, wabr, wo = bf(w_in[0]), bf(w_pool_grp[0]), bf(w_pool_br[0]), bf(w_attn_br[0]), bf(w_o[0])
    pscale = row(pool_scale[0])

    cos_p, sin_p = _rope_tables(jnp.arange(s, dtype=jnp.int32))
    x1 = _ffn_call(x_prompt.reshape(b * s, d), g1, *f1, tm=FFN_ROW_TILE)
    (q0, q1, q2, kv0, kv1, kv2, kvt0, kvt1, kvt2, tail, pp, ga) = _prompt_mix_in_call(
        x1.reshape(b, s, d), gm, win, cos_p, sin_p, wgrp, pscale, wpbr, ts)
    group_planes = []
    for q, kv, (_, dil) in zip((q0, q1, q2), (kv0, kv1, kv2), ATTN_GROUPS):
        group_planes += _stream_attn_call(q, kv, dil, s)
    y_prompt = _mix_out_call(x1, pp.reshape(b * s, d), ga.reshape(b * s, d), None, wabr, wo, g2, *f2, gf,
                             tm=ts, group_planes=tuple(group_planes)).reshape(b, s, d)
    pool_prompt = tail[None, :, POOL_CARRY - POOL_STATE:, :]

    cos_s, sin_s = _rope_tables(PAST_LEN + jnp.arange(dec_seq, dtype=jnp.int32))
    xs1 = _ffn_call(x_sample.reshape(n, d), g1, *f1, tm=n)
    state_t = jnp.swapaxes(state_pool[0], 0, 1)
    (sqkv0, sqkv1, sqkv2, u_new, spp, sga) = _sample_mix_in_call(
        xs1, gm, win, state_t, cos_s, sin_s, wgrp, pscale, wpbr)
    new_windows, attn_t = _cache_step_call([a[:, :, None] for a in (sqkv0, sqkv1, sqkv2)],
                                           [_window_to_lanes(c[0]) for c in caches])
    y_sample = _mix_out_call(xs1, spp, sga, attn_t.T.astype(BF16), wabr, wo, g2, *f2, gf,
                             tm=n).reshape(n, dec_seq, d)
    pool_sample = jnp.swapaxes(jnp.concatenate([state_t[1:], u_new[None]], axis=0), 0, 1)[None]

    return (y_prompt, y_sample, pool_prompt,
            _window_from_lanes(kvt0), _window_from_lanes(kvt1), _window_from_lanes(kvt2),
            pool_sample, *[_window_from_lanes(w) for w in new_windows])
```

```python
import functools

import jax
import jax.numpy as jnp
from jax import lax
from jax.experimental import pallas as pl
from jax.experimental.pallas import tpu as pltpu

F32 = jnp.float32
BF16 = jnp.bfloat16

PAST_LEN = 16384
POOL_WINDOWS = (2, 4, 8, 16)
POOL_GROUP = 128
POOL_WIDTH = POOL_GROUP * len(POOL_WINDOWS)
POOL_STATE = max(POOL_WINDOWS) - 1
POOL_CARRY = 16
HEAD_DIM = 64
HEADS = 4
GROUP_WIDTH = HEADS * HEAD_DIM
ATTN_GROUPS = ((128, 1), (512, 4), (2048, 16))
KEYS_PER_QUERY = 128
QKV_GROUP_WIDTH = 3 * GROUP_WIDTH
ROPE_THETA = 10000.0
RMS_EPS = 1e-6
NEG_INF = -1e30
ATTN_SCALE = HEAD_DIM ** -0.5
LOG2_E = 1.4426950408889634

V7X_VMEM_BYTES = 64 * 1024 * 1024
V7X_LANES = 128
V7X_MXU_COLS = 256
MAX_FAST_SUBLANE_STRIDE = 4

PROMPT_ROW_TILE = 512
FFN_ROW_TILE = 1024
FFN_CHUNK = 4 * V7X_MXU_COLS


def _vmem_limit(pipelined_bytes, resident_bytes, temp_bytes):
    need = 2 * pipelined_bytes + resident_bytes + temp_bytes
    return int(min(need, V7X_VMEM_BYTES - (4 << 20)))


def _nbytes(shape, dtype):
    n = 1
    for s in shape:
        n *= s
    return n * jnp.dtype(dtype).itemsize


def _resident(shape):
    zeros = (0,) * len(shape)
    return pl.BlockSpec(shape, lambda *_: zeros, pipeline_mode=pl.Buffered(1))


def _rmsnorm(x, g):
    ms = jnp.mean(x * x, axis=-1, keepdims=True)
    return x * lax.rsqrt(ms + RMS_EPS) * g


def _ffn_chunks(width):
    return [(lo, min(lo + FFN_CHUNK, width)) for lo in range(0, width, FFN_CHUNK)]


def _swiglu(xn, wg_ref, wu_ref, wd_ref):
    acc = None
    for lo, hi in _ffn_chunks(wg_ref.shape[1]):
        gate = jnp.dot(xn, wg_ref[:, lo:hi], preferred_element_type=F32)
        up = jnp.dot(xn, wu_ref[:, lo:hi], preferred_element_type=F32)
        hid = (gate * jax.nn.sigmoid(gate) * up).astype(BF16)
        part = jnp.dot(hid, wd_ref[lo:hi, :], preferred_element_type=F32)
        acc = part if acc is None else acc + part
    return acc


def _macaron_ffn(x, g_ref, wg_ref, wu_ref, wd_ref):
    xn = _rmsnorm(x, g_ref[...]).astype(BF16)
    return x + 0.5 * _swiglu(xn, wg_ref, wu_ref, wd_ref)


def _ffn_kernel(x_ref, g_ref, wg_ref, wu_ref, wd_ref, o_ref):
    o_ref[...] = _macaron_ffn(x_ref[...], g_ref, wg_ref, wu_ref, wd_ref)


def _ffn_call(x, g, wg, wu, wd, tm):
    m, d = x.shape
    f = wg.shape[1]
    row = pl.BlockSpec((tm, d), lambda i: (i, 0))
    limit = _vmem_limit(2 * _nbytes((tm, d), F32),
                        3 * _nbytes((d, f), BF16),
                        4 * _nbytes((tm, FFN_CHUNK), F32) + 2 * _nbytes((tm, d), F32))
    return pl.pallas_call(
        _ffn_kernel,
        grid=(m // tm,),
        in_specs=[row, _resident((1, d)), _resident((d, f)), _resident((d, f)), _resident((f, d))],
        out_specs=row,
        out_shape=jax.ShapeDtypeStruct((m, d), F32),
        compiler_params=pltpu.CompilerParams(dimension_semantics=("arbitrary",), vmem_limit_bytes=limit),
        name="ffn",
    )(x, g, wg, wu, wd)


def _rope(t, cos, sin_signed, first_half):
    outs = []
    for c in range(t.shape[1] // V7X_LANES):
        tc = t[:, c * V7X_LANES:(c + 1) * V7X_LANES]
        partner = jnp.where(first_half,
                            pltpu.roll(tc, V7X_LANES - HEAD_DIM // 2, 1),
                            pltpu.roll(tc, HEAD_DIM // 2, 1))
        outs.append(tc * cos + partner * sin_signed)
    return jnp.concatenate(outs, axis=1)


def _first_half_mask():
    lane = lax.broadcasted_iota(jnp.int32, (1, V7X_LANES), 1)
    return (lane % HEAD_DIM) < (HEAD_DIM // 2)


def _qkv_group(h, win_ref, g, cos, sin_signed, first_half, q_scale):
    base = POOL_WIDTH + g * QKV_GROUP_WIDTH
    qkv = jnp.dot(h, win_ref[:, base:base + QKV_GROUP_WIDTH], preferred_element_type=F32)
    q = _rope(qkv[:, :GROUP_WIDTH], cos, sin_signed, first_half) * q_scale
    k = _rope(qkv[:, GROUP_WIDTH:2 * GROUP_WIDTH], cos, sin_signed, first_half)
    return jnp.concatenate([q, k, qkv[:, 2 * GROUP_WIDTH:]], axis=1)


def _pool_branch(window_means_minus_tok, wgrp_ref, pscale_ref, wpbr_ref):
    mixed = [jnp.dot(p.astype(BF16), wgrp_ref[gi], preferred_element_type=F32)
             for gi, p in enumerate(window_means_minus_tok)]
    pool = jnp.concatenate(mixed, axis=1) * pscale_ref[...]
    return jnp.dot(pool.astype(BF16), wpbr_ref[...], preferred_element_type=F32)


def _gates(h, win_ref):
    d = win_ref.shape[0]
    base = POOL_WIDTH + len(ATTN_GROUPS) * QKV_GROUP_WIDTH
    gate_pool = jax.nn.sigmoid(jnp.dot(h, win_ref[:, base:base + d], preferred_element_type=F32))
    gate_attn = jax.nn.sigmoid(jnp.dot(h, win_ref[:, base + d:base + 2 * d], preferred_element_type=F32))
    return gate_pool, gate_attn


def _prompt_mix_in_kernel(x_ref, g_ref, win_ref, cos_ref, sin_ref, wgrp_ref, pscale_ref, wpbr_ref,
                          q0_ref, q1_ref, q2_ref, kv0_ref, kv1_ref, kv2_ref,
                          kvt0_ref, kvt1_ref, kvt2_ref, tail_ref, pp_ref, ga_ref, uext_ref, qkv_ref, mid_ref,
                          *, ts):
    j = pl.program_id(1)

    @pl.when(j == 0)
    def _():
        uext_ref[0:POOL_CARRY, :] = jnp.zeros((POOL_CARRY, POOL_WIDTH), F32)

    h = _rmsnorm(x_ref[0], g_ref[...]).astype(BF16)

    cos = cos_ref[...]
    sin_signed = sin_ref[...]
    first_half = _first_half_mask()
    outs = ((q0_ref, kv0_ref, kvt0_ref), (q1_ref, kv1_ref, kvt1_ref), (q2_ref, kv2_ref, kvt2_ref))
    planes = QKV_GROUP_WIDTH // V7X_LANES
    slot = 0
    for g in sorted(range(len(ATTN_GROUPS)), key=lambda g: -ATTN_GROUPS[g][1]):
        q_ref, kv_ref, kvt_ref = outs[g]
        dil = ATTN_GROUPS[g][1]
        qkv = _qkv_group(h, win_ref, g, cos, sin_signed, first_half, ATTN_SCALE * LOG2_E)
        if dil == 1:
            q_ref[0] = qkv[:, :GROUP_WIDTH].astype(BF16)
            kv_ref[0] = qkv[:, GROUP_WIDTH:].astype(BF16)
        else:
            for c in range(planes):
                qkv_ref[slot + c] = qkv[:, c * V7X_LANES:(c + 1) * V7X_LANES]
            inner = min(dil, MAX_FAST_SUBLANE_STRIDE)
            outer = dil // inner
            assert outer <= MAX_FAST_SUBLANE_STRIDE and outer * inner == dil
            if outer > 1:
                for c in range(planes):
                    for bb in range(inner):
                        mid_ref[c, bb * (ts // inner):(bb + 1) * (ts // inner), :] = (
                            qkv_ref[slot + c, pl.ds(bb, ts // inner, stride=inner), :])
            for r in range(dil):
                if outer > 1:
                    a, bb = r // inner, r % inner
                    rows = jnp.concatenate(
                        [mid_ref[c, pl.ds(bb * (ts // inner) + a, ts // dil, stride=outer), :]
                         for c in range(planes)], axis=1)
                else:
                    rows = jnp.concatenate([qkv_ref[slot + c, pl.ds(r, ts // dil, stride=dil), :]
                                            for c in range(planes)], axis=1)
                q_ref[0, :, r * GROUP_WIDTH:(r + 1) * GROUP_WIDTH] = rows[:, :GROUP_WIDTH].astype(BF16)
                kv_ref[0, :, 2 * r * GROUP_WIDTH:2 * (r + 1) * GROUP_WIDTH] = rows[:, GROUP_WIDTH:].astype(BF16)
            slot += planes
        keep = kvt_ref.shape[2]
        kvt_ref[0] = qkv[ts - keep:, GROUP_WIDTH:].T

    u = jnp.dot(h, win_ref[:, :POOL_WIDTH], preferred_element_type=F32)
    uext_ref[POOL_CARRY:POOL_CARRY + ts, :] = u
    pos = j * ts + lax.broadcasted_iota(jnp.int32, (ts, 1), 0)
    pooled = []
    for gi, w in enumerate(POOL_WINDOWS):
        lo, hi = gi * POOL_GROUP, (gi + 1) * POOL_GROUP
        tok = u[:, lo:hi]
        total = tok
        for back in range(1, w):
            total = total + uext_ref[POOL_CARRY - back:POOL_CARRY - back + ts, lo:hi]
        cnt = jnp.minimum(pos + 1, w).astype(F32)
        pooled.append(total / cnt - tok)
    pool_br = _pool_branch(pooled, wgrp_ref, pscale_ref, wpbr_ref)
    uext_ref[0:POOL_CARRY, :] = u[ts - POOL_CARRY:, :]
    tail_ref[0] = u[ts - POOL_CARRY:, :]
    gate_pool, gate_attn = _gates(h, win_ref)
    pp_ref[0] = gate_pool * pool_br
    ga_ref[0] = gate_attn


def _prompt_mix_in_call(x, g, win, cos, sin, wgrp, pscale, wpbr, ts):
    b, s, d = x.shape
    seq_tiles = s // ts
    zw = win.shape[1]

    def tile(width):
        return pl.BlockSpec((1, ts, width), lambda bi, j: (bi, j, 0))

    q_specs, kv_specs, q_shapes, kv_shapes, kvt_specs, kvt_shapes = [], [], [], [], [], []
    for window, dil in ATTN_GROUPS:
        q_specs.append(pl.BlockSpec((1, ts // dil, dil * GROUP_WIDTH), lambda bi, j: (bi, j, 0)))
        kv_specs.append(pl.BlockSpec((1, ts // dil, 2 * dil * GROUP_WIDTH), lambda bi, j: (bi, j, 0)))
        q_shapes.append(jax.ShapeDtypeStruct((b, s // dil, dil * GROUP_WIDTH), BF16))
        kv_shapes.append(jax.ShapeDtypeStruct((b, s // dil, 2 * dil * GROUP_WIDTH), BF16))
        keep = min(window, s)
        if keep >= ts:
            first = seq_tiles - keep // ts
            kvt_specs.append(pl.BlockSpec((1, 2 * GROUP_WIDTH, ts),
                                          lambda bi, j, first=first: (bi, 0, jnp.maximum(j - first, 0))))
        else:
            kvt_specs.append(pl.BlockSpec((1, 2 * GROUP_WIDTH, keep), lambda bi, j: (bi, 0, 0)))
        kvt_shapes.append(jax.ShapeDtypeStruct((b, 2 * GROUP_WIDTH, keep), F32))
    n_g = len(ATTN_GROUPS)
    out_specs = (q_specs + kv_specs + kvt_specs
                 + [pl.BlockSpec((1, POOL_CARRY, POOL_WIDTH), lambda bi, j: (bi, 0, 0)), tile(d), tile(d)])
    out_shape = (q_shapes + kv_shapes + kvt_shapes
                 + [jax.ShapeDtypeStruct((b, POOL_CARRY, POOL_WIDTH), F32),
                    jax.ShapeDtypeStruct((b, s, d), F32), jax.ShapeDtypeStruct((b, s, d), F32)])
    table = pl.BlockSpec((ts, V7X_LANES), lambda bi, j: (j, 0))
    pipelined = (3 * _nbytes((ts, d), F32) + n_g * _nbytes((ts, QKV_GROUP_WIDTH), BF16)
                 + n_g * _nbytes((ts, 2 * GROUP_WIDTH), F32) + 2 * _nbytes((ts, V7X_LANES), F32))
    resident = _nbytes((d, zw), BF16) + _nbytes(wpbr.shape, BF16) + _nbytes(wgrp.shape, BF16)
    n_dilated = sum(1 for _, dil in ATTN_GROUPS if dil != 1)
    scratch = [pltpu.VMEM((ts + POOL_CARRY, POOL_WIDTH), F32),
               pltpu.VMEM((n_dilated * QKV_GROUP_WIDTH // V7X_LANES, ts, V7X_LANES), F32),
               pltpu.VMEM((QKV_GROUP_WIDTH // V7X_LANES, ts, V7X_LANES), F32)]
    temps = (_nbytes((ts + POOL_CARRY, POOL_WIDTH), F32) + (n_dilated + 1) * _nbytes((ts, QKV_GROUP_WIDTH), F32)
             + 6 * _nbytes((ts, d), F32))
    return pl.pallas_call(
        functools.partial(_prompt_mix_in_kernel, ts=ts),
        grid=(b, seq_tiles),
        in_specs=[tile(d), _resident((1, d)), _resident((d, zw)), table, table,
                  _resident(wgrp.shape), _resident((1, POOL_WIDTH)), _resident(wpbr.shape)],
        out_specs=out_specs,
        out_shape=out_shape,
        scratch_shapes=scratch,
        compiler_params=pltpu.CompilerParams(dimension_semantics=("arbitrary", "arbitrary"),
                                             vmem_limit_bytes=_vmem_limit(pipelined, resident, temps)),
        name="prompt_mix_in",
    )(x, g, win, cos, sin, wgrp, pscale, wpbr)


def _head_lane_masks():
    lane = lax.broadcasted_iota(jnp.int32, (1, GROUP_WIDTH), 1)
    return [(lane // HEAD_DIM) == hd for hd in range(HEADS)]


SOFTMAX_ROWS = 128


def _band_block(q, kt, v, bias_ref, head_ones, low_head):
    blk, keys = q.shape[0], v.shape[0]
    q_heads = jnp.concatenate([q * one for one in head_ones], axis=0)
    sc = jnp.dot(q_heads, kt, preferred_element_type=F32)
    outs, lses = [], []
    for hd in range(HEADS):
        es, mxs, dens = [], [], []
        for r0 in range(0, blk, SOFTMAX_ROWS):
            s_c = sc[hd * blk + r0:hd * blk + r0 + SOFTMAX_ROWS, :] + bias_ref[r0:r0 + SOFTMAX_ROWS, :keys]
            mx_c = jnp.max(s_c, axis=1, keepdims=True)
            e_c = jnp.exp2(s_c - mx_c)
            mxs.append(mx_c)
            dens.append(jnp.sum(e_c, axis=1, keepdims=True))
            es.append(e_c.astype(BF16))
        den = jnp.concatenate(dens, axis=0)
        tile = slice((hd * HEAD_DIM // V7X_LANES) * V7X_LANES, (hd * HEAD_DIM // V7X_LANES + 1) * V7X_LANES)
        o_un = jnp.dot(jnp.concatenate(es, axis=0), v[:, tile], preferred_element_type=F32)
        outs.append(o_un * (1.0 / den))
        lses.append(jnp.broadcast_to(jnp.concatenate(mxs, axis=0) + jnp.log2(den), (blk, V7X_LANES)))
    pairs = range(0, HEADS, V7X_LANES // HEAD_DIM)
    return ([jnp.where(low_head, outs[hd], outs[hd + 1]) for hd in pairs],
            [jnp.where(low_head, lses[hd], lses[hd + 1]) for hd in pairs])


ATTN_PLANES = GROUP_WIDTH // V7X_LANES


def _stream_attn_kernel(q_ref, kv_ref, o_ref, l_ref, first_bias_ref, band_bias_ref, *, dil, seq):
    blk = KEYS_PER_QUERY

    @pl.when(pl.program_id(0) == 0)
    def _():
        qi = lax.broadcasted_iota(jnp.int32, (blk, 1), 0)
        first = lax.broadcasted_iota(jnp.int32, (1, blk), 1) <= qi
        dist = qi + blk - lax.broadcasted_iota(jnp.int32, (1, 2 * blk), 1)
        first_bias_ref[...] = jnp.where(first, 0.0, NEG_INF)
        band_bias_ref[...] = jnp.where((dist >= 0) & (dist <= blk), 0.0, NEG_INF)

    head_ones = [jnp.where(m, 1.0, 0.0).astype(BF16) for m in _head_lane_masks()]
    low_head = lax.broadcasted_iota(jnp.int32, (1, V7X_LANES), 1) < HEAD_DIM
    for r in range(dil):
        qc = slice(r * GROUP_WIDTH, (r + 1) * GROUP_WIDTH)
        kc = slice(2 * r * GROUP_WIDTH, (2 * r + 1) * GROUP_WIDTH)
        vc = slice((2 * r + 1) * GROUP_WIDTH, (2 * r + 2) * GROUP_WIDTH)
        for i in range(seq // (dil * blk)):
            krows = slice(max(i - 1, 0) * blk, (i + 1) * blk)
            outs, lses = _band_block(q_ref[0, i * blk:(i + 1) * blk, qc], kv_ref[0, krows, kc].T, kv_ref[0, krows, vc],
                                     first_bias_ref if i == 0 else band_bias_ref, head_ones, low_head)
            for c in range(ATTN_PLANES):
                o_ref[0, c, r, i * blk:(i + 1) * blk, :] = outs[c]
                l_ref[0, c, r, i * blk:(i + 1) * blk, :] = lses[c]


def _merge_groups(outs, lses, exp_fn):
    mx = functools.reduce(jnp.maximum, lses)
    ws = [exp_fn(l - mx) for l in lses]
    num = functools.reduce(lambda a, b: a + b, [w * o for w, o in zip(ws, outs)])
    return num / functools.reduce(lambda a, b: a + b, ws)


def _stream_attn_call(q, kv, dil, seq):
    b = q.shape[0]

    def whole(a):
        zeros = (0,) * (len(a.shape) - 1)
        return pl.BlockSpec((1,) + tuple(a.shape[1:]), lambda bi: (bi,) + zeros)

    planes = jax.ShapeDtypeStruct((b, ATTN_PLANES, dil, seq // dil, V7X_LANES), F32)
    pipelined = _nbytes((seq, QKV_GROUP_WIDTH), BF16) + 2 * _nbytes((seq, GROUP_WIDTH), F32)
    scratch = [pltpu.VMEM((KEYS_PER_QUERY, KEYS_PER_QUERY), F32),
               pltpu.VMEM((KEYS_PER_QUERY, 2 * KEYS_PER_QUERY), F32)]
    scratch_bytes = 3 * _nbytes((KEYS_PER_QUERY, KEYS_PER_QUERY), F32)
    return pl.pallas_call(
        functools.partial(_stream_attn_kernel, dil=dil, seq=seq),
        grid=(b,),
        in_specs=[whole(q), whole(kv)],
        out_specs=[whole(planes), whole(planes)],
        out_shape=[planes, planes],
        scratch_shapes=scratch,
        compiler_params=pltpu.CompilerParams(dimension_semantics=("arbitrary",),
                                             vmem_limit_bytes=_vmem_limit(pipelined, scratch_bytes, 16 << 20)),
        name=f"stream_attn_d{dil}",
    )(q, kv)


def _token_order(ref, plane, stage_ref, slot):
    dil, rows = ref.shape[2], ref.shape[3]
    if dil == 1:
        return ref[0, plane, 0]
    if dil <= MAX_FAST_SUBLANE_STRIDE:
        for r in range(dil):
            stage_ref[slot, pl.ds(r, rows, stride=dil), :] = ref[0, plane, r]
        return stage_ref[slot]
    inner = MAX_FAST_SUBLANE_STRIDE
    outer = dil // inner
    assert outer <= MAX_FAST_SUBLANE_STRIDE and outer * inner == dil
    tmp = stage_ref.shape[0] - 1
    mid = rows * outer
    for b in range(inner):
        for a in range(outer):
            stage_ref[tmp, pl.ds(b * mid + a, rows, stride=outer), :] = ref[0, plane, inner * a + b]
    for b in range(inner):
        stage_ref[slot, pl.ds(b, mid, stride=inner), :] = stage_ref[tmp, b * mid:(b + 1) * mid, :]
    return stage_ref[slot]


def _mix_out_kernel(x_ref, pp_ref, ga_ref, *rest, n_groups):
    if n_groups:
        group_refs, rest = rest[:2 * n_groups], rest[2 * n_groups:]
        stage_ref, attn_ref = rest[-2:]

        @pl.when(pl.program_id(0) == 0)
        def _():
            attn_ref[...] = jnp.zeros(attn_ref.shape, BF16)
    else:
        attn_ref, rest = rest[0], rest[1:]
    wabr_ref, wo_ref, g2_ref, wg_ref, wu_ref, wd_ref, gf_ref, y_ref = rest[:8]
    attn_br = jnp.dot(attn_ref[...], wabr_ref[...], preferred_element_type=F32)
    merged = pp_ref[...] + ga_ref[...] * attn_br
    x = x_ref[...] + jnp.dot(merged.astype(BF16), wo_ref[...], preferred_element_type=F32)
    x = _macaron_ffn(x, g2_ref, wg_ref, wu_ref, wd_ref)
    y_ref[...] = _rmsnorm(x, gf_ref[...])
    if n_groups:
        for c in range(ATTN_PLANES):
            outs = [_token_order(group_refs[2 * g], c, stage_ref, 2 * g) for g in range(n_groups)]
            lses = [_token_order(group_refs[2 * g + 1], c, stage_ref, 2 * g + 1) for g in range(n_groups)]
            attn_ref[:, c * V7X_LANES:(c + 1) * V7X_LANES] = _merge_groups(outs, lses, jnp.exp2).astype(BF16)


def _mix_out_call(x, pp, ga, attn, wabr, wo, g2, wg, wu, wd, gf, tm, group_planes=()):
    m, d = x.shape
    f = wg.shape[1]
    n_tiles = m // tm
    pipelined = 4 * _nbytes((tm, d), F32)
    scratch = []
    if group_planes:
        steps = n_tiles + 1
        row = pl.BlockSpec((tm, d), lambda i: (jnp.maximum(i - 1, 0), 0))
        attn_ins, attn_specs = list(group_planes), []
        for p in group_planes:
            dil, seq_rows = p.shape[2], p.shape[3]
            tiles = seq_rows * dil // tm
            attn_specs.append(pl.BlockSpec(
                (1, ATTN_PLANES, dil, tm // dil, V7X_LANES),
                lambda i, tiles=tiles: (jnp.minimum(i, n_tiles - 1) // tiles, 0, 0,
                                        jnp.minimum(i, n_tiles - 1) % tiles, 0)))
        pipelined += len(group_planes) * _nbytes((tm, GROUP_WIDTH), F32)
        scratch = [pltpu.VMEM((len(group_planes) + 1, tm, V7X_LANES), F32), pltpu.VMEM((tm, GROUP_WIDTH), BF16)]
    else:
        steps = n_tiles
        row = pl.BlockSpec((tm, d), lambda i: (i, 0))
        attn_ins, attn_specs = [attn], [pl.BlockSpec((tm, GROUP_WIDTH), lambda i: (i, 0))]
        pipelined += _nbytes((tm, GROUP_WIDTH), BF16)
    resident = 3 * _nbytes((d, f), BF16) + _nbytes((d, d), BF16) + _nbytes((GROUP_WIDTH, d), BF16)
    temps = 4 * _nbytes((tm, FFN_CHUNK), F32) + 4 * _nbytes((tm, d), F32)
    return pl.pallas_call(
        functools.partial(_mix_out_kernel, n_groups=len(group_planes) // 2),
        grid=(steps,),
        in_specs=[row, row, row] + attn_specs
                 + [_resident((GROUP_WIDTH, d)), _resident((d, d)), _resident((1, d)),
                    _resident((d, f)), _resident((d, f)), _resident((f, d)), _resident((1, d))],
        out_specs=row,
        out_shape=jax.ShapeDtypeStruct((m, d), F32),
        scratch_shapes=scratch,
        compiler_params=pltpu.CompilerParams(dimension_semantics=("arbitrary",),
                                             vmem_limit_bytes=_vmem_limit(pipelined, resident, temps)),
        name="mix_out",
    )(x, pp, ga, *attn_ins, wabr, wo, g2, wg, wu, wd, gf)


def _sample_mix_in_kernel(x_ref, g_ref, win_ref, state_ref, cos_ref, sin_ref, wgrp_ref, pscale_ref, wpbr_ref,
                          qkv0_ref, qkv1_ref, qkv2_ref, u_ref, pp_ref, ga_ref):
    h = _rmsnorm(x_ref[...], g_ref[...]).astype(BF16)
    u = jnp.dot(h, win_ref[:, :POOL_WIDTH], preferred_element_type=F32)
    u_ref[...] = u
    pooled = []
    for gi, w in enumerate(POOL_WINDOWS):
        lo, hi = gi * POOL_GROUP, (gi + 1) * POOL_GROUP
        tok = u[:, lo:hi]
        total = tok
        for back in range(1, w):
            total = total + state_ref[POOL_STATE - back, :, lo:hi]
        pooled.append(total / float(min(PAST_LEN + 1, w)) - tok)
    pool_br = _pool_branch(pooled, wgrp_ref, pscale_ref, wpbr_ref)
    gate_pool, gate_attn = _gates(h, win_ref)
    pp_ref[...] = gate_pool * pool_br
    ga_ref[...] = gate_attn
    first_half = _first_half_mask()
    n = x_ref.shape[0]
    for g, qkv_ref in enumerate((qkv0_ref, qkv1_ref, qkv2_ref)):
        qkv = _qkv_group(h, win_ref, g, cos_ref[...], sin_ref[...], first_half, ATTN_SCALE)
        padded = jnp.concatenate([qkv, jnp.zeros((V7X_LANES - n, QKV_GROUP_WIDTH), F32)], axis=0)
        qkv_ref[...] = jnp.transpose(padded)[:, :n]


def _sample_mix_in_call(x, g, win, state_t, cos, sin, wgrp, pscale, wpbr):
    n, d = x.shape
    n_g = len(ATTN_GROUPS)

    def full(shape):
        zeros = (0,) * len(shape)
        return pl.BlockSpec(shape, lambda i: zeros)

    assert n <= V7X_LANES
    out_shape = ([jax.ShapeDtypeStruct((QKV_GROUP_WIDTH, n), F32)] * n_g
                 + [jax.ShapeDtypeStruct((n, POOL_WIDTH), F32),
                    jax.ShapeDtypeStruct((n, d), F32), jax.ShapeDtypeStruct((n, d), F32)])
    ins = (x, g, win, state_t, cos, sin, wgrp, pscale, wpbr)
    resident = sum(_nbytes(a.shape, a.dtype) for a in ins)
    return pl.pallas_call(
        _sample_mix_in_kernel,
        grid=(1,),
        in_specs=[full(a.shape) for a in ins],
        out_specs=[full(o.shape) for o in out_shape],
        out_shape=out_shape,
        compiler_params=pltpu.CompilerParams(dimension_semantics=("arbitrary",),
                                             vmem_limit_bytes=_vmem_limit(resident, 0, 8 << 20)),
        name="sample_mix_in",
    )(*ins)


def _head_sums(x):
    return jnp.concatenate([jnp.sum(x[hd * HEAD_DIM:(hd + 1) * HEAD_DIM], axis=0, keepdims=True)
                            for hd in range(HEADS)], axis=0)


def _head_expand(y):
    return jnp.concatenate([jnp.broadcast_to(y[hd:hd + 1], (HEAD_DIM, y.shape[1])) for hd in range(HEADS)],
                           axis=0)


def _cache_step_kernel(qkv0_ref, qkv1_ref, qkv2_ref, c0_ref, c1_ref, c2_ref,
                       n0_ref, n1_ref, n2_ref, attn_ref):
    n = pl.program_id(0)
    outs, lses = [], []
    groups = ((qkv0_ref, c0_ref, n0_ref), (qkv1_ref, c1_ref, n1_ref), (qkv2_ref, c2_ref, n2_ref))
    request = lax.broadcasted_iota(jnp.int32, qkv0_ref.shape, 1) == n
    for (qkv_ref, c_ref, new_ref), (window, dil) in zip(groups, ATTN_GROUPS):
        qkv_col = jnp.sum(jnp.where(request, qkv_ref[...], 0.0), axis=1, keepdims=True)
        q = qkv_col[:GROUP_WIDTH]
        k_new = qkv_col[GROUP_WIDTH:2 * GROUP_WIDTH]
        v_new = qkv_col[2 * GROUP_WIDTH:]
        kt = c_ref[0, :GROUP_WIDTH, :]
        vt = c_ref[0, GROUP_WIDTH:, :]
        lane = lax.broadcasted_iota(jnp.int32, (1, window), 1)
        sc = jnp.where(lane % dil == 0, _head_sums(kt * q), NEG_INF)
        sc_new = _head_sums(k_new * q)
        mx = jnp.maximum(jnp.max(sc, axis=1, keepdims=True), sc_new)
        e = jnp.exp(sc - mx)
        e_new = jnp.exp(sc_new - mx)
        den = jnp.sum(e, axis=1, keepdims=True) + e_new
        o_un = jnp.sum(vt * _head_expand(e), axis=1, keepdims=True) + _head_expand(e_new) * v_new
        outs.append(o_un / _head_expand(den))
        lses.append(_head_expand(mx + jnp.log(den)))
        shifted = pltpu.roll(c_ref[0], window - 1, 1)
        new_ref[0] = jnp.where(lane == window - 1, qkv_col[GROUP_WIDTH:], shifted)
    attn_col = _merge_groups(outs, lses, jnp.exp)
    col = lax.broadcasted_iota(jnp.int32, attn_ref.shape, 1)

    @pl.when(n == 0)
    def _():
        attn_ref[...] = jnp.zeros(attn_ref.shape, F32)

    attn_ref[...] = jnp.where(col == n, attn_col, attn_ref[...])


def _cache_step_call(qkv_cols, windows_t):
    n = qkv_cols[0].shape[1]
    qspec = pl.BlockSpec((QKV_GROUP_WIDTH, n), lambda i: (0, 0))
    cspecs = [pl.BlockSpec((1,) + c.shape[1:], lambda i: (i, 0, 0)) for c in windows_t]
    pipelined = 2 * sum(_nbytes(c.shape[1:], F32) for c in windows_t) + 3 * _nbytes((QKV_GROUP_WIDTH, V7X_LANES), F32)
    res = pl.pallas_call(
        _cache_step_kernel,
        grid=(n,),
        in_specs=[qspec] * len(qkv_cols) + cspecs,
        out_specs=cspecs + [pl.BlockSpec((GROUP_WIDTH, n), lambda i: (0, 0))],
        out_shape=[jax.ShapeDtypeStruct(c.shape, F32) for c in windows_t]
                  + [jax.ShapeDtypeStruct((GROUP_WIDTH, n), F32)],
        compiler_params=pltpu.CompilerParams(dimension_semantics=("arbitrary",),
                                             vmem_limit_bytes=_vmem_limit(pipelined, 0, 16 << 20)),
        name="cache_step",
    )(*qkv_cols, *windows_t)
    return res[:-1], res[-1]


def _rope_tables(pos):
    half = HEAD_DIM // 2
    inv = ROPE_THETA ** (-jnp.arange(half, dtype=F32) * (2.0 / HEAD_DIM))
    ang = pos.astype(F32)[:, None] * inv[None, :]
    cos, sin = jnp.cos(ang), jnp.sin(ang)
    reps = V7X_LANES // HEAD_DIM
    return (jnp.tile(jnp.concatenate([cos, cos], axis=1), (1, reps)),
            jnp.tile(jnp.concatenate([-sin, sin], axis=1), (1, reps)))


def _window_to_lanes(w):
    n, width = w.shape[:2]
    return jnp.transpose(w, (0, 2, 3, 4, 1)).reshape(n, 2 * GROUP_WIDTH, width)


def _window_from_lanes(wt):
    n, _, width = wt.shape
    return jnp.transpose(wt.reshape(n, 2, HEADS, HEAD_DIM, width), (0, 4, 1, 2, 3))[None]


def kernel(x_prompt, x_sample, state_pool, cache_kv_w128, cache_kv_w512, cache_kv_w2048, norm_ffn1, ffn1_w_gate, ffn1_w_up, ffn1_w_down, norm_mix, w_in, w_pool_grp, pool_scale, w_pool_br, w_attn_br, w_o, norm_ffn2, ffn2_w_gate, ffn2_w_up, ffn2_w_down, norm_final):
    b, s, d = x_prompt.shape
    n, dec_seq, _ = x_sample.shape
    caches = (cache_kv_w128, cache_kv_w512, cache_kv_w2048)
    assert norm_ffn1.shape[0] == 1 and dec_seq == 1, "single layer, single decode token"
    ts = PROMPT_ROW_TILE
    assert s % ts == 0 and ts >= POOL_CARRY
    for cache, (window, dil) in zip(caches, ATTN_GROUPS):
        assert window // dil == KEYS_PER_QUERY and s % (dil * KEYS_PER_QUERY) == 0 and ts % (16 * dil) == 0
        assert cache.shape[2] == window, "decode path expects full windows of history"

    row = lambda v: v.reshape(1, -1)
    bf = lambda w: w.astype(BF16)
    g1, g2, gm, gf = row(norm_ffn1[0]), row(norm_ffn2[0]), row(norm_mix[0]), row(norm_final)
    f1 = (bf(ffn1_w_gate[0]), bf(ffn1_w_up[0]), bf(ffn1_w_down[0]))
    f2 = (bf(ffn2_w_gate[0]), bf(ffn2_w_up[0]), bf(ffn2_w_down[0]))
    win, wgrp, wpbr, wabr, wo = bf(w_in[0]), bf(w_pool_grp[0]), bf(w_pool_br[0]), bf(w_attn_br[0]), bf(w_o[0])
    pscale = row(pool_scale[0])

    cos_p, sin_p = _rope_tables(jnp.arange(s, dtype=jnp.int32))
    x1 = _ffn_call(x_prompt.reshape(b * s, d), g1, *f1, tm=FFN_ROW_TILE)
    (q0, q1, q2, kv0, kv1, kv2, kvt0, kvt1, kvt2, tail, pp, ga) = _prompt_mix_in_call(
        x1.reshape(b, s, d), gm, win, cos_p, sin_p, wgrp, pscale, wpbr, ts)
    group_planes = []
    for q, kv, (_, dil) in zip((q0, q1, q2), (kv0, kv1, kv2), ATTN_GROUPS):
        group_planes += _stream_attn_call(q, kv, dil, s)
    y_prompt = _mix_out_call(x1, pp.reshape(b * s, d), ga.reshape(b * s, d), None, wabr, wo, g2, *f2, gf,
                             tm=ts, group_planes=tuple(group_planes)).reshape(b, s, d)
    pool_prompt = tail[None, :, POOL_CARRY - POOL_STATE:, :]

    cos_s, sin_s = _rope_tables(PAST_LEN + jnp.arange(dec_seq, dtype=jnp.int32))
    xs1 = _ffn_call(x_sample.reshape(n, d), g1, *f1, tm=n)
    state_t = jnp.swapaxes(state_pool[0], 0, 1)
    (sqkv0, sqkv1, sqkv2, u_new, spp, sga) = _sample_mix_in_call(
        xs1, gm, win, state_t, cos_s, sin_s, wgrp, pscale, wpbr)
    new_windows, attn_t = _cache_step_call((sqkv0, sqkv1, sqkv2),
                                           [_window_to_lanes(c[0]) for c in caches])
    y_sample = _mix_out_call(xs1, spp, sga, attn_t.T.astype(BF16), wabr, wo, g2, *f2, gf,
                             tm=n).reshape(n, dec_seq, d)
    pool_sample = jnp.swapaxes(jnp.concatenate([state_t[1:], u_new[None]], axis=0), 0, 1)[None]

    return (y_prompt, y_sample, pool_prompt,
            _window_from_lanes(kvt0), _window_from_lanes(kvt1), _window_from_lanes(kvt2),
            pool_sample, *[_window_from_lanes(w) for w in new_windows])
```

```python
import functools

import jax
import jax.numpy as jnp
from jax import lax
from jax.experimental import pallas as pl
from jax.experimental.pallas import tpu as pltpu

F32 = jnp.float32
BF16 = jnp.bfloat16

PAST_LEN = 16384
POOL_WINDOWS = (2, 4, 8, 16)
POOL_GROUP = 128
POOL_WIDTH = POOL_GROUP * len(POOL_WINDOWS)
POOL_STATE = max(POOL_WINDOWS) - 1
POOL_CARRY = 16
HEAD_DIM = 64
HEADS = 4
GROUP_WIDTH = HEADS * HEAD_DIM
ATTN_GROUPS = ((128, 1), (512, 4), (2048, 16))
KEYS_PER_QUERY = 128
QKV_GROUP_WIDTH = 3 * GROUP_WIDTH
ROPE_THETA = 10000.0
RMS_EPS = 1e-6
NEG_INF = -1e30
ATTN_SCALE = HEAD_DIM ** -0.5
LOG2_E = 1.4426950408889634

V7X_VMEM_BYTES = 64 * 1024 * 1024
V7X_LANES = 128
V7X_MXU_COLS = 256
MAX_FAST_SUBLANE_STRIDE = 4

PROMPT_ROW_TILE = 512
FFN_ROW_TILE = 1024
FFN_CHUNK = 4 * V7X_MXU_COLS


def _vmem_limit(pipelined_bytes, resident_bytes, temp_bytes):
    need = 2 * pipelined_bytes + resident_bytes + temp_bytes
    return int(min(need, V7X_VMEM_BYTES - (4 << 20)))


def _nbytes(shape, dtype):
    n = 1
    for s in shape:
        n *= s
    return n * jnp.dtype(dtype).itemsize


def _resident(shape):
    zeros = (0,) * len(shape)
    return pl.BlockSpec(shape, lambda *_: zeros, pipeline_mode=pl.Buffered(1))


def _rmsnorm(x, g):
    ms = jnp.mean(x * x, axis=-1, keepdims=True)
    return x * lax.rsqrt(ms + RMS_EPS) * g


def _ffn_chunks(width):
    return [(lo, min(lo + FFN_CHUNK, width)) for lo in range(0, width, FFN_CHUNK)]


def _swiglu(xn, wg_ref, wu_ref, wd_ref):
    acc = None
    for lo, hi in _ffn_chunks(wg_ref.shape[1]):
        gate = jnp.dot(xn, wg_ref[:, lo:hi], preferred_element_type=F32)
        up = jnp.dot(xn, wu_ref[:, lo:hi], preferred_element_type=F32)
        hid = (gate * jax.nn.sigmoid(gate) * up).astype(BF16)
        part = jnp.dot(hid, wd_ref[lo:hi, :], preferred_element_type=F32)
        acc = part if acc is None else acc + part
    return acc


def _macaron_ffn(x, g_ref, wg_ref, wu_ref, wd_ref):
    xn = _rmsnorm(x, g_ref[...]).astype(BF16)
    return x + 0.5 * _swiglu(xn, wg_ref, wu_ref, wd_ref)


def _ffn_kernel(x_ref, g_ref, wg_ref, wu_ref, wd_ref, *rest):
    n_cast = (len(rest) - 1) // 2
    o_ref = rest[n_cast]
    o_ref[...] = _macaron_ffn(x_ref[...], g_ref, wg_ref, wu_ref, wd_ref)
    for src_ref, dst_ref in zip(rest[:n_cast], rest[n_cast + 1:]):
        dst_ref[...] = src_ref[...].astype(BF16)


BF16_SUBLANE_ROWS = 16


def _ffn_call(x, g, wg, wu, wd, tm, cast=()):
    m, d = x.shape
    f = wg.shape[1]
    steps = m // tm
    row = pl.BlockSpec((tm, d), lambda i: (i, 0))
    cast_specs, cast_bytes = [], 0
    for w in cast:
        rows = w.shape[0]
        per_step = next(r for r in range(BF16_SUBLANE_ROWS, rows + 1, BF16_SUBLANE_ROWS)
                        if rows % r == 0 and r * steps >= rows)
        last = rows // per_step - 1
        cast_specs.append(pl.BlockSpec((per_step, w.shape[1]), lambda i, last=last: (jnp.minimum(i, last), 0)))
        cast_bytes += _nbytes((per_step, w.shape[1]), F32) + _nbytes((per_step, w.shape[1]), BF16)
    limit = _vmem_limit(2 * _nbytes((tm, d), F32) + cast_bytes,
                        3 * _nbytes((d, f), BF16),
                        4 * _nbytes((tm, FFN_CHUNK), F32) + 2 * _nbytes((tm, d), F32))
    res = pl.pallas_call(
        _ffn_kernel,
        grid=(steps,),
        in_specs=[row, _resident((1, d)), _resident((d, f)), _resident((d, f)), _resident((f, d))] + cast_specs,
        out_specs=[row] + cast_specs,
        out_shape=[jax.ShapeDtypeStruct((m, d), F32)] + [jax.ShapeDtypeStruct(w.shape, BF16) for w in cast],
        compiler_params=pltpu.CompilerParams(dimension_semantics=("arbitrary",), vmem_limit_bytes=limit),
        name="ffn",
    )(x, g, wg, wu, wd, *cast)
    return res[0], res[1:]


def _rope(t, cos, sin_signed, first_half):
    outs = []
    for c in range(t.shape[1] // V7X_LANES):
        tc = t[:, c * V7X_LANES:(c + 1) * V7X_LANES]
        partner = jnp.where(first_half,
                            pltpu.roll(tc, V7X_LANES - HEAD_DIM // 2, 1),
                            pltpu.roll(tc, HEAD_DIM // 2, 1))
        outs.append(tc * cos + partner * sin_signed)
    return jnp.concatenate(outs, axis=1)


def _first_half_mask():
    lane = lax.broadcasted_iota(jnp.int32, (1, V7X_LANES), 1)
    return (lane % HEAD_DIM) < (HEAD_DIM // 2)


def _qkv_group(h, win_ref, g, cos, sin_signed, first_half, q_scale):
    base = POOL_WIDTH + g * QKV_GROUP_WIDTH
    qkv = jnp.dot(h, win_ref[:, base:base + QKV_GROUP_WIDTH], preferred_element_type=F32)
    q = _rope(qkv[:, :GROUP_WIDTH], cos, sin_signed, first_half) * q_scale
    k = _rope(qkv[:, GROUP_WIDTH:2 * GROUP_WIDTH], cos, sin_signed, first_half)
    return jnp.concatenate([q, k, qkv[:, 2 * GROUP_WIDTH:]], axis=1)


def _pool_branch(window_means_minus_tok, wgrp_ref, pscale_ref, wpbr_ref):
    mixed = [jnp.dot(p.astype(BF16), wgrp_ref[gi], preferred_element_type=F32)
             for gi, p in enumerate(window_means_minus_tok)]
    pool = jnp.concatenate(mixed, axis=1) * pscale_ref[...]
    return jnp.dot(pool.astype(BF16), wpbr_ref[...], preferred_element_type=F32)


def _gates(h, win_ref):
    d = win_ref.shape[0]
    base = POOL_WIDTH + len(ATTN_GROUPS) * QKV_GROUP_WIDTH
    gate_pool = jax.nn.sigmoid(jnp.dot(h, win_ref[:, base:base + d], preferred_element_type=F32))
    gate_attn = jax.nn.sigmoid(jnp.dot(h, win_ref[:, base + d:base + 2 * d], preferred_element_type=F32))
    return gate_pool, gate_attn


def _prompt_mix_in_kernel(x_ref, g_ref, win_ref, cos_ref, sin_ref, wgrp_ref, pscale_ref, wpbr_ref,
                          q0_ref, q1_ref, q2_ref, kv0_ref, kv1_ref, kv2_ref,
                          kvt0_ref, kvt1_ref, kvt2_ref, tail_ref, pp_ref, ga_ref, uext_ref, qkv_ref, mid_ref,
                          *, ts):
    j = pl.program_id(1)

    @pl.when(j == 0)
    def _():
        uext_ref[0:POOL_CARRY, :] = jnp.zeros((POOL_CARRY, POOL_WIDTH), F32)

    h = _rmsnorm(x_ref[0], g_ref[...]).astype(BF16)

    cos = cos_ref[...]
    sin_signed = sin_ref[...]
    first_half = _first_half_mask()
    outs = ((q0_ref, kv0_ref, kvt0_ref), (q1_ref, kv1_ref, kvt1_ref), (q2_ref, kv2_ref, kvt2_ref))
    planes = QKV_GROUP_WIDTH // V7X_LANES
    slot = 0
    for g in sorted(range(len(ATTN_GROUPS)), key=lambda g: -ATTN_GROUPS[g][1]):
        q_ref, kv_ref, kvt_ref = outs[g]
        dil = ATTN_GROUPS[g][1]
        qkv = _qkv_group(h, win_ref, g, cos, sin_signed, first_half, ATTN_SCALE * LOG2_E)
        if dil == 1:
            q_ref[0] = qkv[:, :GROUP_WIDTH].astype(BF16)
            kv_ref[0] = qkv[:, GROUP_WIDTH:].astype(BF16)
        else:
            for c in range(planes):
                qkv_ref[slot + c] = qkv[:, c * V7X_LANES:(c + 1) * V7X_LANES]
            inner = min(dil, MAX_FAST_SUBLANE_STRIDE)
            outer = dil // inner
            assert outer <= MAX_FAST_SUBLANE_STRIDE and outer * inner == dil
            if outer > 1:
                for c in range(planes):
                    for bb in range(inner):
                        mid_ref[c, bb * (ts // inner):(bb + 1) * (ts // inner), :] = (
                            qkv_ref[slot + c, pl.ds(bb, ts // inner, stride=inner), :])
            for r in range(dil):
                if outer > 1:
                    a, bb = r // inner, r % inner
                    rows = jnp.concatenate(
                        [mid_ref[c, pl.ds(bb * (ts // inner) + a, ts // dil, stride=outer), :]
                         for c in range(planes)], axis=1)
                else:
                    rows = jnp.concatenate([qkv_ref[slot + c, pl.ds(r, ts // dil, stride=dil), :]
                                            for c in range(planes)], axis=1)
                q_ref[0, :, r * GROUP_WIDTH:(r + 1) * GROUP_WIDTH] = rows[:, :GROUP_WIDTH].astype(BF16)
                kv_ref[0, :, 2 * r * GROUP_WIDTH:2 * (r + 1) * GROUP_WIDTH] = rows[:, GROUP_WIDTH:].astype(BF16)
            slot += planes
        keep = kvt_ref.shape[2]
        kvt_ref[0] = qkv[ts - keep:, GROUP_WIDTH:].T

    u = jnp.dot(h, win_ref[:, :POOL_WIDTH], preferred_element_type=F32)
    uext_ref[POOL_CARRY:POOL_CARRY + ts, :] = u
    pos = j * ts + lax.broadcasted_iota(jnp.int32, (ts, 1), 0)
    pooled = []
    for gi, w in enumerate(POOL_WINDOWS):
        lo, hi = gi * POOL_GROUP, (gi + 1) * POOL_GROUP
        tok = u[:, lo:hi]
        total = tok
        for back in range(1, w):
            total = total + uext_ref[POOL_CARRY - back:POOL_CARRY - back + ts, lo:hi]
        cnt = jnp.minimum(pos + 1, w).astype(F32)
        pooled.append(total / cnt - tok)
    pool_br = _pool_branch(pooled, wgrp_ref, pscale_ref, wpbr_ref)
    uext_ref[0:POOL_CARRY, :] = u[ts - POOL_CARRY:, :]
    tail_ref[0] = u[ts - POOL_CARRY:, :]
    gate_pool, gate_attn = _gates(h, win_ref)
    pp_ref[0] = gate_pool * pool_br
    ga_ref[0] = gate_attn


def _prompt_mix_in_call(x, g, win, cos, sin, wgrp, pscale, wpbr, ts):
    b, s, d = x.shape
    seq_tiles = s // ts
    zw = win.shape[1]

    def tile(width):
        return pl.BlockSpec((1, ts, width), lambda bi, j: (bi, j, 0))

    q_specs, kv_specs, q_shapes, kv_shapes, kvt_specs, kvt_shapes = [], [], [], [], [], []
    for window, dil in ATTN_GROUPS:
        q_specs.append(pl.BlockSpec((1, ts // dil, dil * GROUP_WIDTH), lambda bi, j: (bi, j, 0)))
        kv_specs.append(pl.BlockSpec((1, ts // dil, 2 * dil * GROUP_WIDTH), lambda bi, j: (bi, j, 0)))
        q_shapes.append(jax.ShapeDtypeStruct((b, s // dil, dil * GROUP_WIDTH), BF16))
        kv_shapes.append(jax.ShapeDtypeStruct((b, s // dil, 2 * dil * GROUP_WIDTH), BF16))
        keep = min(window, s)
        if keep >= ts:
            first = seq_tiles - keep // ts
            kvt_specs.append(pl.BlockSpec((1, 2 * GROUP_WIDTH, ts),
                                          lambda bi, j, first=first: (bi, 0, jnp.maximum(j - first, 0))))
        else:
            kvt_specs.append(pl.BlockSpec((1, 2 * GROUP_WIDTH, keep), lambda bi, j: (bi, 0, 0)))
        kvt_shapes.append(jax.ShapeDtypeStruct((b, 2 * GROUP_WIDTH, keep), F32))
    n_g = len(ATTN_GROUPS)
    out_specs = (q_specs + kv_specs + kvt_specs
                 + [pl.BlockSpec((1, POOL_CARRY, POOL_WIDTH), lambda bi, j: (bi, 0, 0)), tile(d), tile(d)])
    out_shape = (q_shapes + kv_shapes + kvt_shapes
                 + [jax.ShapeDtypeStruct((b, POOL_CARRY, POOL_WIDTH), F32),
                    jax.ShapeDtypeStruct((b, s, d), F32), jax.ShapeDtypeStruct((b, s, d), F32)])
    table = pl.BlockSpec((ts, V7X_LANES), lambda bi, j: (j, 0))
    pipelined = (3 * _nbytes((ts, d), F32) + n_g * _nbytes((ts, QKV_GROUP_WIDTH), BF16)
                 + n_g * _nbytes((ts, 2 * GROUP_WIDTH), F32) + 2 * _nbytes((ts, V7X_LANES), F32))
    resident = _nbytes((d, zw), BF16) + _nbytes(wpbr.shape, BF16) + _nbytes(wgrp.shape, BF16)
    n_dilated = sum(1 for _, dil in ATTN_GROUPS if dil != 1)
    scratch = [pltpu.VMEM((ts + POOL_CARRY, POOL_WIDTH), F32),
               pltpu.VMEM((n_dilated * QKV_GROUP_WIDTH // V7X_LANES, ts, V7X_LANES), F32),
               pltpu.VMEM((QKV_GROUP_WIDTH // V7X_LANES, ts, V7X_LANES), F32)]
    temps = (_nbytes((ts + POOL_CARRY, POOL_WIDTH), F32) + (n_dilated + 1) * _nbytes((ts, QKV_GROUP_WIDTH), F32)
             + 6 * _nbytes((ts, d), F32))
    return pl.pallas_call(
        functools.partial(_prompt_mix_in_kernel, ts=ts),
        grid=(b, seq_tiles),
        in_specs=[tile(d), _resident((1, d)), _resident((d, zw)), table, table,
                  _resident(wgrp.shape), _resident((1, POOL_WIDTH)), _resident(wpbr.shape)],
        out_specs=out_specs,
        out_shape=out_shape,
        scratch_shapes=scratch,
        compiler_params=pltpu.CompilerParams(dimension_semantics=("arbitrary", "arbitrary"),
                                             vmem_limit_bytes=_vmem_limit(pipelined, resident, temps)),
        name="prompt_mix_in",
    )(x, g, win, cos, sin, wgrp, pscale, wpbr)


def _head_lane_masks():
    lane = lax.broadcasted_iota(jnp.int32, (1, GROUP_WIDTH), 1)
    return [(lane // HEAD_DIM) == hd for hd in range(HEADS)]


SOFTMAX_ROWS = 128


def _band_block(q, kt, v, bias_ref, head_ones, low_head):
    blk, keys = q.shape[0], v.shape[0]
    q_heads = jnp.concatenate([q * one for one in head_ones], axis=0)
    sc = jnp.dot(q_heads, kt, preferred_element_type=F32)
    outs, lses = [], []
    for hd in range(HEADS):
        es, mxs, dens = [], [], []
        for r0 in range(0, blk, SOFTMAX_ROWS):
            s_c = sc[hd * blk + r0:hd * blk + r0 + SOFTMAX_ROWS, :] + bias_ref[r0:r0 + SOFTMAX_ROWS, :keys]
            mx_c = jnp.max(s_c, axis=1, keepdims=True)
            e_c = jnp.exp2(s_c - mx_c)
            mxs.append(mx_c)
            dens.append(jnp.sum(e_c, axis=1, keepdims=True))
            es.append(e_c.astype(BF16))
        den = jnp.concatenate(dens, axis=0)
        tile = slice((hd * HEAD_DIM // V7X_LANES) * V7X_LANES, (hd * HEAD_DIM // V7X_LANES + 1) * V7X_LANES)
        o_un = jnp.dot(jnp.concatenate(es, axis=0), v[:, tile], preferred_element_type=F32)
        outs.append(o_un * (1.0 / den))
        lses.append(jnp.broadcast_to(jnp.concatenate(mxs, axis=0) + jnp.log2(den), (blk, V7X_LANES)))
    pairs = range(0, HEADS, V7X_LANES // HEAD_DIM)
    return ([jnp.where(low_head, outs[hd], outs[hd + 1]) for hd in pairs],
            [jnp.where(low_head, lses[hd], lses[hd + 1]) for hd in pairs])


ATTN_PLANES = GROUP_WIDTH // V7X_LANES


def _stream_attn_kernel(q_ref, kv_ref, o_ref, l_ref, first_bias_ref, band_bias_ref, *, dil, seq):
    blk = KEYS_PER_QUERY

    @pl.when(pl.program_id(0) == 0)
    def _():
        qi = lax.broadcasted_iota(jnp.int32, (blk, 1), 0)
        first = lax.broadcasted_iota(jnp.int32, (1, blk), 1) <= qi
        dist = qi + blk - lax.broadcasted_iota(jnp.int32, (1, 2 * blk), 1)
        first_bias_ref[...] = jnp.where(first, 0.0, NEG_INF)
        band_bias_ref[...] = jnp.where((dist >= 0) & (dist <= blk), 0.0, NEG_INF)

    head_ones = [jnp.where(m, 1.0, 0.0).astype(BF16) for m in _head_lane_masks()]
    low_head = lax.broadcasted_iota(jnp.int32, (1, V7X_LANES), 1) < HEAD_DIM
    for r in range(dil):
        qc = slice(r * GROUP_WIDTH, (r + 1) * GROUP_WIDTH)
        kc = slice(2 * r * GROUP_WIDTH, (2 * r + 1) * GROUP_WIDTH)
        vc = slice((2 * r + 1) * GROUP_WIDTH, (2 * r + 2) * GROUP_WIDTH)
        for i in range(seq // (dil * blk)):
            krows = slice(max(i - 1, 0) * blk, (i + 1) * blk)
            outs, lses = _band_block(q_ref[0, i * blk:(i + 1) * blk, qc], kv_ref[0, krows, kc].T, kv_ref[0, krows, vc],
                                     first_bias_ref if i == 0 else band_bias_ref, head_ones, low_head)
            for c in range(ATTN_PLANES):
                o_ref[0, c, r, i * blk:(i + 1) * blk, :] = outs[c]
                l_ref[0, c, r, i * blk:(i + 1) * blk, :] = lses[c]


def _merge_groups(outs, lses, exp_fn):
    mx = functools.reduce(jnp.maximum, lses)
    ws = [exp_fn(l - mx) for l in lses]
    num = functools.reduce(lambda a, b: a + b, [w * o for w, o in zip(ws, outs)])
    return num / functools.reduce(lambda a, b: a + b, ws)


def _stream_attn_call(q, kv, dil, seq):
    b = q.shape[0]

    def whole(a):
        zeros = (0,) * (len(a.shape) - 1)
        return pl.BlockSpec((1,) + tuple(a.shape[1:]), lambda bi: (bi,) + zeros)

    planes = jax.ShapeDtypeStruct((b, ATTN_PLANES, dil, seq // dil, V7X_LANES), F32)
    pipelined = _nbytes((seq, QKV_GROUP_WIDTH), BF16) + 2 * _nbytes((seq, GROUP_WIDTH), F32)
    scratch = [pltpu.VMEM((KEYS_PER_QUERY, KEYS_PER_QUERY), F32),
               pltpu.VMEM((KEYS_PER_QUERY, 2 * KEYS_PER_QUERY), F32)]
    scratch_bytes = 3 * _nbytes((KEYS_PER_QUERY, KEYS_PER_QUERY), F32)
    return pl.pallas_call(
        functools.partial(_stream_attn_kernel, dil=dil, seq=seq),
        grid=(b,),
        in_specs=[whole(q), whole(kv)],
        out_specs=[whole(planes), whole(planes)],
        out_shape=[planes, planes],
        scratch_shapes=scratch,
        compiler_params=pltpu.CompilerParams(dimension_semantics=("arbitrary",),
                                             vmem_limit_bytes=_vmem_limit(pipelined, scratch_bytes, 16 << 20)),
        name=f"stream_attn_d{dil}",
    )(q, kv)


def _token_order(ref, plane, stage_ref, slot):
    dil, rows = ref.shape[2], ref.shape[3]
    if dil == 1:
        return ref[0, plane, 0]
    if dil <= MAX_FAST_SUBLANE_STRIDE:
        for r in range(dil):
            stage_ref[slot, pl.ds(r, rows, stride=dil), :] = ref[0, plane, r]
        return stage_ref[slot]
    inner = MAX_FAST_SUBLANE_STRIDE
    outer = dil // inner
    assert outer <= MAX_FAST_SUBLANE_STRIDE and outer * inner == dil
    tmp = stage_ref.shape[0] - 1
    mid = rows * outer
    for b in range(inner):
        for a in range(outer):
            stage_ref[tmp, pl.ds(b * mid + a, rows, stride=outer), :] = ref[0, plane, inner * a + b]
    for b in range(inner):
        stage_ref[slot, pl.ds(b, mid, stride=inner), :] = stage_ref[tmp, b * mid:(b + 1) * mid, :]
    return stage_ref[slot]


def _mix_out_kernel(x_ref, pp_ref, ga_ref, *rest, n_groups):
    if n_groups:
        group_refs, rest = rest[:2 * n_groups], rest[2 * n_groups:]
        stage_ref, attn_ref = rest[-2:]

        @pl.when(pl.program_id(0) == 0)
        def _():
            attn_ref[...] = jnp.zeros(attn_ref.shape, BF16)
    else:
        attn_ref, rest = rest[0], rest[1:]
    wabr_ref, wo_ref, g2_ref, wg_ref, wu_ref, wd_ref, gf_ref, y_ref = rest[:8]
    attn_br = jnp.dot(attn_ref[...], wabr_ref[...], preferred_element_type=F32)
    merged = pp_ref[...] + ga_ref[...] * attn_br
    x = x_ref[...] + jnp.dot(merged.astype(BF16), wo_ref[...], preferred_element_type=F32)
    x = _macaron_ffn(x, g2_ref, wg_ref, wu_ref, wd_ref)
    y_ref[...] = _rmsnorm(x, gf_ref[...])
    if n_groups:
        for c in range(ATTN_PLANES):
            outs = [_token_order(group_refs[2 * g], c, stage_ref, 2 * g) for g in range(n_groups)]
            lses = [_token_order(group_refs[2 * g + 1], c, stage_ref, 2 * g + 1) for g in range(n_groups)]
            attn_ref[:, c * V7X_LANES:(c + 1) * V7X_LANES] = _merge_groups(outs, lses, jnp.exp2).astype(BF16)


def _mix_out_call(x, pp, ga, attn, wabr, wo, g2, wg, wu, wd, gf, tm, group_planes=()):
    m, d = x.shape
    f = wg.shape[1]
    n_tiles = m // tm
    pipelined = 4 * _nbytes((tm, d), F32)
    scratch = []
    if group_planes:
        steps = n_tiles + 1
        row = pl.BlockSpec((tm, d), lambda i: (jnp.maximum(i - 1, 0), 0))
        attn_ins, attn_specs = list(group_planes), []
        for p in group_planes:
            dil, seq_rows = p.shape[2], p.shape[3]
            tiles = seq_rows * dil // tm
            attn_specs.append(pl.BlockSpec(
                (1, ATTN_PLANES, dil, tm // dil, V7X_LANES),
                lambda i, tiles=tiles: (jnp.minimum(i, n_tiles - 1) // tiles, 0, 0,
                                        jnp.minimum(i, n_tiles - 1) % tiles, 0)))
        pipelined += len(group_planes) * _nbytes((tm, GROUP_WIDTH), F32)
        scratch = [pltpu.VMEM((len(group_planes) + 1, tm, V7X_LANES), F32), pltpu.VMEM((tm, GROUP_WIDTH), BF16)]
    else:
        steps = n_tiles
        row = pl.BlockSpec((tm, d), lambda i: (i, 0))
        attn_ins, attn_specs = [attn], [pl.BlockSpec((tm, GROUP_WIDTH), lambda i: (i, 0))]
        pipelined += _nbytes((tm, GROUP_WIDTH), BF16)
    resident = 3 * _nbytes((d, f), BF16) + _nbytes((d, d), BF16) + _nbytes((GROUP_WIDTH, d), BF16)
    temps = 4 * _nbytes((tm, FFN_CHUNK), F32) + 4 * _nbytes((tm, d), F32)
    return pl.pallas_call(
        functools.partial(_mix_out_kernel, n_groups=len(group_planes) // 2),
        grid=(steps,),
        in_specs=[row, row, row] + attn_specs
                 + [_resident((GROUP_WIDTH, d)), _resident((d, d)), _resident((1, d)),
                    _resident((d, f)), _resident((d, f)), _resident((f, d)), _resident((1, d))],
        out_specs=row,
        out_shape=jax.ShapeDtypeStruct((m, d), F32),
        scratch_shapes=scratch,
        compiler_params=pltpu.CompilerParams(dimension_semantics=("arbitrary",),
                                             vmem_limit_bytes=_vmem_limit(pipelined, resident, temps)),
        name="mix_out",
    )(x, pp, ga, *attn_ins, wabr, wo, g2, wg, wu, wd, gf)


def _sample_mix_in_kernel(x_ref, g_ref, win_ref, state_ref, cos_ref, sin_ref, wgrp_ref, pscale_ref, wpbr_ref,
                          qkv0_ref, qkv1_ref, qkv2_ref, u_ref, pp_ref, ga_ref):
    h = _rmsnorm(x_ref[...], g_ref[...]).astype(BF16)
    u = jnp.dot(h, win_ref[:, :POOL_WIDTH], preferred_element_type=F32)
    u_ref[...] = u
    pooled = []
    for gi, w in enumerate(POOL_WINDOWS):
        lo, hi = gi * POOL_GROUP, (gi + 1) * POOL_GROUP
        tok = u[:, lo:hi]
        total = tok
        for back in range(1, w):
            total = total + state_ref[POOL_STATE - back, :, lo:hi]
        pooled.append(total / float(min(PAST_LEN + 1, w)) - tok)
    pool_br = _pool_branch(pooled, wgrp_ref, pscale_ref, wpbr_ref)
    gate_pool, gate_attn = _gates(h, win_ref)
    pp_ref[...] = gate_pool * pool_br
    ga_ref[...] = gate_attn
    first_half = _first_half_mask()
    n = x_ref.shape[0]
    for g, qkv_ref in enumerate((qkv0_ref, qkv1_ref, qkv2_ref)):
        qkv = _qkv_group(h, win_ref, g, cos_ref[...], sin_ref[...], first_half, ATTN_SCALE)
        padded = jnp.concatenate([qkv, jnp.zeros((V7X_LANES - n, QKV_GROUP_WIDTH), F32)], axis=0)
        qkv_ref[...] = jnp.transpose(padded)[:, :n]


def _sample_mix_in_call(x, g, win, state_t, cos, sin, wgrp, pscale, wpbr):
    n, d = x.shape
    n_g = len(ATTN_GROUPS)

    def full(shape):
        zeros = (0,) * len(shape)
        return pl.BlockSpec(shape, lambda i: zeros)

    assert n <= V7X_LANES
    out_shape = ([jax.ShapeDtypeStruct((QKV_GROUP_WIDTH, n), F32)] * n_g
                 + [jax.ShapeDtypeStruct((n, POOL_WIDTH), F32),
                    jax.ShapeDtypeStruct((n, d), F32), jax.ShapeDtypeStruct((n, d), F32)])
    ins = (x, g, win, state_t, cos, sin, wgrp, pscale, wpbr)
    resident = sum(_nbytes(a.shape, a.dtype) for a in ins)
    return pl.pallas_call(
        _sample_mix_in_kernel,
        grid=(1,),
        in_specs=[full(a.shape) for a in ins],
        out_specs=[full(o.shape) for o in out_shape],
        out_shape=out_shape,
        compiler_params=pltpu.CompilerParams(dimension_semantics=("arbitrary",),
                                             vmem_limit_bytes=_vmem_limit(resident, 0, 8 << 20)),
        name="sample_mix_in",
    )(*ins)


def _head_sums(x):
    return jnp.concatenate([jnp.sum(x[hd * HEAD_DIM:(hd + 1) * HEAD_DIM], axis=0, keepdims=True)
                            for hd in range(HEADS)], axis=0)


def _head_expand(y):
    return jnp.concatenate([jnp.broadcast_to(y[hd:hd + 1], (HEAD_DIM, y.shape[1])) for hd in range(HEADS)],
                           axis=0)


def _cache_step_kernel(qkv0_ref, qkv1_ref, qkv2_ref, c0_ref, c1_ref, c2_ref,
                       n0_ref, n1_ref, n2_ref, attn_ref):
    n = pl.program_id(0)
    outs, lses = [], []
    groups = ((qkv0_ref, c0_ref, n0_ref), (qkv1_ref, c1_ref, n1_ref), (qkv2_ref, c2_ref, n2_ref))
    request = lax.broadcasted_iota(jnp.int32, qkv0_ref.shape, 1) == n
    for (qkv_ref, c_ref, new_ref), (window, dil) in zip(groups, ATTN_GROUPS):
        qkv_col = jnp.sum(jnp.where(request, qkv_ref[...], 0.0), axis=1, keepdims=True)
        q = qkv_col[:GROUP_WIDTH]
        k_new = qkv_col[GROUP_WIDTH:2 * GROUP_WIDTH]
        v_new = qkv_col[2 * GROUP_WIDTH:]
        kt = c_ref[0, :GROUP_WIDTH, :]
        vt = c_ref[0, GROUP_WIDTH:, :]
        lane = lax.broadcasted_iota(jnp.int32, (1, window), 1)
        sc = jnp.where(lane % dil == 0, _head_sums(kt * q), NEG_INF)
        sc_new = _head_sums(k_new * q)
        mx = jnp.maximum(jnp.max(sc, axis=1, keepdims=True), sc_new)
        e = jnp.exp(sc - mx)
        e_new = jnp.exp(sc_new - mx)
        den = jnp.sum(e, axis=1, keepdims=True) + e_new
        o_un = jnp.sum(vt * _head_expand(e), axis=1, keepdims=True) + _head_expand(e_new) * v_new
        outs.append(o_un / _head_expand(den))
        lses.append(_head_expand(mx + jnp.log(den)))
        shifted = pltpu.roll(c_ref[0], window - 1, 1)
        new_ref[0] = jnp.where(lane == window - 1, qkv_col[GROUP_WIDTH:], shifted)
    attn_col = _merge_groups(outs, lses, jnp.exp)
    col = lax.broadcasted_iota(jnp.int32, attn_ref.shape, 1)

    @pl.when(n == 0)
    def _():
        attn_ref[...] = jnp.zeros(attn_ref.shape, F32)

    attn_ref[...] = jnp.where(col == n, attn_col, attn_ref[...])


def _cache_step_call(qkv_cols, windows_t):
    n = qkv_cols[0].shape[1]
    qspec = pl.BlockSpec((QKV_GROUP_WIDTH, n), lambda i: (0, 0))
    cspecs = [pl.BlockSpec((1,) + c.shape[1:], lambda i: (i, 0, 0)) for c in windows_t]
    pipelined = 2 * sum(_nbytes(c.shape[1:], F32) for c in windows_t) + 3 * _nbytes((QKV_GROUP_WIDTH, V7X_LANES), F32)
    res = pl.pallas_call(
        _cache_step_kernel,
        grid=(n,),
        in_specs=[qspec] * len(qkv_cols) + cspecs,
        out_specs=cspecs + [pl.BlockSpec((GROUP_WIDTH, n), lambda i: (0, 0))],
        out_shape=[jax.ShapeDtypeStruct(c.shape, F32) for c in windows_t]
                  + [jax.ShapeDtypeStruct((GROUP_WIDTH, n), F32)],
        compiler_params=pltpu.CompilerParams(dimension_semantics=("arbitrary",),
                                             vmem_limit_bytes=_vmem_limit(pipelined, 0, 16 << 20)),
        name="cache_step",
    )(*qkv_cols, *windows_t)
    return res[:-1], res[-1]


def _rope_tables(pos):
    half = HEAD_DIM // 2
    inv = ROPE_THETA ** (-jnp.arange(half, dtype=F32) * (2.0 / HEAD_DIM))
    ang = pos.astype(F32)[:, None] * inv[None, :]
    cos, sin = jnp.cos(ang), jnp.sin(ang)
    reps = V7X_LANES // HEAD_DIM
    return (jnp.tile(jnp.concatenate([cos, cos], axis=1), (1, reps)),
            jnp.tile(jnp.concatenate([-sin, sin], axis=1), (1, reps)))


def _window_to_lanes(w):
    n, width = w.shape[:2]
    return jnp.transpose(w, (0, 2, 3, 4, 1)).reshape(n, 2 * GROUP_WIDTH, width)


def _window_from_lanes(wt):
    n, _, width = wt.shape
    return jnp.transpose(wt.reshape(n, 2, HEADS, HEAD_DIM, width), (0, 4, 1, 2, 3))[None]


def kernel(x_prompt, x_sample, state_pool, cache_kv_w128, cache_kv_w512, cache_kv_w2048, norm_ffn1, ffn1_w_gate, ffn1_w_up, ffn1_w_down, norm_mix, w_in, w_pool_grp, pool_scale, w_pool_br, w_attn_br, w_o, norm_ffn2, ffn2_w_gate, ffn2_w_up, ffn2_w_down, norm_final):
    b, s, d = x_prompt.shape
    n, dec_seq, _ = x_sample.shape
    caches = (cache_kv_w128, cache_kv_w512, cache_kv_w2048)
    assert norm_ffn1.shape[0] == 1 and dec_seq == 1, "single layer, single decode token"
    ts = PROMPT_ROW_TILE
    assert s % ts == 0 and ts >= POOL_CARRY
    for cache, (window, dil) in zip(caches, ATTN_GROUPS):
        assert window // dil == KEYS_PER_QUERY and s % (dil * KEYS_PER_QUERY) == 0 and ts % (16 * dil) == 0
        assert cache.shape[2] == window, "decode path expects full windows of history"

    row = lambda v: v.reshape(1, -1)
    bf = lambda w: w.astype(BF16)
    g1, g2, gm, gf = row(norm_ffn1[0]), row(norm_ffn2[0]), row(norm_mix[0]), row(norm_final)
    f1 = (bf(ffn1_w_gate[0]), bf(ffn1_w_up[0]), bf(ffn1_w_down[0]))
    pscale = row(pool_scale[0])

    cos_p, sin_p = _rope_tables(jnp.arange(s, dtype=jnp.int32))
    later = (ffn2_w_gate[0], ffn2_w_up[0], ffn2_w_down[0], w_in[0],
             w_pool_grp[0].reshape(-1, POOL_GROUP), w_pool_br[0], w_attn_br[0], w_o[0])
    x1, (f2g, f2u, f2d, win, wgrp, wpbr, wabr, wo) = _ffn_call(
        x_prompt.reshape(b * s, d), g1, *f1, tm=FFN_ROW_TILE, cast=later)
    f2 = (f2g, f2u, f2d)
    wgrp = wgrp.reshape(w_pool_grp.shape[1:])
    (q0, q1, q2, kv0, kv1, kv2, kvt0, kvt1, kvt2, tail, pp, ga) = _prompt_mix_in_call(
        x1.reshape(b, s, d), gm, win, cos_p, sin_p, wgrp, pscale, wpbr, ts)
    group_planes = []
    for q, kv, (_, dil) in zip((q0, q1, q2), (kv0, kv1, kv2), ATTN_GROUPS):
        group_planes += _stream_attn_call(q, kv, dil, s)
    y_prompt = _mix_out_call(x1, pp.reshape(b * s, d), ga.reshape(b * s, d), None, wabr, wo, g2, *f2, gf,
                             tm=ts, group_planes=tuple(group_planes)).reshape(b, s, d)
    pool_prompt = tail[None, :, POOL_CARRY - POOL_STATE:, :]

    cos_s, sin_s = _rope_tables(PAST_LEN + jnp.arange(dec_seq, dtype=jnp.int32))
    xs1, _ = _ffn_call(x_sample.reshape(n, d), g1, *f1, tm=n)
    state_t = jnp.swapaxes(state_pool[0], 0, 1)
    (sqkv0, sqkv1, sqkv2, u_new, spp, sga) = _sample_mix_in_call(
        xs1, gm, win, state_t, cos_s, sin_s, wgrp, pscale, wpbr)
    new_windows, attn_t = _cache_step_call((sqkv0, sqkv1, sqkv2),
                                           [_window_to_lanes(c[0]) for c in caches])
    y_sample = _mix_out_call(xs1, spp, sga, attn_t.T.astype(BF16), wabr, wo, g2, *f2, gf,
                             tm=n).reshape(n, dec_seq, d)
    pool_sample = jnp.swapaxes(jnp.concatenate([state_t[1:], u_new[None]], axis=0), 0, 1)[None]

    return (y_prompt, y_sample, pool_prompt,
            _window_from_lanes(kvt0), _window_from_lanes(kvt1), _window_from_lanes(kvt2),
            pool_sample, *[_window_from_lanes(w) for w in new_windows])
```

```python
import functools

import jax
import jax.numpy as jnp
from jax import lax
from jax.experimental import pallas as pl
from jax.experimental.pallas import tpu as pltpu

F32 = jnp.float32
BF16 = jnp.bfloat16

PAST_LEN = 16384
POOL_WINDOWS = (2, 4, 8, 16)
POOL_GROUP = 128
POOL_WIDTH = POOL_GROUP * len(POOL_WINDOWS)
POOL_STATE = max(POOL_WINDOWS) - 1
POOL_CARRY = 16
HEAD_DIM = 64
HEADS = 4
GROUP_WIDTH = HEADS * HEAD_DIM
ATTN_GROUPS = ((128, 1), (512, 4), (2048, 16))
KEYS_PER_QUERY = 128
QKV_GROUP_WIDTH = 3 * GROUP_WIDTH
ROPE_THETA = 10000.0
RMS_EPS = 1e-6
NEG_INF = -1e30
ATTN_SCALE = HEAD_DIM ** -0.5
LOG2_E = 1.4426950408889634

V7X_VMEM_BYTES = 64 * 1024 * 1024
V7X_LANES = 128
V7X_MXU_COLS = 256
MAX_FAST_SUBLANE_STRIDE = 4

PROMPT_ROW_TILE = 512
FFN_ROW_TILE = 1024
FFN_CHUNK = 4 * V7X_MXU_COLS


def _vmem_limit(pipelined_bytes, resident_bytes, temp_bytes):
    need = 2 * pipelined_bytes + resident_bytes + temp_bytes
    return int(min(need, V7X_VMEM_BYTES - (4 << 20)))


def _nbytes(shape, dtype):
    n = 1
    for s in shape:
        n *= s
    return n * jnp.dtype(dtype).itemsize


def _resident(shape):
    zeros = (0,) * len(shape)
    return pl.BlockSpec(shape, lambda *_: zeros, pipeline_mode=pl.Buffered(1))


def _rmsnorm(x, g):
    ms = jnp.mean(x * x, axis=-1, keepdims=True)
    return x * lax.rsqrt(ms + RMS_EPS) * g


def _ffn_chunks(width):
    return [(lo, min(lo + FFN_CHUNK, width)) for lo in range(0, width, FFN_CHUNK)]


def _swiglu(xn, wg_ref, wu_ref, wd_ref):
    acc = None
    for lo, hi in _ffn_chunks(wg_ref.shape[1]):
        gate = jnp.dot(xn, wg_ref[:, lo:hi], preferred_element_type=F32)
        up = jnp.dot(xn, wu_ref[:, lo:hi], preferred_element_type=F32)
        hid = (gate * jax.nn.sigmoid(gate) * up).astype(BF16)
        part = jnp.dot(hid, wd_ref[lo:hi, :], preferred_element_type=F32)
        acc = part if acc is None else acc + part
    return acc


def _macaron_ffn(x, g_ref, wg_ref, wu_ref, wd_ref):
    xn = _rmsnorm(x, g_ref[...]).astype(BF16)
    return x + 0.5 * _swiglu(xn, wg_ref, wu_ref, wd_ref)


def _ffn_kernel(x_ref, xs_ref, g_ref, wg_ref, wu_ref, wd_ref, *rest):
    n_cast = (len(rest) - 2) // 2
    o_ref, os_ref = rest[n_cast], rest[n_cast + 1]

    @pl.when(pl.program_id(0) == 0)
    def _():
        os_ref[...] = _macaron_ffn(xs_ref[...], g_ref, wg_ref, wu_ref, wd_ref)

    o_ref[...] = _macaron_ffn(x_ref[...], g_ref, wg_ref, wu_ref, wd_ref)
    for src_ref, dst_ref in zip(rest[:n_cast], rest[n_cast + 2:]):
        dst_ref[...] = src_ref[...].astype(BF16)


BF16_SUBLANE_ROWS = 16


def _ffn_call(x, xs, g, wg, wu, wd, tm, cast=()):
    m, d = x.shape
    f = wg.shape[1]
    steps = m // tm
    row = pl.BlockSpec((tm, d), lambda i: (i, 0))
    dec = pl.BlockSpec(xs.shape, lambda i: (0, 0))
    cast_specs, cast_bytes = [], 0
    for w in cast:
        rows = w.shape[0]
        per_step = next(r for r in range(BF16_SUBLANE_ROWS, rows + 1, BF16_SUBLANE_ROWS)
                        if rows % r == 0 and r * steps >= rows)
        last = rows // per_step - 1
        cast_specs.append(pl.BlockSpec((per_step, w.shape[1]), lambda i, last=last: (jnp.minimum(i, last), 0)))
        cast_bytes += _nbytes((per_step, w.shape[1]), F32) + _nbytes((per_step, w.shape[1]), BF16)
    limit = _vmem_limit(2 * _nbytes((tm, d), F32) + 2 * _nbytes(xs.shape, F32) + cast_bytes,
                        3 * _nbytes((d, f), BF16),
                        4 * _nbytes((tm, FFN_CHUNK), F32) + 2 * _nbytes((tm, d), F32))
    res = pl.pallas_call(
        _ffn_kernel,
        grid=(steps,),
        in_specs=[row, dec, _resident((1, d)), _resident((d, f)), _resident((d, f)), _resident((f, d))]
                 + cast_specs,
        out_specs=[row, dec] + cast_specs,
        out_shape=[jax.ShapeDtypeStruct((m, d), F32), jax.ShapeDtypeStruct(xs.shape, F32)]
                  + [jax.ShapeDtypeStruct(w.shape, BF16) for w in cast],
        compiler_params=pltpu.CompilerParams(dimension_semantics=("arbitrary",), vmem_limit_bytes=limit),
        name="ffn",
    )(x, xs, g, wg, wu, wd, *cast)
    return res[0], res[1], res[2:]


def _rope(t, cos, sin_signed, first_half):
    outs = []
    for c in range(t.shape[1] // V7X_LANES):
        tc = t[:, c * V7X_LANES:(c + 1) * V7X_LANES]
        partner = jnp.where(first_half,
                            pltpu.roll(tc, V7X_LANES - HEAD_DIM // 2, 1),
                            pltpu.roll(tc, HEAD_DIM // 2, 1))
        outs.append(tc * cos + partner * sin_signed)
    return jnp.concatenate(outs, axis=1)


def _first_half_mask():
    lane = lax.broadcasted_iota(jnp.int32, (1, V7X_LANES), 1)
    return (lane % HEAD_DIM) < (HEAD_DIM // 2)


def _qkv_group(h, win_ref, g, cos, sin_signed, first_half, q_scale):
    base = POOL_WIDTH + g * QKV_GROUP_WIDTH
    qkv = jnp.dot(h, win_ref[:, base:base + QKV_GROUP_WIDTH], preferred_element_type=F32)
    q = _rope(qkv[:, :GROUP_WIDTH], cos, sin_signed, first_half) * q_scale
    k = _rope(qkv[:, GROUP_WIDTH:2 * GROUP_WIDTH], cos, sin_signed, first_half)
    return jnp.concatenate([q, k, qkv[:, 2 * GROUP_WIDTH:]], axis=1)


def _pool_branch(window_means_minus_tok, wgrp_ref, pscale_ref, wpbr_ref):
    mixed = [jnp.dot(p.astype(BF16), wgrp_ref[gi], preferred_element_type=F32)
             for gi, p in enumerate(window_means_minus_tok)]
    pool = jnp.concatenate(mixed, axis=1) * pscale_ref[...]
    return jnp.dot(pool.astype(BF16), wpbr_ref[...], preferred_element_type=F32)


def _gates(h, win_ref):
    d = win_ref.shape[0]
    base = POOL_WIDTH + len(ATTN_GROUPS) * QKV_GROUP_WIDTH
    gate_pool = jax.nn.sigmoid(jnp.dot(h, win_ref[:, base:base + d], preferred_element_type=F32))
    gate_attn = jax.nn.sigmoid(jnp.dot(h, win_ref[:, base + d:base + 2 * d], preferred_element_type=F32))
    return gate_pool, gate_attn


def _prompt_mix_in_kernel(x_ref, g_ref, win_ref, cos_ref, sin_ref, wgrp_ref, pscale_ref, wpbr_ref,
                          q0_ref, q1_ref, q2_ref, kv0_ref, kv1_ref, kv2_ref,
                          kvt0_ref, kvt1_ref, kvt2_ref, tail_ref, pp_ref, ga_ref, uext_ref, qkv_ref, mid_ref,
                          *, ts):
    j = pl.program_id(1)

    @pl.when(j == 0)
    def _():
        uext_ref[0:POOL_CARRY, :] = jnp.zeros((POOL_CARRY, POOL_WIDTH), F32)

    h = _rmsnorm(x_ref[0], g_ref[...]).astype(BF16)

    cos = cos_ref[...]
    sin_signed = sin_ref[...]
    first_half = _first_half_mask()
    outs = ((q0_ref, kv0_ref, kvt0_ref), (q1_ref, kv1_ref, kvt1_ref), (q2_ref, kv2_ref, kvt2_ref))
    planes = QKV_GROUP_WIDTH // V7X_LANES
    slot = 0
    for g in sorted(range(len(ATTN_GROUPS)), key=lambda g: -ATTN_GROUPS[g][1]):
        q_ref, kv_ref, kvt_ref = outs[g]
        dil = ATTN_GROUPS[g][1]
        qkv = _qkv_group(h, win_ref, g, cos, sin_signed, first_half, ATTN_SCALE * LOG2_E)
        if dil == 1:
            q_ref[0] = qkv[:, :GROUP_WIDTH].astype(BF16)
            kv_ref[0] = qkv[:, GROUP_WIDTH:].astype(BF16)
        else:
            for c in range(planes):
                qkv_ref[slot + c] = qkv[:, c * V7X_LANES:(c + 1) * V7X_LANES]
            inner = min(dil, MAX_FAST_SUBLANE_STRIDE)
            outer = dil // inner
            assert outer <= MAX_FAST_SUBLANE_STRIDE and outer * inner == dil
            if outer > 1:
                for c in range(planes):
                    for bb in range(inner):
                        mid_ref[c, bb * (ts // inner):(bb + 1) * (ts // inner), :] = (
                            qkv_ref[slot + c, pl.ds(bb, ts // inner, stride=inner), :])
            for r in range(dil):
                if outer > 1:
                    a, bb = r // inner, r % inner
                    rows = jnp.concatenate(
                        [mid_ref[c, pl.ds(bb * (ts // inner) + a, ts // dil, stride=outer), :]
                         for c in range(planes)], axis=1)
                else:
                    rows = jnp.concatenate([qkv_ref[slot + c, pl.ds(r, ts // dil, stride=dil), :]
                                            for c in range(planes)], axis=1)
                q_ref[0, :, r * GROUP_WIDTH:(r + 1) * GROUP_WIDTH] = rows[:, :GROUP_WIDTH].astype(BF16)
                kv_ref[0, :, 2 * r * GROUP_WIDTH:2 * (r + 1) * GROUP_WIDTH] = rows[:, GROUP_WIDTH:].astype(BF16)
            slot += planes
        keep = kvt_ref.shape[2]
        kvt_ref[0] = qkv[ts - keep:, GROUP_WIDTH:].T

    u = jnp.dot(h, win_ref[:, :POOL_WIDTH], preferred_element_type=F32)
    uext_ref[POOL_CARRY:POOL_CARRY + ts, :] = u
    pos = j * ts + lax.broadcasted_iota(jnp.int32, (ts, 1), 0)
    pooled = []
    for gi, w in enumerate(POOL_WINDOWS):
        lo, hi = gi * POOL_GROUP, (gi + 1) * POOL_GROUP
        tok = u[:, lo:hi]
        total = tok
        for back in range(1, w):
            total = total + uext_ref[POOL_CARRY - back:POOL_CARRY - back + ts, lo:hi]
        cnt = jnp.minimum(pos + 1, w).astype(F32)
        pooled.append(total / cnt - tok)
    pool_br = _pool_branch(pooled, wgrp_ref, pscale_ref, wpbr_ref)
    uext_ref[0:POOL_CARRY, :] = u[ts - POOL_CARRY:, :]
    tail_ref[0] = u[ts - POOL_CARRY:, :]
    gate_pool, gate_attn = _gates(h, win_ref)
    pp_ref[0] = gate_pool * pool_br
    ga_ref[0] = gate_attn


def _prompt_mix_in_call(x, g, win, cos, sin, wgrp, pscale, wpbr, ts):
    b, s, d = x.shape
    seq_tiles = s // ts
    zw = win.shape[1]

    def tile(width):
        return pl.BlockSpec((1, ts, width), lambda bi, j: (bi, j, 0))

    q_specs, kv_specs, q_shapes, kv_shapes, kvt_specs, kvt_shapes = [], [], [], [], [], []
    for window, dil in ATTN_GROUPS:
        q_specs.append(pl.BlockSpec((1, ts // dil, dil * GROUP_WIDTH), lambda bi, j: (bi, j, 0)))
        kv_specs.append(pl.BlockSpec((1, ts // dil, 2 * dil * GROUP_WIDTH), lambda bi, j: (bi, j, 0)))
        q_shapes.append(jax.ShapeDtypeStruct((b, s // dil, dil * GROUP_WIDTH), BF16))
        kv_shapes.append(jax.ShapeDtypeStruct((b, s // dil, 2 * dil * GROUP_WIDTH), BF16))
        keep = min(window, s)
        if keep >= ts:
            first = seq_tiles - keep // ts
            kvt_specs.append(pl.BlockSpec((1, 2 * GROUP_WIDTH, ts),
                                          lambda bi, j, first=first: (bi, 0, jnp.maximum(j - first, 0))))
        else:
            kvt_specs.append(pl.BlockSpec((1, 2 * GROUP_WIDTH, keep), lambda bi, j: (bi, 0, 0)))
        kvt_shapes.append(jax.ShapeDtypeStruct((b, 2 * GROUP_WIDTH, keep), F32))
    n_g = len(ATTN_GROUPS)
    out_specs = (q_specs + kv_specs + kvt_specs
                 + [pl.BlockSpec((1, POOL_CARRY, POOL_WIDTH), lambda bi, j: (bi, 0, 0)), tile(d), tile(d)])
    out_shape = (q_shapes + kv_shapes + kvt_shapes
                 + [jax.ShapeDtypeStruct((b, POOL_CARRY, POOL_WIDTH), F32),
                    jax.ShapeDtypeStruct((b, s, d), F32), jax.ShapeDtypeStruct((b, s, d), F32)])
    table = pl.BlockSpec((ts, V7X_LANES), lambda bi, j: (j, 0))
    pipelined = (3 * _nbytes((ts, d), F32) + n_g * _nbytes((ts, QKV_GROUP_WIDTH), BF16)
                 + n_g * _nbytes((ts, 2 * GROUP_WIDTH), F32) + 2 * _nbytes((ts, V7X_LANES), F32))
    resident = _nbytes((d, zw), BF16) + _nbytes(wpbr.shape, BF16) + _nbytes(wgrp.shape, BF16)
    n_dilated = sum(1 for _, dil in ATTN_GROUPS if dil != 1)
    scratch = [pltpu.VMEM((ts + POOL_CARRY, POOL_WIDTH), F32),
               pltpu.VMEM((n_dilated * QKV_GROUP_WIDTH // V7X_LANES, ts, V7X_LANES), F32),
               pltpu.VMEM((QKV_GROUP_WIDTH // V7X_LANES, ts, V7X_LANES), F32)]
    temps = (_nbytes((ts + POOL_CARRY, POOL_WIDTH), F32) + (n_dilated + 1) * _nbytes((ts, QKV_GROUP_WIDTH), F32)
             + 6 * _nbytes((ts, d), F32))
    return pl.pallas_call(
        functools.partial(_prompt_mix_in_kernel, ts=ts),
        grid=(b, seq_tiles),
        in_specs=[tile(d), _resident((1, d)), _resident((d, zw)), table, table,
                  _resident(wgrp.shape), _resident((1, POOL_WIDTH)), _resident(wpbr.shape)],
        out_specs=out_specs,
        out_shape=out_shape,
        scratch_shapes=scratch,
        compiler_params=pltpu.CompilerParams(dimension_semantics=("arbitrary", "arbitrary"),
                                             vmem_limit_bytes=_vmem_limit(pipelined, resident, temps)),
        name="prompt_mix_in",
    )(x, g, win, cos, sin, wgrp, pscale, wpbr)


def _head_lane_masks():
    lane = lax.broadcasted_iota(jnp.int32, (1, GROUP_WIDTH), 1)
    return [(lane // HEAD_DIM) == hd for hd in range(HEADS)]


SOFTMAX_ROWS = 128


def _band_block(q, kt, v, bias_ref, head_ones, low_head):
    blk, keys = q.shape[0], v.shape[0]
    q_heads = jnp.concatenate([q * one for one in head_ones], axis=0)
    sc = jnp.dot(q_heads, kt, preferred_element_type=F32)
    outs, lses = [], []
    for hd in range(HEADS):
        es, mxs, dens = [], [], []
        for r0 in range(0, blk, SOFTMAX_ROWS):
            s_c = sc[hd * blk + r0:hd * blk + r0 + SOFTMAX_ROWS, :] + bias_ref[r0:r0 + SOFTMAX_ROWS, :keys]
            mx_c = jnp.max(s_c, axis=1, keepdims=True)
            e_c = jnp.exp2(s_c - mx_c)
            mxs.append(mx_c)
            dens.append(jnp.sum(e_c, axis=1, keepdims=True))
            es.append(e_c.astype(BF16))
        den = jnp.concatenate(dens, axis=0)
        tile = slice((hd * HEAD_DIM // V7X_LANES) * V7X_LANES, (hd * HEAD_DIM // V7X_LANES + 1) * V7X_LANES)
        o_un = jnp.dot(jnp.concatenate(es, axis=0), v[:, tile], preferred_element_type=F32)
        outs.append(o_un * (1.0 / den))
        lses.append(jnp.broadcast_to(jnp.concatenate(mxs, axis=0) + jnp.log2(den), (blk, V7X_LANES)))
    pairs = range(0, HEADS, V7X_LANES // HEAD_DIM)
    return ([jnp.where(low_head, outs[hd], outs[hd + 1]) for hd in pairs],
            [jnp.where(low_head, lses[hd], lses[hd + 1]) for hd in pairs])


ATTN_PLANES = GROUP_WIDTH // V7X_LANES


def _stream_attn_kernel(q_ref, kv_ref, o_ref, l_ref, first_bias_ref, band_bias_ref, *, dil, seq):
    blk = KEYS_PER_QUERY

    @pl.when(pl.program_id(0) == 0)
    def _():
        qi = lax.broadcasted_iota(jnp.int32, (blk, 1), 0)
        first = lax.broadcasted_iota(jnp.int32, (1, blk), 1) <= qi
        dist = qi + blk - lax.broadcasted_iota(jnp.int32, (1, 2 * blk), 1)
        first_bias_ref[...] = jnp.where(first, 0.0, NEG_INF)
        band_bias_ref[...] = jnp.where((dist >= 0) & (dist <= blk), 0.0, NEG_INF)

    head_ones = [jnp.where(m, 1.0, 0.0).astype(BF16) for m in _head_lane_masks()]
    low_head = lax.broadcasted_iota(jnp.int32, (1, V7X_LANES), 1) < HEAD_DIM
    for r in range(dil):
        qc = slice(r * GROUP_WIDTH, (r + 1) * GROUP_WIDTH)
        kc = slice(2 * r * GROUP_WIDTH, (2 * r + 1) * GROUP_WIDTH)
        vc = slice((2 * r + 1) * GROUP_WIDTH, (2 * r + 2) * GROUP_WIDTH)
        for i in range(seq // (dil * blk)):
            krows = slice(max(i - 1, 0) * blk, (i + 1) * blk)
            outs, lses = _band_block(q_ref[0, i * blk:(i + 1) * blk, qc], kv_ref[0, krows, kc].T, kv_ref[0, krows, vc],
                                     first_bias_ref if i == 0 else band_bias_ref, head_ones, low_head)
            for c in range(ATTN_PLANES):
                o_ref[0, c, r, i * blk:(i + 1) * blk, :] = outs[c]
                l_ref[0, c, r, i * blk:(i + 1) * blk, :] = lses[c]


def _merge_groups(outs, lses, exp_fn):
    mx = functools.reduce(jnp.maximum, lses)
    ws = [exp_fn(l - mx) for l in lses]
    num = functools.reduce(lambda a, b: a + b, [w * o for w, o in zip(ws, outs)])
    return num / functools.reduce(lambda a, b: a + b, ws)


def _stream_attn_call(q, kv, dil, seq):
    b = q.shape[0]

    def whole(a):
        zeros = (0,) * (len(a.shape) - 1)
        return pl.BlockSpec((1,) + tuple(a.shape[1:]), lambda bi: (bi,) + zeros)

    planes = jax.ShapeDtypeStruct((b, ATTN_PLANES, dil, seq // dil, V7X_LANES), F32)
    pipelined = _nbytes((seq, QKV_GROUP_WIDTH), BF16) + 2 * _nbytes((seq, GROUP_WIDTH), F32)
    scratch = [pltpu.VMEM((KEYS_PER_QUERY, KEYS_PER_QUERY), F32),
               pltpu.VMEM((KEYS_PER_QUERY, 2 * KEYS_PER_QUERY), F32)]
    scratch_bytes = 3 * _nbytes((KEYS_PER_QUERY, KEYS_PER_QUERY), F32)
    return pl.pallas_call(
        functools.partial(_stream_attn_kernel, dil=dil, seq=seq),
        grid=(b,),
        in_specs=[whole(q), whole(kv)],
        out_specs=[whole(planes), whole(planes)],
        out_shape=[planes, planes],
        scratch_shapes=scratch,
        compiler_params=pltpu.CompilerParams(dimension_semantics=("arbitrary",),
                                             vmem_limit_bytes=_vmem_limit(pipelined, scratch_bytes, 16 << 20)),
        name=f"stream_attn_d{dil}",
    )(q, kv)


def _token_order(ref, plane, stage_ref, slot):
    dil, rows = ref.shape[2], ref.shape[3]
    if dil == 1:
        return ref[0, plane, 0]
    if dil <= MAX_FAST_SUBLANE_STRIDE:
        for r in range(dil):
            stage_ref[slot, pl.ds(r, rows, stride=dil), :] = ref[0, plane, r]
        return stage_ref[slot]
    inner = MAX_FAST_SUBLANE_STRIDE
    outer = dil // inner
    assert outer <= MAX_FAST_SUBLANE_STRIDE and outer * inner == dil
    tmp = stage_ref.shape[0] - 1
    mid = rows * outer
    for b in range(inner):
        for a in range(outer):
            stage_ref[tmp, pl.ds(b * mid + a, rows, stride=outer), :] = ref[0, plane, inner * a + b]
    for b in range(inner):
        stage_ref[slot, pl.ds(b, mid, stride=inner), :] = stage_ref[tmp, b * mid:(b + 1) * mid, :]
    return stage_ref[slot]


def _mix_out_rows(x, pp, ga, attn, wabr_ref, wo_ref, g2_ref, wg_ref, wu_ref, wd_ref, gf_ref):
    attn_br = jnp.dot(attn, wabr_ref[...], preferred_element_type=F32)
    merged = pp + ga * attn_br
    x = x + jnp.dot(merged.astype(BF16), wo_ref[...], preferred_element_type=F32)
    x = _macaron_ffn(x, g2_ref, wg_ref, wu_ref, wd_ref)
    return _rmsnorm(x, gf_ref[...])


def _mix_out_kernel(x_ref, pp_ref, ga_ref, *rest, n_groups):
    group_refs, rest = rest[:2 * n_groups], rest[2 * n_groups:]
    weights, rest = rest[:7], rest[7:]
    xs_ref, spp_ref, sga_ref, sattn_ref, y_ref, ys_ref, stage_ref, attn_ref = rest

    @pl.when(pl.program_id(0) == 0)
    def _():
        attn_ref[...] = jnp.zeros(attn_ref.shape, BF16)
        ys_ref[...] = _mix_out_rows(xs_ref[...], spp_ref[...], sga_ref[...], sattn_ref[...], *weights)

    y_ref[...] = _mix_out_rows(x_ref[...], pp_ref[...], ga_ref[...], attn_ref[...], *weights)
    for c in range(ATTN_PLANES):
        outs = [_token_order(group_refs[2 * g], c, stage_ref, 2 * g) for g in range(n_groups)]
        lses = [_token_order(group_refs[2 * g + 1], c, stage_ref, 2 * g + 1) for g in range(n_groups)]
        attn_ref[:, c * V7X_LANES:(c + 1) * V7X_LANES] = _merge_groups(outs, lses, jnp.exp2).astype(BF16)


def _mix_out_call(x, pp, ga, group_planes, decode, wabr, wo, g2, wg, wu, wd, gf, tm):
    m, d = x.shape
    f = wg.shape[1]
    n_tiles = m // tm
    row = pl.BlockSpec((tm, d), lambda i: (jnp.maximum(i - 1, 0), 0))
    plane_specs = []
    for p in group_planes:
        dil, seq_rows = p.shape[2], p.shape[3]
        tiles = seq_rows * dil // tm
        plane_specs.append(pl.BlockSpec(
            (1, ATTN_PLANES, dil, tm // dil, V7X_LANES),
            lambda i, tiles=tiles: (jnp.minimum(i, n_tiles - 1) // tiles, 0, 0,
                                    jnp.minimum(i, n_tiles - 1) % tiles, 0)))

    def whole(a):
        return pl.BlockSpec(a.shape, lambda i: (0, 0))

    n_dec = decode[0].shape[0]
    pipelined = (4 * _nbytes((tm, d), F32) + len(group_planes) * _nbytes((tm, GROUP_WIDTH), F32)
                 + 4 * _nbytes((n_dec, d), F32))
    scratch = [pltpu.VMEM((len(group_planes) + 1, tm, V7X_LANES), F32), pltpu.VMEM((tm, GROUP_WIDTH), BF16)]
    resident = 3 * _nbytes((d, f), BF16) + _nbytes((d, d), BF16) + _nbytes((GROUP_WIDTH, d), BF16)
    temps = 4 * _nbytes((tm, FFN_CHUNK), F32) + 4 * _nbytes((tm, d), F32)
    return pl.pallas_call(
        functools.partial(_mix_out_kernel, n_groups=len(group_planes) // 2),
        grid=(n_tiles + 1,),
        in_specs=[row, row, row] + plane_specs
                 + [_resident((GROUP_WIDTH, d)), _resident((d, d)), _resident((1, d)),
                    _resident((d, f)), _resident((d, f)), _resident((f, d)), _resident((1, d))]
                 + [whole(a) for a in decode],
        out_specs=[row, pl.BlockSpec((n_dec, d), lambda i: (0, 0))],
        out_shape=[jax.ShapeDtypeStruct((m, d), F32), jax.ShapeDtypeStruct((n_dec, d), F32)],
        scratch_shapes=scratch,
        compiler_params=pltpu.CompilerParams(dimension_semantics=("arbitrary",),
                                             vmem_limit_bytes=_vmem_limit(pipelined, resident, temps)),
        name="mix_out",
    )(x, pp, ga, *group_planes, wabr, wo, g2, wg, wu, wd, gf, *decode)


def _sample_mix_in_kernel(x_ref, g_ref, win_ref, state_ref, cos_ref, sin_ref, wgrp_ref, pscale_ref, wpbr_ref,
                          qkv0_ref, qkv1_ref, qkv2_ref, u_ref, pp_ref, ga_ref):
    h = _rmsnorm(x_ref[...], g_ref[...]).astype(BF16)
    u = jnp.dot(h, win_ref[:, :POOL_WIDTH], preferred_element_type=F32)
    u_ref[...] = u
    pooled = []
    for gi, w in enumerate(POOL_WINDOWS):
        lo, hi = gi * POOL_GROUP, (gi + 1) * POOL_GROUP
        tok = u[:, lo:hi]
        total = tok
        for back in range(1, w):
            total = total + state_ref[POOL_STATE - back, :, lo:hi]
        pooled.append(total / float(min(PAST_LEN + 1, w)) - tok)
    pool_br = _pool_branch(pooled, wgrp_ref, pscale_ref, wpbr_ref)
    gate_pool, gate_attn = _gates(h, win_ref)
    pp_ref[...] = gate_pool * pool_br
    ga_ref[...] = gate_attn
    first_half = _first_half_mask()
    n = x_ref.shape[0]
    for g, qkv_ref in enumerate((qkv0_ref, qkv1_ref, qkv2_ref)):
        qkv = _qkv_group(h, win_ref, g, cos_ref[...], sin_ref[...], first_half, ATTN_SCALE)
        padded = jnp.concatenate([qkv, jnp.zeros((V7X_LANES - n, QKV_GROUP_WIDTH), F32)], axis=0)
        qkv_ref[...] = jnp.transpose(padded)[:, :n]


def _sample_mix_in_call(x, g, win, state_t, cos, sin, wgrp, pscale, wpbr):
    n, d = x.shape
    n_g = len(ATTN_GROUPS)

    def full(shape):
        zeros = (0,) * len(shape)
        return pl.BlockSpec(shape, lambda i: zeros)

    assert n <= V7X_LANES
    out_shape = ([jax.ShapeDtypeStruct((QKV_GROUP_WIDTH, n), F32)] * n_g
                 + [jax.ShapeDtypeStruct((n, POOL_WIDTH), F32),
                    jax.ShapeDtypeStruct((n, d), F32), jax.ShapeDtypeStruct((n, d), F32)])
    ins = (x, g, win, state_t, cos, sin, wgrp, pscale, wpbr)
    resident = sum(_nbytes(a.shape, a.dtype) for a in ins)
    return pl.pallas_call(
        _sample_mix_in_kernel,
        grid=(1,),
        in_specs=[full(a.shape) for a in ins],
        out_specs=[full(o.shape) for o in out_shape],
        out_shape=out_shape,
        compiler_params=pltpu.CompilerParams(dimension_semantics=("arbitrary",),
                                             vmem_limit_bytes=_vmem_limit(resident, 0, 8 << 20)),
        name="sample_mix_in",
    )(*ins)


def _head_sums(x):
    return jnp.concatenate([jnp.sum(x[hd * HEAD_DIM:(hd + 1) * HEAD_DIM], axis=0, keepdims=True)
                            for hd in range(HEADS)], axis=0)


def _head_expand(y):
    return jnp.concatenate([jnp.broadcast_to(y[hd:hd + 1], (HEAD_DIM, y.shape[1])) for hd in range(HEADS)],
                           axis=0)


def _cache_step_kernel(qkv0_ref, qkv1_ref, qkv2_ref, c0_ref, c1_ref, c2_ref,
                       n0_ref, n1_ref, n2_ref, attn_ref):
    n = pl.program_id(0)
    outs, lses = [], []
    groups = ((qkv0_ref, c0_ref, n0_ref), (qkv1_ref, c1_ref, n1_ref), (qkv2_ref, c2_ref, n2_ref))
    request = lax.broadcasted_iota(jnp.int32, qkv0_ref.shape, 1) == n
    for (qkv_ref, c_ref, new_ref), (window, dil) in zip(groups, ATTN_GROUPS):
        qkv_col = jnp.sum(jnp.where(request, qkv_ref[...], 0.0), axis=1, keepdims=True)
        q = qkv_col[:GROUP_WIDTH]
        k_new = qkv_col[GROUP_WIDTH:2 * GROUP_WIDTH]
        v_new = qkv_col[2 * GROUP_WIDTH:]
        kt = c_ref[0, :GROUP_WIDTH, :]
        vt = c_ref[0, GROUP_WIDTH:, :]
        lane = lax.broadcasted_iota(jnp.int32, (1, window), 1)
        sc = jnp.where(lane % dil == 0, _head_sums(kt * q), NEG_INF)
        sc_new = _head_sums(k_new * q)
        mx = jnp.maximum(jnp.max(sc, axis=1, keepdims=True), sc_new)
        e = jnp.exp(sc - mx)
        e_new = jnp.exp(sc_new - mx)
        den = jnp.sum(e, axis=1, keepdims=True) + e_new
        o_un = jnp.sum(vt * _head_expand(e), axis=1, keepdims=True) + _head_expand(e_new) * v_new
        outs.append(o_un / _head_expand(den))
        lses.append(_head_expand(mx + jnp.log(den)))
        shifted = pltpu.roll(c_ref[0], window - 1, 1)
        new_ref[0] = jnp.where(lane == window - 1, qkv_col[GROUP_WIDTH:], shifted)
    attn_col = _merge_groups(outs, lses, jnp.exp)
    col = lax.broadcasted_iota(jnp.int32, attn_ref.shape, 1)

    @pl.when(n == 0)
    def _():
        attn_ref[...] = jnp.zeros(attn_ref.shape, F32)

    attn_ref[...] = jnp.where(col == n, attn_col, attn_ref[...])


def _cache_step_call(qkv_cols, windows_t):
    n = qkv_cols[0].shape[1]
    qspec = pl.BlockSpec((QKV_GROUP_WIDTH, n), lambda i: (0, 0))
    cspecs = [pl.BlockSpec((1,) + c.shape[1:], lambda i: (i, 0, 0)) for c in windows_t]
    pipelined = 2 * sum(_nbytes(c.shape[1:], F32) for c in windows_t) + 3 * _nbytes((QKV_GROUP_WIDTH, V7X_LANES), F32)
    res = pl.pallas_call(
        _cache_step_kernel,
        grid=(n,),
        in_specs=[qspec] * len(qkv_cols) + cspecs,
        out_specs=cspecs + [pl.BlockSpec((GROUP_WIDTH, n), lambda i: (0, 0))],
        out_shape=[jax.ShapeDtypeStruct(c.shape, F32) for c in windows_t]
                  + [jax.ShapeDtypeStruct((GROUP_WIDTH, n), F32)],
        compiler_params=pltpu.CompilerParams(dimension_semantics=("arbitrary",),
                                             vmem_limit_bytes=_vmem_limit(pipelined, 0, 16 << 20)),
        name="cache_step",
    )(*qkv_cols, *windows_t)
    return res[:-1], res[-1]


def _rope_tables(pos):
    half = HEAD_DIM // 2
    inv = ROPE_THETA ** (-jnp.arange(half, dtype=F32) * (2.0 / HEAD_DIM))
    ang = pos.astype(F32)[:, None] * inv[None, :]
    cos, sin = jnp.cos(ang), jnp.sin(ang)
    reps = V7X_LANES // HEAD_DIM
    return (jnp.tile(jnp.concatenate([cos, cos], axis=1), (1, reps)),
            jnp.tile(jnp.concatenate([-sin, sin], axis=1), (1, reps)))


def _window_to_lanes(w):
    n, width = w.shape[:2]
    return jnp.transpose(w, (0, 2, 3, 4, 1)).reshape(n, 2 * GROUP_WIDTH, width)


def _window_from_lanes(wt):
    n, _, width = wt.shape
    return jnp.transpose(wt.reshape(n, 2, HEADS, HEAD_DIM, width), (0, 4, 1, 2, 3))[None]


def kernel(x_prompt, x_sample, state_pool, cache_kv_w128, cache_kv_w512, cache_kv_w2048, norm_ffn1, ffn1_w_gate, ffn1_w_up, ffn1_w_down, norm_mix, w_in, w_pool_grp, pool_scale, w_pool_br, w_attn_br, w_o, norm_ffn2, ffn2_w_gate, ffn2_w_up, ffn2_w_down, norm_final):
    b, s, d = x_prompt.shape
    n, dec_seq, _ = x_sample.shape
    caches = (cache_kv_w128, cache_kv_w512, cache_kv_w2048)
    assert norm_ffn1.shape[0] == 1 and dec_seq == 1, "single layer, single decode token"
    ts = PROMPT_ROW_TILE
    assert s % ts == 0 and ts >= POOL_CARRY
    for cache, (window, dil) in zip(caches, ATTN_GROUPS):
        assert window // dil == KEYS_PER_QUERY and s % (dil * KEYS_PER_QUERY) == 0 and ts % (16 * dil) == 0
        assert cache.shape[2] == window, "decode path expects full windows of history"

    row = lambda v: v.reshape(1, -1)
    bf = lambda w: w.astype(BF16)
    g1, g2, gm, gf = row(norm_ffn1[0]), row(norm_ffn2[0]), row(norm_mix[0]), row(norm_final)
    f1 = (bf(ffn1_w_gate[0]), bf(ffn1_w_up[0]), bf(ffn1_w_down[0]))
    pscale = row(pool_scale[0])

    cos_p, sin_p = _rope_tables(jnp.arange(s, dtype=jnp.int32))
    later = (ffn2_w_gate[0], ffn2_w_up[0], ffn2_w_down[0], w_in[0],
             w_pool_grp[0].reshape(-1, POOL_GROUP), w_pool_br[0], w_attn_br[0], w_o[0])
    x1, xs1, (f2g, f2u, f2d, win, wgrp, wpbr, wabr, wo) = _ffn_call(
        x_prompt.reshape(b * s, d), x_sample.reshape(n, d), g1, *f1, tm=FFN_ROW_TILE, cast=later)
    f2 = (f2g, f2u, f2d)
    wgrp = wgrp.reshape(w_pool_grp.shape[1:])
    (q0, q1, q2, kv0, kv1, kv2, kvt0, kvt1, kvt2, tail, pp, ga) = _prompt_mix_in_call(
        x1.reshape(b, s, d), gm, win, cos_p, sin_p, wgrp, pscale, wpbr, ts)
    group_planes = []
    for q, kv, (_, dil) in zip((q0, q1, q2), (kv0, kv1, kv2), ATTN_GROUPS):
        group_planes += _stream_attn_call(q, kv, dil, s)
    pool_prompt = tail[None, :, POOL_CARRY - POOL_STATE:, :]

    cos_s, sin_s = _rope_tables(PAST_LEN + jnp.arange(dec_seq, dtype=jnp.int32))
    state_t = jnp.swapaxes(state_pool[0], 0, 1)
    (sqkv0, sqkv1, sqkv2, u_new, spp, sga) = _sample_mix_in_call(
        xs1, gm, win, state_t, cos_s, sin_s, wgrp, pscale, wpbr)
    new_windows, attn_t = _cache_step_call((sqkv0, sqkv1, sqkv2),
                                           [_window_to_lanes(c[0]) for c in caches])
    pool_sample = jnp.swapaxes(jnp.concatenate([state_t[1:], u_new[None]], axis=0), 0, 1)[None]

    y_prompt, y_sample = _mix_out_call(x1, pp.reshape(b * s, d), ga.reshape(b * s, d), tuple(group_planes),
                                       (xs1, spp, sga, attn_t.T.astype(BF16)), wabr, wo, g2, *f2, gf, tm=ts)
    y_prompt = y_prompt.reshape(b, s, d)
    y_sample = y_sample.reshape(n, dec_seq, d)

    return (y_prompt, y_sample, pool_prompt,
            _window_from_lanes(kvt0), _window_from_lanes(kvt1), _window_from_lanes(kvt2),
            pool_sample, *[_window_from_lanes(w) for w in new_windows])
```

```python
import functools

import jax
import jax.numpy as jnp
from jax import lax
from jax.experimental import pallas as pl
from jax.experimental.pallas import tpu as pltpu

F32 = jnp.float32
BF16 = jnp.bfloat16

PAST_LEN = 16384
POOL_WINDOWS = (2, 4, 8, 16)
POOL_GROUP = 128
POOL_WIDTH = POOL_GROUP * len(POOL_WINDOWS)
POOL_STATE = max(POOL_WINDOWS) - 1
POOL_CARRY = 16
HEAD_DIM = 64
HEADS = 4
GROUP_WIDTH = HEADS * HEAD_DIM
ATTN_GROUPS = ((128, 1), (512, 4), (2048, 16))
KEYS_PER_QUERY = 128
QKV_GROUP_WIDTH = 3 * GROUP_WIDTH
ROPE_THETA = 10000.0
RMS_EPS = 1e-6
NEG_INF = -1e30
ATTN_SCALE = HEAD_DIM ** -0.5
LOG2_E = 1.4426950408889634

V7X_VMEM_BYTES = 64 * 1024 * 1024
V7X_LANES = 128
V7X_MXU_COLS = 256
MAX_FAST_SUBLANE_STRIDE = 4

PROMPT_ROW_TILE = 512
FFN_ROW_TILE = 1024
FFN_CHUNK = 4 * V7X_MXU_COLS


def _vmem_limit(pipelined_bytes, resident_bytes, temp_bytes):
    need = 2 * pipelined_bytes + resident_bytes + temp_bytes
    return int(min(need, V7X_VMEM_BYTES - (4 << 20)))


def _nbytes(shape, dtype):
    n = 1
    for s in shape:
        n *= s
    return n * jnp.dtype(dtype).itemsize


def _resident(shape):
    zeros = (0,) * len(shape)
    return pl.BlockSpec(shape, lambda *_: zeros, pipeline_mode=pl.Buffered(1))


def _rmsnorm(x, g):
    ms = jnp.mean(x * x, axis=-1, keepdims=True)
    return x * lax.rsqrt(ms + RMS_EPS) * g


def _ffn_chunks(width):
    return [(lo, min(lo + FFN_CHUNK, width)) for lo in range(0, width, FFN_CHUNK)]


def _swiglu(xn, wg_ref, wu_ref, wd_ref):
    acc = None
    for lo, hi in _ffn_chunks(wg_ref.shape[1]):
        gate = jnp.dot(xn, wg_ref[:, lo:hi], preferred_element_type=F32)
        up = jnp.dot(xn, wu_ref[:, lo:hi], preferred_element_type=F32)
        hid = (gate * jax.nn.sigmoid(gate) * up).astype(BF16)
        part = jnp.dot(hid, wd_ref[lo:hi, :], preferred_element_type=F32)
        acc = part if acc is None else acc + part
    return acc


def _macaron_ffn(x, g_ref, wg_ref, wu_ref, wd_ref):
    xn = _rmsnorm(x, g_ref[...]).astype(BF16)
    return x + 0.5 * _swiglu(xn, wg_ref, wu_ref, wd_ref)


def _ffn_kernel(x_ref, xs_ref, g_ref, wg_ref, wu_ref, wd_ref, *rest):
    n_cast = (len(rest) - 2) // 2
    o_ref, os_ref = rest[n_cast], rest[n_cast + 1]

    @pl.when(pl.program_id(0) == 0)
    def _():
        os_ref[...] = _macaron_ffn(xs_ref[...], g_ref, wg_ref, wu_ref, wd_ref)

    o_ref[...] = _macaron_ffn(x_ref[...], g_ref, wg_ref, wu_ref, wd_ref)
    for src_ref, dst_ref in zip(rest[:n_cast], rest[n_cast + 2:]):
        dst_ref[...] = src_ref[...].astype(BF16)


BF16_SUBLANE_ROWS = 16


def _ffn_call(x, xs, g, wg, wu, wd, tm, cast=()):
    m, d = x.shape
    f = wg.shape[1]
    steps = m // tm
    row = pl.BlockSpec((tm, d), lambda i: (i, 0))
    dec = pl.BlockSpec(xs.shape, lambda i: (0, 0))
    cast_specs, cast_bytes = [], 0
    for w in cast:
        rows = w.shape[0]
        per_step = next(r for r in range(BF16_SUBLANE_ROWS, rows + 1, BF16_SUBLANE_ROWS)
                        if rows % r == 0 and r * steps >= rows)
        last = rows // per_step - 1
        cast_specs.append(pl.BlockSpec((per_step, w.shape[1]), lambda i, last=last: (jnp.minimum(i, last), 0)))
        cast_bytes += _nbytes((per_step, w.shape[1]), F32) + _nbytes((per_step, w.shape[1]), BF16)
    limit = _vmem_limit(2 * _nbytes((tm, d), F32) + 2 * _nbytes(xs.shape, F32) + cast_bytes,
                        3 * _nbytes((d, f), BF16),
                        4 * _nbytes((tm, FFN_CHUNK), F32) + 2 * _nbytes((tm, d), F32))
    res = pl.pallas_call(
        _ffn_kernel,
        grid=(steps,),
        in_specs=[row, dec, _resident((1, d)), _resident((d, f)), _resident((d, f)), _resident((f, d))]
                 + cast_specs,
        out_specs=[row, dec] + cast_specs,
        out_shape=[jax.ShapeDtypeStruct((m, d), F32), jax.ShapeDtypeStruct(xs.shape, F32)]
                  + [jax.ShapeDtypeStruct(w.shape, BF16) for w in cast],
        compiler_params=pltpu.CompilerParams(dimension_semantics=("arbitrary",), vmem_limit_bytes=limit),
        name="ffn",
    )(x, xs, g, wg, wu, wd, *cast)
    return res[0], res[1], res[2:]


def _rope(t, cos, sin_signed, first_half):
    outs = []
    for c in range(t.shape[1] // V7X_LANES):
        tc = t[:, c * V7X_LANES:(c + 1) * V7X_LANES]
        partner = jnp.where(first_half,
                            pltpu.roll(tc, V7X_LANES - HEAD_DIM // 2, 1),
                            pltpu.roll(tc, HEAD_DIM // 2, 1))
        outs.append(tc * cos + partner * sin_signed)
    return jnp.concatenate(outs, axis=1)


def _first_half_mask():
    lane = lax.broadcasted_iota(jnp.int32, (1, V7X_LANES), 1)
    return (lane % HEAD_DIM) < (HEAD_DIM // 2)


def _qkv_group(h, win_ref, g, cos, sin_signed, first_half, q_scale):
    base = POOL_WIDTH + g * QKV_GROUP_WIDTH
    qkv = jnp.dot(h, win_ref[:, base:base + QKV_GROUP_WIDTH], preferred_element_type=F32)
    q = _rope(qkv[:, :GROUP_WIDTH], cos, sin_signed, first_half) * q_scale
    k = _rope(qkv[:, GROUP_WIDTH:2 * GROUP_WIDTH], cos, sin_signed, first_half)
    return jnp.concatenate([q, k, qkv[:, 2 * GROUP_WIDTH:]], axis=1)


def _pool_branch(window_means_minus_tok, wgrp_ref, pscale_ref, wpbr_ref):
    mixed = [jnp.dot(p.astype(BF16), wgrp_ref[gi], preferred_element_type=F32)
             for gi, p in enumerate(window_means_minus_tok)]
    pool = jnp.concatenate(mixed, axis=1) * pscale_ref[...]
    return jnp.dot(pool.astype(BF16), wpbr_ref[...], preferred_element_type=F32)


def _gates(h, win_ref):
    d = win_ref.shape[0]
    base = POOL_WIDTH + len(ATTN_GROUPS) * QKV_GROUP_WIDTH
    gate_pool = jax.nn.sigmoid(jnp.dot(h, win_ref[:, base:base + d], preferred_element_type=F32))
    gate_attn = jax.nn.sigmoid(jnp.dot(h, win_ref[:, base + d:base + 2 * d], preferred_element_type=F32))
    return gate_pool, gate_attn


def _prompt_mix_in_kernel(x_ref, g_ref, win_ref, cos_ref, sin_ref, wgrp_ref, pscale_ref, wpbr_ref,
                          s0_ref, s1_ref, s2_ref,
                          kvt0_ref, kvt1_ref, kvt2_ref, tail_ref, pp_ref, ga_ref, uext_ref, qkv_ref, mid_ref,
                          *, ts):
    j = pl.program_id(1)

    @pl.when(j == 0)
    def _():
        uext_ref[0:POOL_CARRY, :] = jnp.zeros((POOL_CARRY, POOL_WIDTH), F32)

    h = _rmsnorm(x_ref[0], g_ref[...]).astype(BF16)

    cos = cos_ref[...]
    sin_signed = sin_ref[...]
    first_half = _first_half_mask()
    outs = ((s0_ref, kvt0_ref), (s1_ref, kvt1_ref), (s2_ref, kvt2_ref))
    planes = QKV_GROUP_WIDTH // V7X_LANES
    slot = 0
    for g in sorted(range(len(ATTN_GROUPS)), key=lambda g: -ATTN_GROUPS[g][1]):
        stream_ref, kvt_ref = outs[g]
        dil = ATTN_GROUPS[g][1]
        qkv = _qkv_group(h, win_ref, g, cos, sin_signed, first_half, ATTN_SCALE * LOG2_E)
        if dil == 1:
            stream_ref[0] = qkv.astype(BF16)
        else:
            for c in range(planes):
                qkv_ref[slot + c] = qkv[:, c * V7X_LANES:(c + 1) * V7X_LANES]
            inner = min(dil, MAX_FAST_SUBLANE_STRIDE)
            outer = dil // inner
            assert outer <= MAX_FAST_SUBLANE_STRIDE and outer * inner == dil
            if outer > 1:
                for c in range(planes):
                    for bb in range(inner):
                        mid_ref[c, bb * (ts // inner):(bb + 1) * (ts // inner), :] = (
                            qkv_ref[slot + c, pl.ds(bb, ts // inner, stride=inner), :])
            for r in range(dil):
                if outer > 1:
                    a, bb = r // inner, r % inner
                    rows = jnp.concatenate(
                        [mid_ref[c, pl.ds(bb * (ts // inner) + a, ts // dil, stride=outer), :]
                         for c in range(planes)], axis=1)
                else:
                    rows = jnp.concatenate([qkv_ref[slot + c, pl.ds(r, ts // dil, stride=dil), :]
                                            for c in range(planes)], axis=1)
                stream_ref[0, :, r * QKV_GROUP_WIDTH:(r + 1) * QKV_GROUP_WIDTH] = rows.astype(BF16)
            slot += planes
        keep = kvt_ref.shape[2]
        kvt_ref[0] = qkv[ts - keep:, GROUP_WIDTH:].T

    u = jnp.dot(h, win_ref[:, :POOL_WIDTH], preferred_element_type=F32)
    uext_ref[POOL_CARRY:POOL_CARRY + ts, :] = u
    pos = j * ts + lax.broadcasted_iota(jnp.int32, (ts, 1), 0)
    pooled = []
    for gi, w in enumerate(POOL_WINDOWS):
        lo, hi = gi * POOL_GROUP, (gi + 1) * POOL_GROUP
        tok = u[:, lo:hi]
        total = tok
        for back in range(1, w):
            total = total + uext_ref[POOL_CARRY - back:POOL_CARRY - back + ts, lo:hi]
        cnt = jnp.minimum(pos + 1, w).astype(F32)
        pooled.append(total / cnt - tok)
    pool_br = _pool_branch(pooled, wgrp_ref, pscale_ref, wpbr_ref)
    uext_ref[0:POOL_CARRY, :] = u[ts - POOL_CARRY:, :]
    tail_ref[0] = u[ts - POOL_CARRY:, :]
    gate_pool, gate_attn = _gates(h, win_ref)
    pp_ref[0] = gate_pool * pool_br
    ga_ref[0] = gate_attn


def _prompt_mix_in_call(x, g, win, cos, sin, wgrp, pscale, wpbr, ts):
    b, s, d = x.shape
    seq_tiles = s // ts
    zw = win.shape[1]

    def tile(width):
        return pl.BlockSpec((1, ts, width), lambda bi, j: (bi, j, 0))

    stream_specs, stream_shapes, kvt_specs, kvt_shapes = [], [], [], []
    for window, dil in ATTN_GROUPS:
        stream_specs.append(pl.BlockSpec((1, ts // dil, dil * QKV_GROUP_WIDTH), lambda bi, j: (bi, j, 0)))
        stream_shapes.append(jax.ShapeDtypeStruct((b, s // dil, dil * QKV_GROUP_WIDTH), BF16))
        keep = min(window, s)
        if keep >= ts:
            first = seq_tiles - keep // ts
            kvt_specs.append(pl.BlockSpec((1, 2 * GROUP_WIDTH, ts),
                                          lambda bi, j, first=first: (bi, 0, jnp.maximum(j - first, 0))))
        else:
            kvt_specs.append(pl.BlockSpec((1, 2 * GROUP_WIDTH, keep), lambda bi, j: (bi, 0, 0)))
        kvt_shapes.append(jax.ShapeDtypeStruct((b, 2 * GROUP_WIDTH, keep), F32))
    n_g = len(ATTN_GROUPS)
    out_specs = (stream_specs + kvt_specs
                 + [pl.BlockSpec((1, POOL_CARRY, POOL_WIDTH), lambda bi, j: (bi, 0, 0)), tile(d), tile(d)])
    out_shape = (stream_shapes + kvt_shapes
                 + [jax.ShapeDtypeStruct((b, POOL_CARRY, POOL_WIDTH), F32),
                    jax.ShapeDtypeStruct((b, s, d), F32), jax.ShapeDtypeStruct((b, s, d), F32)])
    table = pl.BlockSpec((ts, V7X_LANES), lambda bi, j: (j, 0))
    pipelined = (3 * _nbytes((ts, d), F32) + n_g * _nbytes((ts, QKV_GROUP_WIDTH), BF16)
                 + n_g * _nbytes((ts, 2 * GROUP_WIDTH), F32) + 2 * _nbytes((ts, V7X_LANES), F32))
    resident = _nbytes((d, zw), BF16) + _nbytes(wpbr.shape, BF16) + _nbytes(wgrp.shape, BF16)
    n_dilated = sum(1 for _, dil in ATTN_GROUPS if dil != 1)
    scratch = [pltpu.VMEM((ts + POOL_CARRY, POOL_WIDTH), F32),
               pltpu.VMEM((n_dilated * QKV_GROUP_WIDTH // V7X_LANES, ts, V7X_LANES), F32),
               pltpu.VMEM((QKV_GROUP_WIDTH // V7X_LANES, ts, V7X_LANES), F32)]
    temps = (_nbytes((ts + POOL_CARRY, POOL_WIDTH), F32) + (n_dilated + 1) * _nbytes((ts, QKV_GROUP_WIDTH), F32)
             + 6 * _nbytes((ts, d), F32))
    return pl.pallas_call(
        functools.partial(_prompt_mix_in_kernel, ts=ts),
        grid=(b, seq_tiles),
        in_specs=[tile(d), _resident((1, d)), _resident((d, zw)), table, table,
                  _resident(wgrp.shape), _resident((1, POOL_WIDTH)), _resident(wpbr.shape)],
        out_specs=out_specs,
        out_shape=out_shape,
        scratch_shapes=scratch,
        compiler_params=pltpu.CompilerParams(dimension_semantics=("arbitrary", "arbitrary"),
                                             vmem_limit_bytes=_vmem_limit(pipelined, resident, temps)),
        name="prompt_mix_in",
    )(x, g, win, cos, sin, wgrp, pscale, wpbr)


def _head_lane_masks():
    lane = lax.broadcasted_iota(jnp.int32, (1, GROUP_WIDTH), 1)
    return [(lane // HEAD_DIM) == hd for hd in range(HEADS)]


SOFTMAX_ROWS = 128


def _band_block(q, kt, v, bias_ref, head_ones, low_head):
    blk, keys = q.shape[0], v.shape[0]
    q_heads = jnp.concatenate([q * one for one in head_ones], axis=0)
    sc = jnp.dot(q_heads, kt, preferred_element_type=F32)
    outs, lses = [], []
    for hd in range(HEADS):
        es, mxs, dens = [], [], []
        for r0 in range(0, blk, SOFTMAX_ROWS):
            s_c = sc[hd * blk + r0:hd * blk + r0 + SOFTMAX_ROWS, :] + bias_ref[r0:r0 + SOFTMAX_ROWS, :keys]
            mx_c = jnp.max(s_c, axis=1, keepdims=True)
            e_c = jnp.exp2(s_c - mx_c)
            mxs.append(mx_c)
            dens.append(jnp.sum(e_c, axis=1, keepdims=True))
            es.append(e_c.astype(BF16))
        den = jnp.concatenate(dens, axis=0)
        tile = slice((hd * HEAD_DIM // V7X_LANES) * V7X_LANES, (hd * HEAD_DIM // V7X_LANES + 1) * V7X_LANES)
        o_un = jnp.dot(jnp.concatenate(es, axis=0), v[:, tile], preferred_element_type=F32)
        outs.append(o_un * (1.0 / den))
        lses.append(jnp.broadcast_to(jnp.concatenate(mxs, axis=0) + jnp.log2(den), (blk, V7X_LANES)))
    pairs = range(0, HEADS, V7X_LANES // HEAD_DIM)
    return ([jnp.where(low_head, outs[hd], outs[hd + 1]) for hd in pairs],
            [jnp.where(low_head, lses[hd], lses[hd + 1]) for hd in pairs])


ATTN_PLANES = GROUP_WIDTH // V7X_LANES


def _stream_attn_kernel(s_ref, o_ref, l_ref, first_bias_ref, band_bias_ref, *, dil, seq):
    blk = KEYS_PER_QUERY

    @pl.when(pl.program_id(0) == 0)
    def _():
        qi = lax.broadcasted_iota(jnp.int32, (blk, 1), 0)
        first = lax.broadcasted_iota(jnp.int32, (1, blk), 1) <= qi
        dist = qi + blk - lax.broadcasted_iota(jnp.int32, (1, 2 * blk), 1)
        first_bias_ref[...] = jnp.where(first, 0.0, NEG_INF)
        band_bias_ref[...] = jnp.where((dist >= 0) & (dist <= blk), 0.0, NEG_INF)

    head_ones = [jnp.where(m, 1.0, 0.0).astype(BF16) for m in _head_lane_masks()]
    low_head = lax.broadcasted_iota(jnp.int32, (1, V7X_LANES), 1) < HEAD_DIM
    for r in range(dil):
        base = r * QKV_GROUP_WIDTH
        qc = slice(base, base + GROUP_WIDTH)
        kc = slice(base + GROUP_WIDTH, base + 2 * GROUP_WIDTH)
        vc = slice(base + 2 * GROUP_WIDTH, base + 3 * GROUP_WIDTH)
        for i in range(seq // (dil * blk)):
            krows = slice(max(i - 1, 0) * blk, (i + 1) * blk)
            outs, lses = _band_block(s_ref[0, i * blk:(i + 1) * blk, qc], s_ref[0, krows, kc].T, s_ref[0, krows, vc],
                                     first_bias_ref if i == 0 else band_bias_ref, head_ones, low_head)
            for c in range(ATTN_PLANES):
                o_ref[0, c, r, i * blk:(i + 1) * blk, :] = outs[c]
                l_ref[0, c, r, i * blk:(i + 1) * blk, :] = lses[c]


def _merge_groups(outs, lses, exp_fn):
    mx = functools.reduce(jnp.maximum, lses)
    ws = [exp_fn(l - mx) for l in lses]
    num = functools.reduce(lambda a, b: a + b, [w * o for w, o in zip(ws, outs)])
    return num / functools.reduce(lambda a, b: a + b, ws)


def _stream_attn_call(streams, dil, seq):
    b = streams.shape[0]

    def whole(a):
        zeros = (0,) * (len(a.shape) - 1)
        return pl.BlockSpec((1,) + tuple(a.shape[1:]), lambda bi: (bi,) + zeros)

    planes = jax.ShapeDtypeStruct((b, ATTN_PLANES, dil, seq // dil, V7X_LANES), F32)
    pipelined = _nbytes((seq, QKV_GROUP_WIDTH), BF16) + 2 * _nbytes((seq, GROUP_WIDTH), F32)
    scratch = [pltpu.VMEM((KEYS_PER_QUERY, KEYS_PER_QUERY), F32),
               pltpu.VMEM((KEYS_PER_QUERY, 2 * KEYS_PER_QUERY), F32)]
    scratch_bytes = 3 * _nbytes((KEYS_PER_QUERY, KEYS_PER_QUERY), F32)
    return pl.pallas_call(
        functools.partial(_stream_attn_kernel, dil=dil, seq=seq),
        grid=(b,),
        in_specs=[whole(streams)],
        out_specs=[whole(planes), whole(planes)],
        out_shape=[planes, planes],
        scratch_shapes=scratch,
        compiler_params=pltpu.CompilerParams(dimension_semantics=("arbitrary",),
                                             vmem_limit_bytes=_vmem_limit(pipelined, scratch_bytes, 16 << 20)),
        name=f"stream_attn_d{dil}",
    )(streams)


def _token_order(ref, plane, stage_ref, slot):
    dil, rows = ref.shape[2], ref.shape[3]
    if dil == 1:
        return ref[0, plane, 0]
    if dil <= MAX_FAST_SUBLANE_STRIDE:
        for r in range(dil):
            stage_ref[slot, pl.ds(r, rows, stride=dil), :] = ref[0, plane, r]
        return stage_ref[slot]
    inner = MAX_FAST_SUBLANE_STRIDE
    outer = dil // inner
    assert outer <= MAX_FAST_SUBLANE_STRIDE and outer * inner == dil
    tmp = stage_ref.shape[0] - 1
    mid = rows * outer
    for b in range(inner):
        for a in range(outer):
            stage_ref[tmp, pl.ds(b * mid + a, rows, stride=outer), :] = ref[0, plane, inner * a + b]
    for b in range(inner):
        stage_ref[slot, pl.ds(b, mid, stride=inner), :] = stage_ref[tmp, b * mid:(b + 1) * mid, :]
    return stage_ref[slot]


def _mix_out_rows(x, pp, ga, attn, wabr_ref, wo_ref, g2_ref, wg_ref, wu_ref, wd_ref, gf_ref):
    attn_br = jnp.dot(attn, wabr_ref[...], preferred_element_type=F32)
    merged = pp + ga * attn_br
    x = x + jnp.dot(merged.astype(BF16), wo_ref[...], preferred_element_type=F32)
    x = _macaron_ffn(x, g2_ref, wg_ref, wu_ref, wd_ref)
    return _rmsnorm(x, gf_ref[...])


def _mix_out_kernel(x_ref, pp_ref, ga_ref, *rest, n_groups):
    group_refs, rest = rest[:2 * n_groups], rest[2 * n_groups:]
    weights, rest = rest[:7], rest[7:]
    xs_ref, spp_ref, sga_ref, sattn_ref, y_ref, ys_ref, stage_ref, attn_ref = rest

    @pl.when(pl.program_id(0) == 0)
    def _():
        attn_ref[...] = jnp.zeros(attn_ref.shape, BF16)
        ys_ref[...] = _mix_out_rows(xs_ref[...], spp_ref[...], sga_ref[...], sattn_ref[...], *weights)

    y_ref[...] = _mix_out_rows(x_ref[...], pp_ref[...], ga_ref[...], attn_ref[...], *weights)
    for c in range(ATTN_PLANES):
        outs = [_token_order(group_refs[2 * g], c, stage_ref, 2 * g) for g in range(n_groups)]
        lses = [_token_order(group_refs[2 * g + 1], c, stage_ref, 2 * g + 1) for g in range(n_groups)]
        attn_ref[:, c * V7X_LANES:(c + 1) * V7X_LANES] = _merge_groups(outs, lses, jnp.exp2).astype(BF16)


def _mix_out_call(x, pp, ga, group_planes, decode, wabr, wo, g2, wg, wu, wd, gf, tm):
    m, d = x.shape
    f = wg.shape[1]
    n_tiles = m // tm
    row = pl.BlockSpec((tm, d), lambda i: (jnp.maximum(i - 1, 0), 0))
    plane_specs = []
    for p in group_planes:
        dil, seq_rows = p.shape[2], p.shape[3]
        tiles = seq_rows * dil // tm
        plane_specs.append(pl.BlockSpec(
            (1, ATTN_PLANES, dil, tm // dil, V7X_LANES),
            lambda i, tiles=tiles: (jnp.minimum(i, n_tiles - 1) // tiles, 0, 0,
                                    jnp.minimum(i, n_tiles - 1) % tiles, 0)))

    def whole(a):
        return pl.BlockSpec(a.shape, lambda i: (0, 0))

    n_dec = decode[0].shape[0]
    pipelined = (4 * _nbytes((tm, d), F32) + len(group_planes) * _nbytes((tm, GROUP_WIDTH), F32)
                 + 4 * _nbytes((n_dec, d), F32))
    scratch = [pltpu.VMEM((len(group_planes) + 1, tm, V7X_LANES), F32), pltpu.VMEM((tm, GROUP_WIDTH), BF16)]
    resident = 3 * _nbytes((d, f), BF16) + _nbytes((d, d), BF16) + _nbytes((GROUP_WIDTH, d), BF16)
    temps = 4 * _nbytes((tm, FFN_CHUNK), F32) + 4 * _nbytes((tm, d), F32)
    return pl.pallas_call(
        functools.partial(_mix_out_kernel, n_groups=len(group_planes) // 2),
        grid=(n_tiles + 1,),
        in_specs=[row, row, row] + plane_specs
                 + [_resident((GROUP_WIDTH, d)), _resident((d, d)), _resident((1, d)),
                    _resident((d, f)), _resident((d, f)), _resident((f, d)), _resident((1, d))]
                 + [whole(a) for a in decode],
        out_specs=[row, pl.BlockSpec((n_dec, d), lambda i: (0, 0))],
        out_shape=[jax.ShapeDtypeStruct((m, d), F32), jax.ShapeDtypeStruct((n_dec, d), F32)],
        scratch_shapes=scratch,
        compiler_params=pltpu.CompilerParams(dimension_semantics=("arbitrary",),
                                             vmem_limit_bytes=_vmem_limit(pipelined, resident, temps)),
        name="mix_out",
    )(x, pp, ga, *group_planes, wabr, wo, g2, wg, wu, wd, gf, *decode)


def _sample_mix_in_kernel(x_ref, g_ref, win_ref, state_ref, cos_ref, sin_ref, wgrp_ref, pscale_ref, wpbr_ref,
                          qkv0_ref, qkv1_ref, qkv2_ref, u_ref, pp_ref, ga_ref):
    h = _rmsnorm(x_ref[...], g_ref[...]).astype(BF16)
    u = jnp.dot(h, win_ref[:, :POOL_WIDTH], preferred_element_type=F32)
    u_ref[...] = u
    pooled = []
    for gi, w in enumerate(POOL_WINDOWS):
        lo, hi = gi * POOL_GROUP, (gi + 1) * POOL_GROUP
        tok = u[:, lo:hi]
        total = tok
        for back in range(1, w):
            total = total + state_ref[POOL_STATE - back, :, lo:hi]
        pooled.append(total / float(min(PAST_LEN + 1, w)) - tok)
    pool_br = _pool_branch(pooled, wgrp_ref, pscale_ref, wpbr_ref)
    gate_pool, gate_attn = _gates(h, win_ref)
    pp_ref[...] = gate_pool * pool_br
    ga_ref[...] = gate_attn
    first_half = _first_half_mask()
    n = x_ref.shape[0]
    for g, qkv_ref in enumerate((qkv0_ref, qkv1_ref, qkv2_ref)):
        qkv = _qkv_group(h, win_ref, g, cos_ref[...], sin_ref[...], first_half, ATTN_SCALE)
        padded = jnp.concatenate([qkv, jnp.zeros((V7X_LANES - n, QKV_GROUP_WIDTH), F32)], axis=0)
        qkv_ref[...] = jnp.transpose(padded)[:, :n]


def _sample_mix_in_call(x, g, win, state_t, cos, sin, wgrp, pscale, wpbr):
    n, d = x.shape
    n_g = len(ATTN_GROUPS)

    def full(shape):
        zeros = (0,) * len(shape)
        return pl.BlockSpec(shape, lambda i: zeros)

    assert n <= V7X_LANES
    out_shape = ([jax.ShapeDtypeStruct((QKV_GROUP_WIDTH, n), F32)] * n_g
                 + [jax.ShapeDtypeStruct((n, POOL_WIDTH), F32),
                    jax.ShapeDtypeStruct((n, d), F32), jax.ShapeDtypeStruct((n, d), F32)])
    ins = (x, g, win, state_t, cos, sin, wgrp, pscale, wpbr)
    resident = sum(_nbytes(a.shape, a.dtype) for a in ins)
    return pl.pallas_call(
        _sample_mix_in_kernel,
        grid=(1,),
        in_specs=[full(a.shape) for a in ins],
        out_specs=[full(o.shape) for o in out_shape],
        out_shape=out_shape,
        compiler_params=pltpu.CompilerParams(dimension_semantics=("arbitrary",),
                                             vmem_limit_bytes=_vmem_limit(resident, 0, 8 << 20)),
        name="sample_mix_in",
    )(*ins)


def _head_sums(x):
    return jnp.concatenate([jnp.sum(x[hd * HEAD_DIM:(hd + 1) * HEAD_DIM], axis=0, keepdims=True)
                            for hd in range(HEADS)], axis=0)


def _head_expand(y):
    return jnp.concatenate([jnp.broadcast_to(y[hd:hd + 1], (HEAD_DIM, y.shape[1])) for hd in range(HEADS)],
                           axis=0)


def _cache_step_kernel(qkv0_ref, qkv1_ref, qkv2_ref, c0_ref, c1_ref, c2_ref,
                       n0_ref, n1_ref, n2_ref, attn_ref):
    n = pl.program_id(0)
    outs, lses = [], []
    groups = ((qkv0_ref, c0_ref, n0_ref), (qkv1_ref, c1_ref, n1_ref), (qkv2_ref, c2_ref, n2_ref))
    request = lax.broadcasted_iota(jnp.int32, qkv0_ref.shape, 1) == n
    for (qkv_ref, c_ref, new_ref), (window, dil) in zip(groups, ATTN_GROUPS):
        qkv_col = jnp.sum(jnp.where(request, qkv_ref[...], 0.0), axis=1, keepdims=True)
        q = qkv_col[:GROUP_WIDTH]
        k_new = qkv_col[GROUP_WIDTH:2 * GROUP_WIDTH]
        v_new = qkv_col[2 * GROUP_WIDTH:]
        kt = c_ref[0, :GROUP_WIDTH, :]
        vt = c_ref[0, GROUP_WIDTH:, :]
        lane = lax.broadcasted_iota(jnp.int32, (1, window), 1)
        sc = jnp.where(lane % dil == 0, _head_sums(kt * q), NEG_INF)
        sc_new = _head_sums(k_new * q)
        mx = jnp.maximum(jnp.max(sc, axis=1, keepdims=True), sc_new)
        e = jnp.exp(sc - mx)
        e_new = jnp.exp(sc_new - mx)
        den = jnp.sum(e, axis=1, keepdims=True) + e_new
        o_un = jnp.sum(vt * _head_expand(e), axis=1, keepdims=True) + _head_expand(e_new) * v_new
        outs.append(o_un / _head_expand(den))
        lses.append(_head_expand(mx + jnp.log(den)))
        shifted = pltpu.roll(c_ref[0], window - 1, 1)
        new_ref[0] = jnp.where(lane == window - 1, qkv_col[GROUP_WIDTH:], shifted)
    attn_col = _merge_groups(outs, lses, jnp.exp)
    col = lax.broadcasted_iota(jnp.int32, attn_ref.shape, 1)

    @pl.when(n == 0)
    def _():
        attn_ref[...] = jnp.zeros(attn_ref.shape, F32)

    attn_ref[...] = jnp.where(col == n, attn_col, attn_ref[...])


def _cache_step_call(qkv_cols, windows_t):
    n = qkv_cols[0].shape[1]
    qspec = pl.BlockSpec((QKV_GROUP_WIDTH, n), lambda i: (0, 0))
    cspecs = [pl.BlockSpec((1,) + c.shape[1:], lambda i: (i, 0, 0)) for c in windows_t]
    pipelined = 2 * sum(_nbytes(c.shape[1:], F32) for c in windows_t) + 3 * _nbytes((QKV_GROUP_WIDTH, V7X_LANES), F32)
    res = pl.pallas_call(
        _cache_step_kernel,
        grid=(n,),
        in_specs=[qspec] * len(qkv_cols) + cspecs,
        out_specs=cspecs + [pl.BlockSpec((GROUP_WIDTH, n), lambda i: (0, 0))],
        out_shape=[jax.ShapeDtypeStruct(c.shape, F32) for c in windows_t]
                  + [jax.ShapeDtypeStruct((GROUP_WIDTH, n), F32)],
        compiler_params=pltpu.CompilerParams(dimension_semantics=("arbitrary",),
                                             vmem_limit_bytes=_vmem_limit(pipelined, 0, 16 << 20)),
        name="cache_step",
    )(*qkv_cols, *windows_t)
    return res[:-1], res[-1]


def _rope_tables(pos):
    half = HEAD_DIM // 2
    inv = ROPE_THETA ** (-jnp.arange(half, dtype=F32) * (2.0 / HEAD_DIM))
    ang = pos.astype(F32)[:, None] * inv[None, :]
    cos, sin = jnp.cos(ang), jnp.sin(ang)
    reps = V7X_LANES // HEAD_DIM
    return (jnp.tile(jnp.concatenate([cos, cos], axis=1), (1, reps)),
            jnp.tile(jnp.concatenate([-sin, sin], axis=1), (1, reps)))


def _window_to_lanes(w):
    n, width = w.shape[:2]
    return jnp.transpose(w, (0, 2, 3, 4, 1)).reshape(n, 2 * GROUP_WIDTH, width)


def _window_from_lanes(wt):
    n, _, width = wt.shape
    return jnp.transpose(wt.reshape(n, 2, HEADS, HEAD_DIM, width), (0, 4, 1, 2, 3))[None]


def kernel(x_prompt, x_sample, state_pool, cache_kv_w128, cache_kv_w512, cache_kv_w2048, norm_ffn1, ffn1_w_gate, ffn1_w_up, ffn1_w_down, norm_mix, w_in, w_pool_grp, pool_scale, w_pool_br, w_attn_br, w_o, norm_ffn2, ffn2_w_gate, ffn2_w_up, ffn2_w_down, norm_final):
    b, s, d = x_prompt.shape
    n, dec_seq, _ = x_sample.shape
    caches = (cache_kv_w128, cache_kv_w512, cache_kv_w2048)
    assert norm_ffn1.shape[0] == 1 and dec_seq == 1, "single layer, single decode token"
    ts = PROMPT_ROW_TILE
    assert s % ts == 0 and ts >= POOL_CARRY
    for cache, (window, dil) in zip(caches, ATTN_GROUPS):
        assert window // dil == KEYS_PER_QUERY and s % (dil * KEYS_PER_QUERY) == 0 and ts % (16 * dil) == 0
        assert cache.shape[2] == window, "decode path expects full windows of history"

    row = lambda v: v.reshape(1, -1)
    bf = lambda w: w.astype(BF16)
    g1, g2, gm, gf = row(norm_ffn1[0]), row(norm_ffn2[0]), row(norm_mix[0]), row(norm_final)
    f1 = (bf(ffn1_w_gate[0]), bf(ffn1_w_up[0]), bf(ffn1_w_down[0]))
    pscale = row(pool_scale[0])

    cos_p, sin_p = _rope_tables(jnp.arange(s, dtype=jnp.int32))
    later = (ffn2_w_gate[0], ffn2_w_up[0], ffn2_w_down[0], w_in[0],
             w_pool_grp[0].reshape(-1, POOL_GROUP), w_pool_br[0], w_attn_br[0], w_o[0])
    x1, xs1, (f2g, f2u, f2d, win, wgrp, wpbr, wabr, wo) = _ffn_call(
        x_prompt.reshape(b * s, d), x_sample.reshape(n, d), g1, *f1, tm=FFN_ROW_TILE, cast=later)
    f2 = (f2g, f2u, f2d)
    wgrp = wgrp.reshape(w_pool_grp.shape[1:])
    (st0, st1, st2, kvt0, kvt1, kvt2, tail, pp, ga) = _prompt_mix_in_call(
        x1.reshape(b, s, d), gm, win, cos_p, sin_p, wgrp, pscale, wpbr, ts)
    group_planes = []
    for streams, (_, dil) in zip((st0, st1, st2), ATTN_GROUPS):
        group_planes += _stream_attn_call(streams, dil, s)
    pool_prompt = tail[None, :, POOL_CARRY - POOL_STATE:, :]

    cos_s, sin_s = _rope_tables(PAST_LEN + jnp.arange(dec_seq, dtype=jnp.int32))
    state_t = jnp.swapaxes(state_pool[0], 0, 1)
    (sqkv0, sqkv1, sqkv2, u_new, spp, sga) = _sample_mix_in_call(
        xs1, gm, win, state_t, cos_s, sin_s, wgrp, pscale, wpbr)
    new_windows, attn_t = _cache_step_call((sqkv0, sqkv1, sqkv2),
                                           [_window_to_lanes(c[0]) for c in caches])
    pool_sample = jnp.swapaxes(jnp.concatenate([state_t[1:], u_new[None]], axis=0), 0, 1)[None]

    y_prompt, y_sample = _mix_out_call(x1, pp.reshape(b * s, d), ga.reshape(b * s, d), tuple(group_planes),
                                       (xs1, spp, sga, attn_t.T.astype(BF16)), wabr, wo, g2, *f2, gf, tm=ts)
    y_prompt = y_prompt.reshape(b, s, d)
    y_sample = y_sample.reshape(n, dec_seq, d)

    return (y_prompt, y_sample, pool_prompt,
            _window_from_lanes(kvt0), _window_from_lanes(kvt1), _window_from_lanes(kvt2),
            pool_sample, *[_window_from_lanes(w) for w in new_windows])
```

```python
import functools

import jax
import jax.numpy as jnp
from jax import lax
from jax.experimental import pallas as pl
from jax.experimental.pallas import tpu as pltpu

F32 = jnp.float32
BF16 = jnp.bfloat16

PAST_LEN = 16384
POOL_WINDOWS = (2, 4, 8, 16)
POOL_GROUP = 128
POOL_WIDTH = POOL_GROUP * len(POOL_WINDOWS)
POOL_STATE = max(POOL_WINDOWS) - 1
POOL_CARRY = 16
HEAD_DIM = 64
HEADS = 4
GROUP_WIDTH = HEADS * HEAD_DIM
ATTN_GROUPS = ((128, 1), (512, 4), (2048, 16))
KEYS_PER_QUERY = 128
QKV_GROUP_WIDTH = 3 * GROUP_WIDTH
ROPE_THETA = 10000.0
RMS_EPS = 1e-6
NEG_INF = -1e30
ATTN_SCALE = HEAD_DIM ** -0.5
LOG2_E = 1.4426950408889634

V7X_VMEM_BYTES = 64 * 1024 * 1024
V7X_LANES = 128
V7X_MXU_COLS = 256
MAX_FAST_SUBLANE_STRIDE = 4

PROMPT_ROW_TILE = 512
FFN_ROW_TILE = 1024
FFN_CHUNK = 4 * V7X_MXU_COLS


def _vmem_limit(pipelined_bytes, resident_bytes, temp_bytes):
    need = 2 * pipelined_bytes + resident_bytes + temp_bytes
    return int(min(need, V7X_VMEM_BYTES - (4 << 20)))


def _nbytes(shape, dtype):
    n = 1
    for s in shape:
        n *= s
    return n * jnp.dtype(dtype).itemsize


def _resident(shape):
    zeros = (0,) * len(shape)
    return pl.BlockSpec(shape, lambda *_: zeros, pipeline_mode=pl.Buffered(1))


def _rmsnorm(x, g):
    ms = jnp.mean(x * x, axis=-1, keepdims=True)
    return x * lax.rsqrt(ms + RMS_EPS) * g


def _ffn_chunks(width):
    return [(lo, min(lo + FFN_CHUNK, width)) for lo in range(0, width, FFN_CHUNK)]


def _swiglu(xn, wg_ref, wu_ref, wd_ref):
    acc = None
    for lo, hi in _ffn_chunks(wg_ref.shape[1]):
        gate = jnp.dot(xn, wg_ref[:, lo:hi], preferred_element_type=F32)
        up = jnp.dot(xn, wu_ref[:, lo:hi], preferred_element_type=F32)
        hid = (gate * jax.nn.sigmoid(gate) * up).astype(BF16)
        part = jnp.dot(hid, wd_ref[lo:hi, :], preferred_element_type=F32)
        acc = part if acc is None else acc + part
    return acc


def _macaron_ffn(x, g_ref, wg_ref, wu_ref, wd_ref):
    xn = _rmsnorm(x, g_ref[...]).astype(BF16)
    return x + 0.5 * _swiglu(xn, wg_ref, wu_ref, wd_ref)


def _ffn_kernel(x_ref, xs_ref, g_ref, wg_ref, wu_ref, wd_ref, *rest):
    n_cast = (len(rest) - 2) // 2
    o_ref, os_ref = rest[n_cast], rest[n_cast + 1]

    @pl.when(pl.program_id(0) == 0)
    def _():
        os_ref[...] = _macaron_ffn(xs_ref[...], g_ref, wg_ref, wu_ref, wd_ref)

    o_ref[...] = _macaron_ffn(x_ref[...], g_ref, wg_ref, wu_ref, wd_ref)
    for src_ref, dst_ref in zip(rest[:n_cast], rest[n_cast + 2:]):
        dst_ref[...] = src_ref[...].astype(BF16)


BF16_SUBLANE_ROWS = 16


def _ffn_call(x, xs, g, wg, wu, wd, tm, cast=()):
    m, d = x.shape
    f = wg.shape[1]
    steps = m // tm
    row = pl.BlockSpec((tm, d), lambda i: (i, 0))
    dec = pl.BlockSpec(xs.shape, lambda i: (0, 0))
    cast_specs, cast_bytes = [], 0
    for w in cast:
        rows = w.shape[0]
        per_step = next(r for r in range(BF16_SUBLANE_ROWS, rows + 1, BF16_SUBLANE_ROWS)
                        if rows % r == 0 and r * steps >= rows)
        last = rows // per_step - 1
        cast_specs.append(pl.BlockSpec((per_step, w.shape[1]), lambda i, last=last: (jnp.minimum(i, last), 0)))
        cast_bytes += _nbytes((per_step, w.shape[1]), F32) + _nbytes((per_step, w.shape[1]), BF16)
    limit = _vmem_limit(2 * _nbytes((tm, d), F32) + 2 * _nbytes(xs.shape, F32) + cast_bytes,
                        3 * _nbytes((d, f), BF16),
                        4 * _nbytes((tm, FFN_CHUNK), F32) + 2 * _nbytes((tm, d), F32))
    res = pl.pallas_call(
        _ffn_kernel,
        grid=(steps,),
        in_specs=[row, dec, _resident((1, d)), _resident((d, f)), _resident((d, f)), _resident((f, d))]
                 + cast_specs,
        out_specs=[row, dec] + cast_specs,
        out_shape=[jax.ShapeDtypeStruct((m, d), F32), jax.ShapeDtypeStruct(xs.shape, F32)]
                  + [jax.ShapeDtypeStruct(w.shape, BF16) for w in cast],
        compiler_params=pltpu.CompilerParams(dimension_semantics=("arbitrary",), vmem_limit_bytes=limit),
        name="ffn",
    )(x, xs, g, wg, wu, wd, *cast)
    return res[0], res[1], res[2:]


def _rope(t, cos, sin_signed, first_half):
    outs = []
    for c in range(t.shape[1] // V7X_LANES):
        tc = t[:, c * V7X_LANES:(c + 1) * V7X_LANES]
        partner = jnp.where(first_half,
                            pltpu.roll(tc, V7X_LANES - HEAD_DIM // 2, 1),
                            pltpu.roll(tc, HEAD_DIM // 2, 1))
        outs.append(tc * cos + partner * sin_signed)
    return jnp.concatenate(outs, axis=1)


def _first_half_mask():
    lane = lax.broadcasted_iota(jnp.int32, (1, V7X_LANES), 1)
    return (lane % HEAD_DIM) < (HEAD_DIM // 2)


def _qkv_group(h, win_ref, g, cos, sin_signed, first_half, q_scale):
    base = POOL_WIDTH + g * QKV_GROUP_WIDTH
    qkv = jnp.dot(h, win_ref[:, base:base + QKV_GROUP_WIDTH], preferred_element_type=F32)
    q = _rope(qkv[:, :GROUP_WIDTH], cos, sin_signed, first_half) * q_scale
    k = _rope(qkv[:, GROUP_WIDTH:2 * GROUP_WIDTH], cos, sin_signed, first_half)
    return jnp.concatenate([q, k, qkv[:, 2 * GROUP_WIDTH:]], axis=1)


def _pool_branch(window_means_minus_tok, wgrp_ref, pscale_ref, wpbr_ref):
    mixed = [jnp.dot(p.astype(BF16), wgrp_ref[gi], preferred_element_type=F32)
             for gi, p in enumerate(window_means_minus_tok)]
    pool = jnp.concatenate(mixed, axis=1) * pscale_ref[...]
    return jnp.dot(pool.astype(BF16), wpbr_ref[...], preferred_element_type=F32)


def _gates(h, win_ref):
    d = win_ref.shape[0]
    base = POOL_WIDTH + len(ATTN_GROUPS) * QKV_GROUP_WIDTH
    gate_pool = jax.nn.sigmoid(jnp.dot(h, win_ref[:, base:base + d], preferred_element_type=F32))
    gate_attn = jax.nn.sigmoid(jnp.dot(h, win_ref[:, base + d:base + 2 * d], preferred_element_type=F32))
    return gate_pool, gate_attn


def _prompt_mix_in_kernel(x_ref, g_ref, win_ref, cos_ref, sin_ref, wgrp_ref, pscale_ref, wpbr_ref,
                          s0_ref, s1_ref, s2_ref,
                          kvt0_ref, kvt1_ref, kvt2_ref, tail_ref, pp_ref, ga_ref, uext_ref, qkv_ref, mid_ref,
                          *, ts):
    j = pl.program_id(1)

    @pl.when(j == 0)
    def _():
        uext_ref[0:POOL_CARRY, :] = jnp.zeros((POOL_CARRY, POOL_WIDTH), F32)

    h = _rmsnorm(x_ref[0], g_ref[...]).astype(BF16)

    cos = cos_ref[...]
    sin_signed = sin_ref[...]
    first_half = _first_half_mask()
    outs = ((s0_ref, kvt0_ref), (s1_ref, kvt1_ref), (s2_ref, kvt2_ref))
    planes = QKV_GROUP_WIDTH // V7X_LANES
    slot = 0
    for g in sorted(range(len(ATTN_GROUPS)), key=lambda g: -ATTN_GROUPS[g][1]):
        stream_ref, kvt_ref = outs[g]
        dil = ATTN_GROUPS[g][1]
        qkv = _qkv_group(h, win_ref, g, cos, sin_signed, first_half, ATTN_SCALE * LOG2_E)
        if dil == 1:
            stream_ref[0] = qkv.astype(BF16)
        else:
            for c in range(planes):
                qkv_ref[slot + c] = qkv[:, c * V7X_LANES:(c + 1) * V7X_LANES]
            inner = min(dil, MAX_FAST_SUBLANE_STRIDE)
            outer = dil // inner
            assert outer <= MAX_FAST_SUBLANE_STRIDE and outer * inner == dil
            if outer > 1:
                for c in range(planes):
                    for bb in range(inner):
                        mid_ref[c, bb * (ts // inner):(bb + 1) * (ts // inner), :] = (
                            qkv_ref[slot + c, pl.ds(bb, ts // inner, stride=inner), :])
            for r in range(dil):
                if outer > 1:
                    a, bb = r // inner, r % inner
                    rows = jnp.concatenate(
                        [mid_ref[c, pl.ds(bb * (ts // inner) + a, ts // dil, stride=outer), :]
                         for c in range(planes)], axis=1)
                else:
                    rows = jnp.concatenate([qkv_ref[slot + c, pl.ds(r, ts // dil, stride=dil), :]
                                            for c in range(planes)], axis=1)
                stream_ref[0, :, r * QKV_GROUP_WIDTH:(r + 1) * QKV_GROUP_WIDTH] = rows.astype(BF16)
            slot += planes
        keep = kvt_ref.shape[2]
        kvt_ref[0] = qkv[ts - keep:, GROUP_WIDTH:].T

    u = jnp.dot(h, win_ref[:, :POOL_WIDTH], preferred_element_type=F32)
    uext_ref[POOL_CARRY:POOL_CARRY + ts, :] = u
    pos = j * ts + lax.broadcasted_iota(jnp.int32, (ts, 1), 0)
    pooled = []
    for gi, w in enumerate(POOL_WINDOWS):
        lo, hi = gi * POOL_GROUP, (gi + 1) * POOL_GROUP
        tok = u[:, lo:hi]
        total = tok
        for back in range(1, w):
            total = total + uext_ref[POOL_CARRY - back:POOL_CARRY - back + ts, lo:hi]
        cnt = jnp.minimum(pos + 1, w).astype(F32)
        pooled.append(total / cnt - tok)
    pool_br = _pool_branch(pooled, wgrp_ref, pscale_ref, wpbr_ref)
    uext_ref[0:POOL_CARRY, :] = u[ts - POOL_CARRY:, :]
    tail_ref[0] = u[ts - POOL_CARRY:, :]
    gate_pool, gate_attn = _gates(h, win_ref)
    pp_ref[0] = gate_pool * pool_br
    ga_ref[0] = gate_attn


def _prompt_mix_in_call(x, g, win, cos, sin, wgrp, pscale, wpbr, ts):
    b, s, d = x.shape
    seq_tiles = s // ts
    zw = win.shape[1]

    def tile(width):
        return pl.BlockSpec((1, ts, width), lambda bi, j: (bi, j, 0))

    stream_specs, stream_shapes, kvt_specs, kvt_shapes = [], [], [], []
    for window, dil in ATTN_GROUPS:
        stream_specs.append(pl.BlockSpec((1, ts // dil, dil * QKV_GROUP_WIDTH), lambda bi, j: (bi, j, 0)))
        stream_shapes.append(jax.ShapeDtypeStruct((b, s // dil, dil * QKV_GROUP_WIDTH), BF16))
        keep = min(window, s)
        if keep >= ts:
            first = seq_tiles - keep // ts
            kvt_specs.append(pl.BlockSpec((1, 2 * GROUP_WIDTH, ts),
                                          lambda bi, j, first=first: (bi, 0, jnp.maximum(j - first, 0))))
        else:
            kvt_specs.append(pl.BlockSpec((1, 2 * GROUP_WIDTH, keep), lambda bi, j: (bi, 0, 0)))
        kvt_shapes.append(jax.ShapeDtypeStruct((b, 2 * GROUP_WIDTH, keep), F32))
    n_g = len(ATTN_GROUPS)
    out_specs = (stream_specs + kvt_specs
                 + [pl.BlockSpec((1, POOL_CARRY, POOL_WIDTH), lambda bi, j: (bi, 0, 0)), tile(d), tile(d)])
    out_shape = (stream_shapes + kvt_shapes
                 + [jax.ShapeDtypeStruct((b, POOL_CARRY, POOL_WIDTH), F32),
                    jax.ShapeDtypeStruct((b, s, d), F32), jax.ShapeDtypeStruct((b, s, d), F32)])
    table = pl.BlockSpec((ts, V7X_LANES), lambda bi, j: (j, 0))
    pipelined = (3 * _nbytes((ts, d), F32) + n_g * _nbytes((ts, QKV_GROUP_WIDTH), BF16)
                 + n_g * _nbytes((ts, 2 * GROUP_WIDTH), F32) + 2 * _nbytes((ts, V7X_LANES), F32))
    resident = _nbytes((d, zw), BF16) + _nbytes(wpbr.shape, BF16) + _nbytes(wgrp.shape, BF16)
    n_dilated = sum(1 for _, dil in ATTN_GROUPS if dil != 1)
    scratch = [pltpu.VMEM((ts + POOL_CARRY, POOL_WIDTH), F32),
               pltpu.VMEM((n_dilated * QKV_GROUP_WIDTH // V7X_LANES, ts, V7X_LANES), F32),
               pltpu.VMEM((QKV_GROUP_WIDTH // V7X_LANES, ts, V7X_LANES), F32)]
    temps = (_nbytes((ts + POOL_CARRY, POOL_WIDTH), F32) + (n_dilated + 1) * _nbytes((ts, QKV_GROUP_WIDTH), F32)
             + 6 * _nbytes((ts, d), F32))
    return pl.pallas_call(
        functools.partial(_prompt_mix_in_kernel, ts=ts),
        grid=(b, seq_tiles),
        in_specs=[tile(d), _resident((1, d)), _resident((d, zw)), table, table,
                  _resident(wgrp.shape), _resident((1, POOL_WIDTH)), _resident(wpbr.shape)],
        out_specs=out_specs,
        out_shape=out_shape,
        scratch_shapes=scratch,
        compiler_params=pltpu.CompilerParams(dimension_semantics=("arbitrary", "arbitrary"),
                                             vmem_limit_bytes=_vmem_limit(pipelined, resident, temps)),
        name="prompt_mix_in",
    )(x, g, win, cos, sin, wgrp, pscale, wpbr)


def _head_lane_masks():
    lane = lax.broadcasted_iota(jnp.int32, (1, GROUP_WIDTH), 1)
    return [(lane // HEAD_DIM) == hd for hd in range(HEADS)]


def _band_scores(q, kt, head_ones):
    q_heads = jnp.concatenate([q * one for one in head_ones], axis=0)
    return jnp.dot(q_heads, kt, preferred_element_type=F32)


def _band_softmax(sc, bias_ref, keys):
    blk = sc.shape[0] // HEADS
    heads = []
    for hd in range(HEADS):
        s_h = sc[hd * blk:(hd + 1) * blk, :] + bias_ref[:, :keys]
        mx = jnp.max(s_h, axis=1, keepdims=True)
        e = jnp.exp2(s_h - mx)
        heads.append((e.astype(BF16), mx, jnp.sum(e, axis=1, keepdims=True)))
    return heads


def _band_outputs(heads, v, low_head):
    outs, lses = [], []
    for hd, (e, mx, den) in enumerate(heads):
        tile = slice((hd * HEAD_DIM // V7X_LANES) * V7X_LANES, (hd * HEAD_DIM // V7X_LANES + 1) * V7X_LANES)
        o_un = jnp.dot(e, v[:, tile], preferred_element_type=F32)
        outs.append(o_un * (1.0 / den))
        lses.append(jnp.broadcast_to(mx + jnp.log2(den), (e.shape[0], V7X_LANES)))
    pairs = range(0, HEADS, V7X_LANES // HEAD_DIM)
    return ([jnp.where(low_head, outs[hd], outs[hd + 1]) for hd in pairs],
            [jnp.where(low_head, lses[hd], lses[hd + 1]) for hd in pairs])


ATTN_PLANES = GROUP_WIDTH // V7X_LANES
ATTN_BLOCKS_SIDE_BY_SIDE = {1: 1, 4: 3, 16: 2}


def _stream_attn_kernel(s_ref, o_ref, l_ref, first_bias_ref, band_bias_ref, *, dil, seq):
    blk = KEYS_PER_QUERY

    @pl.when(pl.program_id(0) == 0)
    def _():
        qi = lax.broadcasted_iota(jnp.int32, (blk, 1), 0)
        first = lax.broadcasted_iota(jnp.int32, (1, blk), 1) <= qi
        dist = qi + blk - lax.broadcasted_iota(jnp.int32, (1, 2 * blk), 1)
        first_bias_ref[...] = jnp.where(first, 0.0, NEG_INF)
        band_bias_ref[...] = jnp.where((dist >= 0) & (dist <= blk), 0.0, NEG_INF)

    head_ones = [jnp.where(m, 1.0, 0.0).astype(BF16) for m in _head_lane_masks()]
    low_head = lax.broadcasted_iota(jnp.int32, (1, V7X_LANES), 1) < HEAD_DIM
    blocks = [(r, i) for r in range(dil) for i in range(seq // (dil * blk))]
    side = ATTN_BLOCKS_SIDE_BY_SIDE.get(dil, 1)
    for first in range(0, len(blocks), side):
        operands = []
        for r, i in blocks[first:first + side]:
            base = r * QKV_GROUP_WIDTH
            krows = slice(max(i - 1, 0) * blk, (i + 1) * blk)
            operands.append((s_ref[0, i * blk:(i + 1) * blk, base:base + GROUP_WIDTH],
                             s_ref[0, krows, base + GROUP_WIDTH:base + 2 * GROUP_WIDTH].T,
                             s_ref[0, krows, base + 2 * GROUP_WIDTH:base + 3 * GROUP_WIDTH]))
        scores = [_band_scores(q, kt, head_ones) for q, kt, _ in operands]
        softmaxes = [_band_softmax(sc, first_bias_ref if i == 0 else band_bias_ref, v.shape[0])
                     for sc, (_, _, v), (_, i) in zip(scores, operands, blocks[first:first + side])]
        for heads, (_, _, v), (r, i) in zip(softmaxes, operands, blocks[first:first + side]):
            outs, lses = _band_outputs(heads, v, low_head)
            for c in range(ATTN_PLANES):
                o_ref[0, c, r, i * blk:(i + 1) * blk, :] = outs[c]
                l_ref[0, c, r, i * blk:(i + 1) * blk, :] = lses[c]


def _merge_groups(outs, lses, exp_fn):
    mx = functools.reduce(jnp.maximum, lses)
    ws = [exp_fn(l - mx) for l in lses]
    num = functools.reduce(lambda a, b: a + b, [w * o for w, o in zip(ws, outs)])
    return num / functools.reduce(lambda a, b: a + b, ws)


def _stream_attn_call(streams, dil, seq):
    b = streams.shape[0]

    def whole(a):
        zeros = (0,) * (len(a.shape) - 1)
        return pl.BlockSpec((1,) + tuple(a.shape[1:]), lambda bi: (bi,) + zeros)

    planes = jax.ShapeDtypeStruct((b, ATTN_PLANES, dil, seq // dil, V7X_LANES), F32)
    pipelined = _nbytes((seq, QKV_GROUP_WIDTH), BF16) + 2 * _nbytes((seq, GROUP_WIDTH), F32)
    scratch = [pltpu.VMEM((KEYS_PER_QUERY, KEYS_PER_QUERY), F32),
               pltpu.VMEM((KEYS_PER_QUERY, 2 * KEYS_PER_QUERY), F32)]
    scratch_bytes = 3 * _nbytes((KEYS_PER_QUERY, KEYS_PER_QUERY), F32)
    return pl.pallas_call(
        functools.partial(_stream_attn_kernel, dil=dil, seq=seq),
        grid=(b,),
        in_specs=[whole(streams)],
        out_specs=[whole(planes), whole(planes)],
        out_shape=[planes, planes],
        scratch_shapes=scratch,
        compiler_params=pltpu.CompilerParams(dimension_semantics=("arbitrary",),
                                             vmem_limit_bytes=_vmem_limit(pipelined, scratch_bytes, 16 << 20)),
        name=f"stream_attn_d{dil}",
    )(streams)


def _token_order(ref, plane, stage_ref, slot):
    dil, rows = ref.shape[2], ref.shape[3]
    if dil == 1:
        return ref[0, plane, 0]
    if dil <= MAX_FAST_SUBLANE_STRIDE:
        for r in range(dil):
            stage_ref[slot, pl.ds(r, rows, stride=dil), :] = ref[0, plane, r]
        return stage_ref[slot]
    inner = MAX_FAST_SUBLANE_STRIDE
    outer = dil // inner
    assert outer <= MAX_FAST_SUBLANE_STRIDE and outer * inner == dil
    tmp = stage_ref.shape[0] - 1
    mid = rows * outer
    for b in range(inner):
        for a in range(outer):
            stage_ref[tmp, pl.ds(b * mid + a, rows, stride=outer), :] = ref[0, plane, inner * a + b]
    for b in range(inner):
        stage_ref[slot, pl.ds(b, mid, stride=inner), :] = stage_ref[tmp, b * mid:(b + 1) * mid, :]
    return stage_ref[slot]


def _mix_out_rows(x, pp, ga, attn, wabr_ref, wo_ref, g2_ref, wg_ref, wu_ref, wd_ref, gf_ref):
    attn_br = jnp.dot(attn, wabr_ref[...], preferred_element_type=F32)
    merged = pp + ga * attn_br
    x = x + jnp.dot(merged.astype(BF16), wo_ref[...], preferred_element_type=F32)
    x = _macaron_ffn(x, g2_ref, wg_ref, wu_ref, wd_ref)
    return _rmsnorm(x, gf_ref[...])


def _mix_out_kernel(x_ref, pp_ref, ga_ref, *rest, n_groups):
    group_refs, rest = rest[:2 * n_groups], rest[2 * n_groups:]
    weights, rest = rest[:7], rest[7:]
    xs_ref, spp_ref, sga_ref, sattn_ref, y_ref, ys_ref, stage_ref = rest

    @pl.when(pl.program_id(0) == 0)
    def _():
        ys_ref[...] = _mix_out_rows(xs_ref[...], spp_ref[...], sga_ref[...], sattn_ref[...], *weights)

    tiles = []
    for c in range(ATTN_PLANES):
        outs = [_token_order(group_refs[2 * g], c, stage_ref, 2 * g) for g in range(n_groups)]
        lses = [_token_order(group_refs[2 * g + 1], c, stage_ref, 2 * g + 1) for g in range(n_groups)]
        tiles.append(_merge_groups(outs, lses, jnp.exp2).astype(BF16))
    attn = jnp.concatenate(tiles, axis=1)
    y_ref[...] = _mix_out_rows(x_ref[...], pp_ref[...], ga_ref[...], attn, *weights)


def _mix_out_call(x, pp, ga, group_planes, decode, wabr, wo, g2, wg, wu, wd, gf, tm):
    m, d = x.shape
    f = wg.shape[1]
    n_tiles = m // tm
    row = pl.BlockSpec((tm, d), lambda i: (i, 0))
    plane_specs = []
    for p in group_planes:
        dil, seq_rows = p.shape[2], p.shape[3]
        tiles = seq_rows * dil // tm
        plane_specs.append(pl.BlockSpec((1, ATTN_PLANES, dil, tm // dil, V7X_LANES),
                                        lambda i, tiles=tiles: (i // tiles, 0, 0, i % tiles, 0)))

    def whole(a):
        return pl.BlockSpec(a.shape, lambda i: (0, 0))

    n_dec = decode[0].shape[0]
    pipelined = (4 * _nbytes((tm, d), F32) + len(group_planes) * _nbytes((tm, GROUP_WIDTH), F32)
                 + 4 * _nbytes((n_dec, d), F32))
    scratch = [pltpu.VMEM((len(group_planes) + 1, tm, V7X_LANES), F32)]
    resident = 3 * _nbytes((d, f), BF16) + _nbytes((d, d), BF16) + _nbytes((GROUP_WIDTH, d), BF16)
    temps = 4 * _nbytes((tm, FFN_CHUNK), F32) + 4 * _nbytes((tm, d), F32)
    return pl.pallas_call(
        functools.partial(_mix_out_kernel, n_groups=len(group_planes) // 2),
        grid=(n_tiles,),
        in_specs=[row, row, row] + plane_specs
                 + [_resident((GROUP_WIDTH, d)), _resident((d, d)), _resident((1, d)),
                    _resident((d, f)), _resident((d, f)), _resident((f, d)), _resident((1, d))]
                 + [whole(a) for a in decode],
        out_specs=[row, pl.BlockSpec((n_dec, d), lambda i: (0, 0))],
        out_shape=[jax.ShapeDtypeStruct((m, d), F32), jax.ShapeDtypeStruct((n_dec, d), F32)],
        scratch_shapes=scratch,
        compiler_params=pltpu.CompilerParams(dimension_semantics=("arbitrary",),
                                             vmem_limit_bytes=_vmem_limit(pipelined, resident, temps)),
        name="mix_out",
    )(x, pp, ga, *group_planes, wabr, wo, g2, wg, wu, wd, gf, *decode)


def _sample_mix_in_kernel(x_ref, g_ref, win_ref, state_ref, cos_ref, sin_ref, wgrp_ref, pscale_ref, wpbr_ref,
                          qkv0_ref, qkv1_ref, qkv2_ref, u_ref, pp_ref, ga_ref):
    h = _rmsnorm(x_ref[...], g_ref[...]).astype(BF16)
    u = jnp.dot(h, win_ref[:, :POOL_WIDTH], preferred_element_type=F32)
    u_ref[...] = u
    pooled = []
    for gi, w in enumerate(POOL_WINDOWS):
        lo, hi = gi * POOL_GROUP, (gi + 1) * POOL_GROUP
        tok = u[:, lo:hi]
        total = tok
        for back in range(1, w):
            total = total + state_ref[POOL_STATE - back, :, lo:hi]
        pooled.append(total / float(min(PAST_LEN + 1, w)) - tok)
    pool_br = _pool_branch(pooled, wgrp_ref, pscale_ref, wpbr_ref)
    gate_pool, gate_attn = _gates(h, win_ref)
    pp_ref[...] = gate_pool * pool_br
    ga_ref[...] = gate_attn
    first_half = _first_half_mask()
    n = x_ref.shape[0]
    for g, qkv_ref in enumerate((qkv0_ref, qkv1_ref, qkv2_ref)):
        qkv = _qkv_group(h, win_ref, g, cos_ref[...], sin_ref[...], first_half, ATTN_SCALE)
        padded = jnp.concatenate([qkv, jnp.zeros((V7X_LANES - n, QKV_GROUP_WIDTH), F32)], axis=0)
        qkv_ref[...] = jnp.transpose(padded)[:, :n]


def _sample_mix_in_call(x, g, win, state_t, cos, sin, wgrp, pscale, wpbr):
    n, d = x.shape
    n_g = len(ATTN_GROUPS)

    def full(shape):
        zeros = (0,) * len(shape)
        return pl.BlockSpec(shape, lambda i: zeros)

    assert n <= V7X_LANES
    out_shape = ([jax.ShapeDtypeStruct((QKV_GROUP_WIDTH, n), F32)] * n_g
                 + [jax.ShapeDtypeStruct((n, POOL_WIDTH), F32),
                    jax.ShapeDtypeStruct((n, d), F32), jax.ShapeDtypeStruct((n, d), F32)])
    ins = (x, g, win, state_t, cos, sin, wgrp, pscale, wpbr)
    resident = sum(_nbytes(a.shape, a.dtype) for a in ins)
    return pl.pallas_call(
        _sample_mix_in_kernel,
        grid=(1,),
        in_specs=[full(a.shape) for a in ins],
        out_specs=[full(o.shape) for o in out_shape],
        out_shape=out_shape,
        compiler_params=pltpu.CompilerParams(dimension_semantics=("arbitrary",),
                                             vmem_limit_bytes=_vmem_limit(resident, 0, 8 << 20)),
        name="sample_mix_in",
    )(*ins)


def _head_sums(x):
    return jnp.concatenate([jnp.sum(x[hd * HEAD_DIM:(hd + 1) * HEAD_DIM], axis=0, keepdims=True)
                            for hd in range(HEADS)], axis=0)


def _head_expand(y):
    return jnp.concatenate([jnp.broadcast_to(y[hd:hd + 1], (HEAD_DIM, y.shape[1])) for hd in range(HEADS)],
                           axis=0)


def _cache_step_kernel(qkv0_ref, qkv1_ref, qkv2_ref, c0_ref, c1_ref, c2_ref,
                       n0_ref, n1_ref, n2_ref, attn_ref):
    n = pl.program_id(0)
    outs, lses = [], []
    groups = ((qkv0_ref, c0_ref, n0_ref), (qkv1_ref, c1_ref, n1_ref), (qkv2_ref, c2_ref, n2_ref))
    request = lax.broadcasted_iota(jnp.int32, qkv0_ref.shape, 1) == n
    for (qkv_ref, c_ref, new_ref), (window, dil) in zip(groups, ATTN_GROUPS):
        qkv_col = jnp.sum(jnp.where(request, qkv_ref[...], 0.0), axis=1, keepdims=True)
        q = qkv_col[:GROUP_WIDTH]
        k_new = qkv_col[GROUP_WIDTH:2 * GROUP_WIDTH]
        v_new = qkv_col[2 * GROUP_WIDTH:]
        kt = c_ref[0, :GROUP_WIDTH, :]
        vt = c_ref[0, GROUP_WIDTH:, :]
        lane = lax.broadcasted_iota(jnp.int32, (1, window), 1)
        sc = jnp.where(lane % dil == 0, _head_sums(kt * q), NEG_INF)
        sc_new = _head_sums(k_new * q)
        mx = jnp.maximum(jnp.max(sc, axis=1, keepdims=True), sc_new)
        e = jnp.exp(sc - mx)
        e_new = jnp.exp(sc_new - mx)
        den = jnp.sum(e, axis=1, keepdims=True) + e_new
        o_un = jnp.sum(vt * _head_expand(e), axis=1, keepdims=True) + _head_expand(e_new) * v_new
        outs.append(o_un / _head_expand(den))
        lses.append(_head_expand(mx + jnp.log(den)))
        shifted = pltpu.roll(c_ref[0], window - 1, 1)
        new_ref[0] = jnp.where(lane == window - 1, qkv_col[GROUP_WIDTH:], shifted)
    attn_col = _merge_groups(outs, lses, jnp.exp)
    col = lax.broadcasted_iota(jnp.int32, attn_ref.shape, 1)

    @pl.when(n == 0)
    def _():
        attn_ref[...] = jnp.zeros(attn_ref.shape, F32)

    attn_ref[...] = jnp.where(col == n, attn_col, attn_ref[...])


def _cache_step_call(qkv_cols, windows_t):
    n = qkv_cols[0].shape[1]
    qspec = pl.BlockSpec((QKV_GROUP_WIDTH, n), lambda i: (0, 0))
    cspecs = [pl.BlockSpec((1,) + c.shape[1:], lambda i: (i, 0, 0)) for c in windows_t]
    pipelined = 2 * sum(_nbytes(c.shape[1:], F32) for c in windows_t) + 3 * _nbytes((QKV_GROUP_WIDTH, V7X_LANES), F32)
    res = pl.pallas_call(
        _cache_step_kernel,
        grid=(n,),
        in_specs=[qspec] * len(qkv_cols) + cspecs,
        out_specs=cspecs + [pl.BlockSpec((GROUP_WIDTH, n), lambda i: (0, 0))],
        out_shape=[jax.ShapeDtypeStruct(c.shape, F32) for c in windows_t]
                  + [jax.ShapeDtypeStruct((GROUP_WIDTH, n), F32)],
        compiler_params=pltpu.CompilerParams(dimension_semantics=("arbitrary",),
                                             vmem_limit_bytes=_vmem_limit(pipelined, 0, 16 << 20)),
        name="cache_step",
    )(*qkv_cols, *windows_t)
    return res[:-1], res[-1]


def _rope_tables(pos):
    half = HEAD_DIM // 2
    inv = ROPE_THETA ** (-jnp.arange(half, dtype=F32) * (2.0 / HEAD_DIM))
    ang = pos.astype(F32)[:, None] * inv[None, :]
    cos, sin = jnp.cos(ang), jnp.sin(ang)
    reps = V7X_LANES // HEAD_DIM
    return (jnp.tile(jnp.concatenate([cos, cos], axis=1), (1, reps)),
            jnp.tile(jnp.concatenate([-sin, sin], axis=1), (1, reps)))


def _window_to_lanes(w):
    n, width = w.shape[:2]
    return jnp.transpose(w, (0, 2, 3, 4, 1)).reshape(n, 2 * GROUP_WIDTH, width)


def _window_from_lanes(wt):
    n, _, width = wt.shape
    return jnp.transpose(wt.reshape(n, 2, HEADS, HEAD_DIM, width), (0, 4, 1, 2, 3))[None]


def kernel(x_prompt, x_sample, state_pool, cache_kv_w128, cache_kv_w512, cache_kv_w2048, norm_ffn1, ffn1_w_gate, ffn1_w_up, ffn1_w_down, norm_mix, w_in, w_pool_grp, pool_scale, w_pool_br, w_attn_br, w_o, norm_ffn2, ffn2_w_gate, ffn2_w_up, ffn2_w_down, norm_final):
    b, s, d = x_prompt.shape
    n, dec_seq, _ = x_sample.shape
    caches = (cache_kv_w128, cache_kv_w512, cache_kv_w2048)
    assert norm_ffn1.shape[0] == 1 and dec_seq == 1, "single layer, single decode token"
    ts = PROMPT_ROW_TILE
    assert s % ts == 0 and ts >= POOL_CARRY
    for cache, (window, dil) in zip(caches, ATTN_GROUPS):
        assert window // dil == KEYS_PER_QUERY and s % (dil * KEYS_PER_QUERY) == 0 and ts % (16 * dil) == 0
        assert cache.shape[2] == window, "decode path expects full windows of history"

    row = lambda v: v.reshape(1, -1)
    bf = lambda w: w.astype(BF16)
    g1, g2, gm, gf = row(norm_ffn1[0]), row(norm_ffn2[0]), row(norm_mix[0]), row(norm_final)
    f1 = (bf(ffn1_w_gate[0]), bf(ffn1_w_up[0]), bf(ffn1_w_down[0]))
    pscale = row(pool_scale[0])

    cos_p, sin_p = _rope_tables(jnp.arange(s, dtype=jnp.int32))
    later = (ffn2_w_gate[0], ffn2_w_up[0], ffn2_w_down[0], w_in[0],
             w_pool_grp[0].reshape(-1, POOL_GROUP), w_pool_br[0], w_attn_br[0], w_o[0])
    x1, xs1, (f2g, f2u, f2d, win, wgrp, wpbr, wabr, wo) = _ffn_call(
        x_prompt.reshape(b * s, d), x_sample.reshape(n, d), g1, *f1, tm=FFN_ROW_TILE, cast=later)
    f2 = (f2g, f2u, f2d)
    wgrp = wgrp.reshape(w_pool_grp.shape[1:])
    (st0, st1, st2, kvt0, kvt1, kvt2, tail, pp, ga) = _prompt_mix_in_call(
        x1.reshape(b, s, d), gm, win, cos_p, sin_p, wgrp, pscale, wpbr, ts)
    group_planes = []
    for streams, (_, dil) in zip((st0, st1, st2), ATTN_GROUPS):
        group_planes += _stream_attn_call(streams, dil, s)
    pool_prompt = tail[None, :, POOL_CARRY - POOL_STATE:, :]

    cos_s, sin_s = _rope_tables(PAST_LEN + jnp.arange(dec_seq, dtype=jnp.int32))
    state_t = jnp.swapaxes(state_pool[0], 0, 1)
    (sqkv0, sqkv1, sqkv2, u_new, spp, sga) = _sample_mix_in_call(
        xs1, gm, win, state_t, cos_s, sin_s, wgrp, pscale, wpbr)
    new_windows, attn_t = _cache_step_call((sqkv0, sqkv1, sqkv2),
                                           [_window_to_lanes(c[0]) for c in caches])
    pool_sample = jnp.swapaxes(jnp.concatenate([state_t[1:], u_new[None]], axis=0), 0, 1)[None]

    y_prompt, y_sample = _mix_out_call(x1, pp.reshape(b * s, d), ga.reshape(b * s, d), tuple(group_planes),
                                       (xs1, spp, sga, attn_t.T.astype(BF16)), wabr, wo, g2, *f2, gf, tm=ts)
    y_prompt = y_prompt.reshape(b, s, d)
    y_sample = y_sample.reshape(n, dec_seq, d)

    return (y_prompt, y_sample, pool_prompt,
            _window_from_lanes(kvt0), _window_from_lanes(kvt1), _window_from_lanes(kvt2),
            pool_sample, *[_window_from_lanes(w) for w in new_windows])
```

```python
import functools

import jax
import jax.numpy as jnp
from jax import lax
from jax.experimental import pallas as pl
from jax.experimental.pallas import tpu as pltpu

F32 = jnp.float32
BF16 = jnp.bfloat16

PAST_LEN = 16384
POOL_WINDOWS = (2, 4, 8, 16)
POOL_GROUP = 128
POOL_WIDTH = POOL_GROUP * len(POOL_WINDOWS)
POOL_STATE = max(POOL_WINDOWS) - 1
POOL_CARRY = 16
HEAD_DIM = 64
HEADS = 4
GROUP_WIDTH = HEADS * HEAD_DIM
ATTN_GROUPS = ((128, 1), (512, 4), (2048, 16))
KEYS_PER_QUERY = 128
QKV_GROUP_WIDTH = 3 * GROUP_WIDTH
ROPE_THETA = 10000.0
RMS_EPS = 1e-6
NEG_INF = -1e30
ATTN_SCALE = HEAD_DIM ** -0.5
LOG2_E = 1.4426950408889634

V7X_VMEM_BYTES = 64 * 1024 * 1024
V7X_LANES = 128
V7X_MXU_COLS = 256
MAX_FAST_SUBLANE_STRIDE = 4

PROMPT_ROW_TILE = 512
FFN_ROW_TILE = 1024
FFN_CHUNK = 4 * V7X_MXU_COLS


def _vmem_limit(pipelined_bytes, resident_bytes, temp_bytes):
    need = 2 * pipelined_bytes + resident_bytes + temp_bytes
    return int(min(need, V7X_VMEM_BYTES - (4 << 20)))


def _nbytes(shape, dtype):
    n = 1
    for s in shape:
        n *= s
    return n * jnp.dtype(dtype).itemsize


def _resident(shape):
    zeros = (0,) * len(shape)
    return pl.BlockSpec(shape, lambda *_: zeros, pipeline_mode=pl.Buffered(1))


def _rmsnorm(x, g):
    ms = jnp.mean(x * x, axis=-1, keepdims=True)
    return x * lax.rsqrt(ms + RMS_EPS) * g


def _ffn_chunks(width):
    return [(lo, min(lo + FFN_CHUNK, width)) for lo in range(0, width, FFN_CHUNK)]


def _swiglu(xn, wg_ref, wu_ref, wd_ref):
    acc = None
    for lo, hi in _ffn_chunks(wg_ref.shape[1]):
        gate = jnp.dot(xn, wg_ref[:, lo:hi], preferred_element_type=F32)
        up = jnp.dot(xn, wu_ref[:, lo:hi], preferred_element_type=F32)
        hid = (gate * jax.nn.sigmoid(gate) * up).astype(BF16)
        part = jnp.dot(hid, wd_ref[lo:hi, :], preferred_element_type=F32)
        acc = part if acc is None else acc + part
    return acc


def _macaron_ffn(x, g_ref, wg_ref, wu_ref, wd_ref):
    xn = _rmsnorm(x, g_ref[...]).astype(BF16)
    return x + 0.5 * _swiglu(xn, wg_ref, wu_ref, wd_ref)


def _ffn_kernel(x_ref, xs_ref, g_ref, wg_ref, wu_ref, wd_ref, *rest):
    n_cast = (len(rest) - 2) // 2
    o_ref, os_ref = rest[n_cast], rest[n_cast + 1]

    @pl.when(pl.program_id(0) == 0)
    def _():
        os_ref[...] = _macaron_ffn(xs_ref[...], g_ref, wg_ref, wu_ref, wd_ref)

    o_ref[...] = _macaron_ffn(x_ref[...], g_ref, wg_ref, wu_ref, wd_ref)
    for src_ref, dst_ref in zip(rest[:n_cast], rest[n_cast + 2:]):
        dst_ref[...] = src_ref[...].astype(BF16)


BF16_SUBLANE_ROWS = 16


def _ffn_call(x, xs, g, wg, wu, wd, tm, cast=()):
    m, d = x.shape
    f = wg.shape[1]
    steps = m // tm
    row = pl.BlockSpec((tm, d), lambda i: (i, 0))
    dec = pl.BlockSpec(xs.shape, lambda i: (0, 0))
    cast_specs, cast_bytes = [], 0
    for w in cast:
        rows = w.shape[0]
        per_step = next(r for r in range(BF16_SUBLANE_ROWS, rows + 1, BF16_SUBLANE_ROWS)
                        if rows % r == 0 and r * steps >= rows)
        last = rows // per_step - 1
        cast_specs.append(pl.BlockSpec((per_step, w.shape[1]), lambda i, last=last: (jnp.minimum(i, last), 0)))
        cast_bytes += _nbytes((per_step, w.shape[1]), F32) + _nbytes((per_step, w.shape[1]), BF16)
    limit = _vmem_limit(2 * _nbytes((tm, d), F32) + 2 * _nbytes(xs.shape, F32) + cast_bytes,
                        3 * _nbytes((d, f), BF16),
                        4 * _nbytes((tm, FFN_CHUNK), F32) + 2 * _nbytes((tm, d), F32))
    res = pl.pallas_call(
        _ffn_kernel,
        grid=(steps,),
        in_specs=[row, dec, _resident((1, d)), _resident((d, f)), _resident((d, f)), _resident((f, d))]
                 + cast_specs,
        out_specs=[row, dec] + cast_specs,
        out_shape=[jax.ShapeDtypeStruct((m, d), F32), jax.ShapeDtypeStruct(xs.shape, F32)]
                  + [jax.ShapeDtypeStruct(w.shape, BF16) for w in cast],
        compiler_params=pltpu.CompilerParams(dimension_semantics=("arbitrary",), vmem_limit_bytes=limit),
        name="ffn",
    )(x, xs, g, wg, wu, wd, *cast)
    return res[0], res[1], res[2:]


def _rope(t, cos, sin_signed, first_half):
    outs = []
    for c in range(t.shape[1] // V7X_LANES):
        tc = t[:, c * V7X_LANES:(c + 1) * V7X_LANES]
        partner = jnp.where(first_half,
                            pltpu.roll(tc, V7X_LANES - HEAD_DIM // 2, 1),
                            pltpu.roll(tc, HEAD_DIM // 2, 1))
        outs.append(tc * cos + partner * sin_signed)
    return jnp.concatenate(outs, axis=1)


def _first_half_mask():
    lane = lax.broadcasted_iota(jnp.int32, (1, V7X_LANES), 1)
    return (lane % HEAD_DIM) < (HEAD_DIM // 2)


def _qkv_group(h, win_ref, g, cos, sin_signed, first_half, q_scale):
    base = POOL_WIDTH + g * QKV_GROUP_WIDTH
    qkv = jnp.dot(h, win_ref[:, base:base + QKV_GROUP_WIDTH], preferred_element_type=F32)
    q = _rope(qkv[:, :GROUP_WIDTH], cos, sin_signed, first_half) * q_scale
    k = _rope(qkv[:, GROUP_WIDTH:2 * GROUP_WIDTH], cos, sin_signed, first_half)
    return jnp.concatenate([q, k, qkv[:, 2 * GROUP_WIDTH:]], axis=1)


def _pool_branch(window_means_minus_tok, wgrp_ref, pscale_ref, wpbr_ref):
    mixed = [jnp.dot(p.astype(BF16), wgrp_ref[gi], preferred_element_type=F32)
             for gi, p in enumerate(window_means_minus_tok)]
    pool = jnp.concatenate(mixed, axis=1) * pscale_ref[...]
    return jnp.dot(pool.astype(BF16), wpbr_ref[...], preferred_element_type=F32)


def _gates(h, win_ref):
    d = win_ref.shape[0]
    base = POOL_WIDTH + len(ATTN_GROUPS) * QKV_GROUP_WIDTH
    gate_pool = jax.nn.sigmoid(jnp.dot(h, win_ref[:, base:base + d], preferred_element_type=F32))
    gate_attn = jax.nn.sigmoid(jnp.dot(h, win_ref[:, base + d:base + 2 * d], preferred_element_type=F32))
    return gate_pool, gate_attn


def _prompt_mix_in_kernel(x_ref, g_ref, win_ref, cos_ref, sin_ref, wgrp_ref, pscale_ref, wpbr_ref,
                          s0_ref, s1_ref, s2_ref,
                          kvt0_ref, kvt1_ref, kvt2_ref, tail_ref, pp_ref, ga_ref, uext_ref, qkv_ref, mid_ref,
                          *, ts):
    j = pl.program_id(1)

    @pl.when(j == 0)
    def _():
        uext_ref[0:POOL_CARRY, :] = jnp.zeros((POOL_CARRY, POOL_WIDTH), F32)

    h = _rmsnorm(x_ref[0], g_ref[...]).astype(BF16)

    cos = cos_ref[...]
    sin_signed = sin_ref[...]
    first_half = _first_half_mask()
    outs = ((s0_ref, kvt0_ref), (s1_ref, kvt1_ref), (s2_ref, kvt2_ref))
    planes = QKV_GROUP_WIDTH // V7X_LANES
    slot = 0
    for g in sorted(range(len(ATTN_GROUPS)), key=lambda g: -ATTN_GROUPS[g][1]):
        stream_ref, kvt_ref = outs[g]
        dil = ATTN_GROUPS[g][1]
        qkv = _qkv_group(h, win_ref, g, cos, sin_signed, first_half, ATTN_SCALE * LOG2_E)
        if dil == 1:
            stream_ref[0] = qkv.astype(BF16)
        else:
            for c in range(planes):
                qkv_ref[slot + c] = qkv[:, c * V7X_LANES:(c + 1) * V7X_LANES]
            inner = min(dil, MAX_FAST_SUBLANE_STRIDE)
            outer = dil // inner
            assert outer <= MAX_FAST_SUBLANE_STRIDE and outer * inner == dil
            if outer > 1:
                for c in range(planes):
                    for bb in range(inner):
                        mid_ref[c, bb * (ts // inner):(bb + 1) * (ts // inner), :] = (
                            qkv_ref[slot + c, pl.ds(bb, ts // inner, stride=inner), :])
            for r in range(dil):
                if outer > 1:
                    a, bb = r // inner, r % inner
                    rows = jnp.concatenate(
                        [mid_ref[c, pl.ds(bb * (ts // inner) + a, ts // dil, stride=outer), :]
                         for c in range(planes)], axis=1)
                else:
                    rows = jnp.concatenate([qkv_ref[slot + c, pl.ds(r, ts // dil, stride=dil), :]
                                            for c in range(planes)], axis=1)
                stream_ref[0, :, r * QKV_GROUP_WIDTH:(r + 1) * QKV_GROUP_WIDTH] = rows.astype(BF16)
            slot += planes
        keep = kvt_ref.shape[2]
        kvt_ref[0] = qkv[ts - keep:, GROUP_WIDTH:].T

    u = jnp.dot(h, win_ref[:, :POOL_WIDTH], preferred_element_type=F32)
    uext_ref[POOL_CARRY:POOL_CARRY + ts, :] = u
    pos = j * ts + lax.broadcasted_iota(jnp.int32, (ts, 1), 0)
    pooled = []
    for gi, w in enumerate(POOL_WINDOWS):
        lo, hi = gi * POOL_GROUP, (gi + 1) * POOL_GROUP
        tok = u[:, lo:hi]
        total = tok
        for back in range(1, w):
            total = total + uext_ref[POOL_CARRY - back:POOL_CARRY - back + ts, lo:hi]
        cnt = jnp.minimum(pos + 1, w).astype(F32)
        pooled.append(total / cnt - tok)
    pool_br = _pool_branch(pooled, wgrp_ref, pscale_ref, wpbr_ref)
    uext_ref[0:POOL_CARRY, :] = u[ts - POOL_CARRY:, :]
    tail_ref[0] = u[ts - POOL_CARRY:, :]
    gate_pool, gate_attn = _gates(h, win_ref)
    pp_ref[0] = gate_pool * pool_br
    ga_ref[0] = gate_attn


def _prompt_mix_in_call(x, g, win, cos, sin, wgrp, pscale, wpbr, ts):
    b, s, d = x.shape
    seq_tiles = s // ts
    zw = win.shape[1]

    def tile(width):
        return pl.BlockSpec((1, ts, width), lambda bi, j: (bi, j, 0))

    stream_specs, stream_shapes, kvt_specs, kvt_shapes = [], [], [], []
    for window, dil in ATTN_GROUPS:
        stream_specs.append(pl.BlockSpec((1, ts // dil, dil * QKV_GROUP_WIDTH), lambda bi, j: (bi, j, 0)))
        stream_shapes.append(jax.ShapeDtypeStruct((b, s // dil, dil * QKV_GROUP_WIDTH), BF16))
        keep = min(window, s)
        if keep >= ts:
            first = seq_tiles - keep // ts
            kvt_specs.append(pl.BlockSpec((1, 2 * GROUP_WIDTH, ts),
                                          lambda bi, j, first=first: (bi, 0, jnp.maximum(j - first, 0))))
        else:
            kvt_specs.append(pl.BlockSpec((1, 2 * GROUP_WIDTH, keep), lambda bi, j: (bi, 0, 0)))
        kvt_shapes.append(jax.ShapeDtypeStruct((b, 2 * GROUP_WIDTH, keep), F32))
    n_g = len(ATTN_GROUPS)
    out_specs = (stream_specs + kvt_specs
                 + [pl.BlockSpec((1, POOL_CARRY, POOL_WIDTH), lambda bi, j: (bi, 0, 0)), tile(d), tile(d)])
    out_shape = (stream_shapes + kvt_shapes
                 + [jax.ShapeDtypeStruct((b, POOL_CARRY, POOL_WIDTH), F32),
                    jax.ShapeDtypeStruct((b, s, d), F32), jax.ShapeDtypeStruct((b, s, d), F32)])
    table = pl.BlockSpec((ts, V7X_LANES), lambda bi, j: (j, 0))
    pipelined = (3 * _nbytes((ts, d), F32) + n_g * _nbytes((ts, QKV_GROUP_WIDTH), BF16)
                 + n_g * _nbytes((ts, 2 * GROUP_WIDTH), F32) + 2 * _nbytes((ts, V7X_LANES), F32))
    resident = _nbytes((d, zw), BF16) + _nbytes(wpbr.shape, BF16) + _nbytes(wgrp.shape, BF16)
    n_dilated = sum(1 for _, dil in ATTN_GROUPS if dil != 1)
    scratch = [pltpu.VMEM((ts + POOL_CARRY, POOL_WIDTH), F32),
               pltpu.VMEM((n_dilated * QKV_GROUP_WIDTH // V7X_LANES, ts, V7X_LANES), F32),
               pltpu.VMEM((QKV_GROUP_WIDTH // V7X_LANES, ts, V7X_LANES), F32)]
    temps = (_nbytes((ts + POOL_CARRY, POOL_WIDTH), F32) + (n_dilated + 1) * _nbytes((ts, QKV_GROUP_WIDTH), F32)
             + 6 * _nbytes((ts, d), F32))
    return pl.pallas_call(
        functools.partial(_prompt_mix_in_kernel, ts=ts),
        grid=(b, seq_tiles),
        in_specs=[tile(d), _resident((1, d)), _resident((d, zw)), table, table,
                  _resident(wgrp.shape), _resident((1, POOL_WIDTH)), _resident(wpbr.shape)],
        out_specs=out_specs,
        out_shape=out_shape,
        scratch_shapes=scratch,
        compiler_params=pltpu.CompilerParams(dimension_semantics=("arbitrary", "arbitrary"),
                                             vmem_limit_bytes=_vmem_limit(pipelined, resident, temps)),
        name="prompt_mix_in",
    )(x, g, win, cos, sin, wgrp, pscale, wpbr)


def _head_lane_masks():
    lane = lax.broadcasted_iota(jnp.int32, (1, GROUP_WIDTH), 1)
    return [(lane // HEAD_DIM) == hd for hd in range(HEADS)]


def _band_scores(q, kt, head_ones):
    q_heads = jnp.concatenate([q * one for one in head_ones], axis=0)
    return jnp.dot(q_heads, kt, preferred_element_type=F32)


def _band_softmax(sc, bias_ref, keys):
    blk = sc.shape[0] // HEADS
    heads = []
    for hd in range(HEADS):
        s_h = sc[hd * blk:(hd + 1) * blk, :] + bias_ref[:, :keys]
        mx = jnp.max(s_h, axis=1, keepdims=True)
        e = jnp.exp2(s_h - mx)
        heads.append((e.astype(BF16), mx, jnp.sum(e, axis=1, keepdims=True)))
    return heads


def _band_outputs(heads, v, low_head):
    outs, lses = [], []
    for hd, (e, mx, den) in enumerate(heads):
        tile = slice((hd * HEAD_DIM // V7X_LANES) * V7X_LANES, (hd * HEAD_DIM // V7X_LANES + 1) * V7X_LANES)
        o_un = jnp.dot(e, v[:, tile], preferred_element_type=F32)
        outs.append(o_un * (1.0 / den))
        lses.append(jnp.broadcast_to(mx + jnp.log2(den), (e.shape[0], V7X_LANES)))
    pairs = range(0, HEADS, V7X_LANES // HEAD_DIM)
    return ([jnp.where(low_head, outs[hd], outs[hd + 1]) for hd in pairs],
            [jnp.where(low_head, lses[hd], lses[hd + 1]) for hd in pairs])


ATTN_PLANES = GROUP_WIDTH // V7X_LANES
ATTN_BLOCKS_SIDE_BY_SIDE = {1: 1, 4: 3, 16: 2}


def _stream_attn_kernel(s_ref, o_ref, l_ref, first_bias_ref, band_bias_ref, *, dil, seq):
    blk = KEYS_PER_QUERY

    @pl.when(pl.program_id(0) == 0)
    def _():
        qi = lax.broadcasted_iota(jnp.int32, (blk, 1), 0)
        first = lax.broadcasted_iota(jnp.int32, (1, blk), 1) <= qi
        dist = qi + blk - lax.broadcasted_iota(jnp.int32, (1, 2 * blk), 1)
        first_bias_ref[...] = jnp.where(first, 0.0, NEG_INF)
        band_bias_ref[...] = jnp.where((dist >= 0) & (dist <= blk), 0.0, NEG_INF)

    head_ones = [jnp.where(m, 1.0, 0.0).astype(BF16) for m in _head_lane_masks()]
    low_head = lax.broadcasted_iota(jnp.int32, (1, V7X_LANES), 1) < HEAD_DIM
    blocks = [(r, i) for r in range(dil) for i in range(seq // (dil * blk))]
    side = ATTN_BLOCKS_SIDE_BY_SIDE.get(dil, 1)
    for first in range(0, len(blocks), side):
        operands = []
        for r, i in blocks[first:first + side]:
            base = r * QKV_GROUP_WIDTH
            krows = slice(max(i - 1, 0) * blk, (i + 1) * blk)
            operands.append((s_ref[0, i * blk:(i + 1) * blk, base:base + GROUP_WIDTH],
                             s_ref[0, krows, base + GROUP_WIDTH:base + 2 * GROUP_WIDTH].T,
                             s_ref[0, krows, base + 2 * GROUP_WIDTH:base + 3 * GROUP_WIDTH]))
        scores = [_band_scores(q, kt, head_ones) for q, kt, _ in operands]
        softmaxes = [_band_softmax(sc, first_bias_ref if i == 0 else band_bias_ref, v.shape[0])
                     for sc, (_, _, v), (_, i) in zip(scores, operands, blocks[first:first + side])]
        for heads, (_, _, v), (r, i) in zip(softmaxes, operands, blocks[first:first + side]):
            outs, lses = _band_outputs(heads, v, low_head)
            for c in range(ATTN_PLANES):
                o_ref[0, c, r, i * blk:(i + 1) * blk, :] = outs[c]
                l_ref[0, c, r, i * blk:(i + 1) * blk, :] = lses[c]


def _merge_groups(outs, lses, exp_fn):
    mx = functools.reduce(jnp.maximum, lses)
    ws = [exp_fn(l - mx) for l in lses]
    num = functools.reduce(lambda a, b: a + b, [w * o for w, o in zip(ws, outs)])
    return num / functools.reduce(lambda a, b: a + b, ws)


def _stream_attn_call(streams, dil, seq):
    b = streams.shape[0]

    def whole(a):
        zeros = (0,) * (len(a.shape) - 1)
        return pl.BlockSpec((1,) + tuple(a.shape[1:]), lambda bi: (bi,) + zeros)

    planes = jax.ShapeDtypeStruct((b, ATTN_PLANES, dil, seq // dil, V7X_LANES), F32)
    pipelined = _nbytes((seq, QKV_GROUP_WIDTH), BF16) + 2 * _nbytes((seq, GROUP_WIDTH), F32)
    scratch = [pltpu.VMEM((KEYS_PER_QUERY, KEYS_PER_QUERY), F32),
               pltpu.VMEM((KEYS_PER_QUERY, 2 * KEYS_PER_QUERY), F32)]
    scratch_bytes = 3 * _nbytes((KEYS_PER_QUERY, KEYS_PER_QUERY), F32)
    return pl.pallas_call(
        functools.partial(_stream_attn_kernel, dil=dil, seq=seq),
        grid=(b,),
        in_specs=[whole(streams)],
        out_specs=[whole(planes), whole(planes)],
        out_shape=[planes, planes],
        scratch_shapes=scratch,
        compiler_params=pltpu.CompilerParams(dimension_semantics=("arbitrary",),
                                             vmem_limit_bytes=_vmem_limit(pipelined, scratch_bytes, 16 << 20)),
        name=f"stream_attn_d{dil}",
    )(streams)


def _token_order(ref, plane, stage_ref, slot):
    dil, rows = ref.shape[2], ref.shape[3]
    if dil == 1:
        return ref[0, plane, 0]
    if dil <= MAX_FAST_SUBLANE_STRIDE:
        for r in range(dil):
            stage_ref[slot, pl.ds(r, rows, stride=dil), :] = ref[0, plane, r]
        return stage_ref[slot]
    inner = MAX_FAST_SUBLANE_STRIDE
    outer = dil // inner
    assert outer <= MAX_FAST_SUBLANE_STRIDE and outer * inner == dil
    tmp = stage_ref.shape[0] - 1
    mid = rows * outer
    for b in range(inner):
        for a in range(outer):
            stage_ref[tmp, pl.ds(b * mid + a, rows, stride=outer), :] = ref[0, plane, inner * a + b]
    for b in range(inner):
        stage_ref[slot, pl.ds(b, mid, stride=inner), :] = stage_ref[tmp, b * mid:(b + 1) * mid, :]
    return stage_ref[slot]


def _mix_out_rows(x, pp, ga, attn, wabr_ref, wo_ref, g2_ref, wg_ref, wu_ref, wd_ref, gf_ref):
    attn_br = jnp.dot(attn, wabr_ref[...], preferred_element_type=F32)
    merged = pp + ga * attn_br
    x = x + jnp.dot(merged.astype(BF16), wo_ref[...], preferred_element_type=F32)
    x = _macaron_ffn(x, g2_ref, wg_ref, wu_ref, wd_ref)
    return _rmsnorm(x, gf_ref[...])


def _mix_out_kernel(x_ref, pp_ref, ga_ref, *rest, n_groups):
    group_refs, rest = rest[:2 * n_groups], rest[2 * n_groups:]
    weights, rest = rest[:7], rest[7:]
    xs_ref, spp_ref, sga_ref, sattn_ref, y_ref, ys_ref, stage_ref = rest

    @pl.when(pl.program_id(0) == 0)
    def _():
        ys_ref[...] = _mix_out_rows(xs_ref[...], spp_ref[...], sga_ref[...], sattn_ref[...], *weights)

    tiles = []
    for c in range(ATTN_PLANES):
        outs = [_token_order(group_refs[2 * g], c, stage_ref, 2 * g) for g in range(n_groups)]
        lses = [_token_order(group_refs[2 * g + 1], c, stage_ref, 2 * g + 1) for g in range(n_groups)]
        tiles.append(_merge_groups(outs, lses, jnp.exp2).astype(BF16))
    attn = jnp.concatenate(tiles, axis=1)
    y_ref[...] = _mix_out_rows(x_ref[...], pp_ref[...], ga_ref[...], attn, *weights)


def _mix_out_call(x, pp, ga, group_planes, decode, wabr, wo, g2, wg, wu, wd, gf, tm):
    m, d = x.shape
    f = wg.shape[1]
    n_tiles = m // tm
    row = pl.BlockSpec((tm, d), lambda i: (i, 0))
    plane_specs = []
    for p in group_planes:
        dil, seq_rows = p.shape[2], p.shape[3]
        tiles = seq_rows * dil // tm
        plane_specs.append(pl.BlockSpec((1, ATTN_PLANES, dil, tm // dil, V7X_LANES),
                                        lambda i, tiles=tiles: (i // tiles, 0, 0, i % tiles, 0)))

    def whole(a):
        return pl.BlockSpec(a.shape, lambda i: (0, 0))

    n_dec = decode[0].shape[0]
    pipelined = (4 * _nbytes((tm, d), F32) + len(group_planes) * _nbytes((tm, GROUP_WIDTH), F32)
                 + 4 * _nbytes((n_dec, d), F32))
    scratch = [pltpu.VMEM((len(group_planes) + 1, tm, V7X_LANES), F32)]
    resident = 3 * _nbytes((d, f), BF16) + _nbytes((d, d), BF16) + _nbytes((GROUP_WIDTH, d), BF16)
    temps = 4 * _nbytes((tm, FFN_CHUNK), F32) + 4 * _nbytes((tm, d), F32)
    return pl.pallas_call(
        functools.partial(_mix_out_kernel, n_groups=len(group_planes) // 2),
        grid=(n_tiles,),
        in_specs=[row, row, row] + plane_specs
                 + [_resident((GROUP_WIDTH, d)), _resident((d, d)), _resident((1, d)),
                    _resident((d, f)), _resident((d, f)), _resident((f, d)), _resident((1, d))]
                 + [whole(a) for a in decode],
        out_specs=[row, pl.BlockSpec((n_dec, d), lambda i: (0, 0))],
        out_shape=[jax.ShapeDtypeStruct((m, d), F32), jax.ShapeDtypeStruct((n_dec, d), F32)],
        scratch_shapes=scratch,
        compiler_params=pltpu.CompilerParams(dimension_semantics=("arbitrary",),
                                             vmem_limit_bytes=_vmem_limit(pipelined, resident, temps)),
        name="mix_out",
    )(x, pp, ga, *group_planes, wabr, wo, g2, wg, wu, wd, gf, *decode)


def _sample_mix_in_kernel(x_ref, g_ref, win_ref, state_ref, cos_ref, sin_ref, wgrp_ref, pscale_ref, wpbr_ref,
                          qkv0_ref, qkv1_ref, qkv2_ref, u_ref, pp_ref, ga_ref):
    h = _rmsnorm(x_ref[...], g_ref[...]).astype(BF16)
    u = jnp.dot(h, win_ref[:, :POOL_WIDTH], preferred_element_type=F32)
    u_ref[...] = u
    pooled = []
    for gi, w in enumerate(POOL_WINDOWS):
        lo, hi = gi * POOL_GROUP, (gi + 1) * POOL_GROUP
        tok = u[:, lo:hi]
        total = tok
        for back in range(1, w):
            total = total + state_ref[POOL_STATE - back, :, lo:hi]
        pooled.append(total / float(min(PAST_LEN + 1, w)) - tok)
    pool_br = _pool_branch(pooled, wgrp_ref, pscale_ref, wpbr_ref)
    gate_pool, gate_attn = _gates(h, win_ref)
    pp_ref[...] = gate_pool * pool_br
    ga_ref[...] = gate_attn
    first_half = _first_half_mask()
    n = x_ref.shape[0]
    for g, qkv_ref in enumerate((qkv0_ref, qkv1_ref, qkv2_ref)):
        qkv = _qkv_group(h, win_ref, g, cos_ref[...], sin_ref[...], first_half, ATTN_SCALE)
        padded = jnp.concatenate([qkv, jnp.zeros((V7X_LANES - n, QKV_GROUP_WIDTH), F32)], axis=0)
        qkv_ref[...] = jnp.transpose(padded)[:, :n]


def _sample_mix_in_call(x, g, win, state_t, cos, sin, wgrp, pscale, wpbr):
    n, d = x.shape
    n_g = len(ATTN_GROUPS)

    def full(shape):
        zeros = (0,) * len(shape)
        return pl.BlockSpec(shape, lambda i: zeros)

    assert n <= V7X_LANES
    out_shape = ([jax.ShapeDtypeStruct((QKV_GROUP_WIDTH, n), F32)] * n_g
                 + [jax.ShapeDtypeStruct((n, POOL_WIDTH), F32),
                    jax.ShapeDtypeStruct((n, d), F32), jax.ShapeDtypeStruct((n, d), F32)])
    ins = (x, g, win, state_t, cos, sin, wgrp, pscale, wpbr)
    resident = sum(_nbytes(a.shape, a.dtype) for a in ins)
    return pl.pallas_call(
        _sample_mix_in_kernel,
        grid=(1,),
        in_specs=[full(a.shape) for a in ins],
        out_specs=[full(o.shape) for o in out_shape],
        out_shape=out_shape,
        compiler_params=pltpu.CompilerParams(dimension_semantics=("arbitrary",),
                                             vmem_limit_bytes=_vmem_limit(resident, 0, 8 << 20)),
        name="sample_mix_in",
    )(*ins)


def _head_sums(x):
    return jnp.concatenate([jnp.sum(x[hd * HEAD_DIM:(hd + 1) * HEAD_DIM], axis=0, keepdims=True)
                            for hd in range(HEADS)], axis=0)


def _head_expand(y):
    return jnp.concatenate([jnp.broadcast_to(y[hd:hd + 1], (HEAD_DIM, y.shape[1])) for hd in range(HEADS)],
                           axis=0)


def _cache_step_kernel(qkv0_ref, qkv1_ref, qkv2_ref, c0_ref, c1_ref, c2_ref,
                       n0_ref, n1_ref, n2_ref, attn_ref):
    n = pl.program_id(0)

    @pl.when(n == 0)
    def _():
        attn_ref[...] = jnp.zeros(attn_ref.shape, F32)

    groups = ((qkv0_ref, c0_ref, n0_ref), (qkv1_ref, c1_ref, n1_ref), (qkv2_ref, c2_ref, n2_ref))
    request = lax.broadcasted_iota(jnp.int32, qkv0_ref.shape, 1) == n
    cols = [jnp.sum(jnp.where(request, qkv_ref[...], 0.0), axis=1, keepdims=True) for qkv_ref, _, _ in groups]
    for qkv_col, (_, c_ref, new_ref), (window, _) in zip(cols, groups, ATTN_GROUPS):
        lane = lax.broadcasted_iota(jnp.int32, (1, window), 1)
        new_ref[0] = jnp.where(lane == window - 1, qkv_col[GROUP_WIDTH:], pltpu.roll(c_ref[0], window - 1, 1))
    outs, lses = [], []
    for qkv_col, (_, c_ref, _), (window, dil) in zip(cols, groups, ATTN_GROUPS):
        q = qkv_col[:GROUP_WIDTH]
        k_new = qkv_col[GROUP_WIDTH:2 * GROUP_WIDTH]
        v_new = qkv_col[2 * GROUP_WIDTH:]
        kt = c_ref[0, :GROUP_WIDTH, :]
        vt = c_ref[0, GROUP_WIDTH:, :]
        lane = lax.broadcasted_iota(jnp.int32, (1, window), 1)
        sc = jnp.where(lane % dil == 0, _head_sums(kt * q), NEG_INF)
        sc_new = _head_sums(k_new * q)
        mx = jnp.maximum(jnp.max(sc, axis=1, keepdims=True), sc_new)
        e = jnp.exp(sc - mx)
        e_new = jnp.exp(sc_new - mx)
        den = jnp.sum(e, axis=1, keepdims=True) + e_new
        o_un = jnp.sum(vt * _head_expand(e), axis=1, keepdims=True) + _head_expand(e_new) * v_new
        outs.append(o_un / _head_expand(den))
        lses.append(_head_expand(mx + jnp.log(den)))
    attn_col = _merge_groups(outs, lses, jnp.exp)
    col = lax.broadcasted_iota(jnp.int32, attn_ref.shape, 1)
    attn_ref[...] = jnp.where(col == n, attn_col, attn_ref[...])


def _cache_step_call(qkv_cols, windows_t):
    n = qkv_cols[0].shape[1]
    qspec = pl.BlockSpec((QKV_GROUP_WIDTH, n), lambda i: (0, 0))
    cspecs = [pl.BlockSpec((1,) + c.shape[1:], lambda i: (i, 0, 0)) for c in windows_t]
    pipelined = 2 * sum(_nbytes(c.shape[1:], F32) for c in windows_t) + 3 * _nbytes((QKV_GROUP_WIDTH, V7X_LANES), F32)
    res = pl.pallas_call(
        _cache_step_kernel,
        grid=(n,),
        in_specs=[qspec] * len(qkv_cols) + cspecs,
        out_specs=cspecs + [pl.BlockSpec((GROUP_WIDTH, n), lambda i: (0, 0))],
        out_shape=[jax.ShapeDtypeStruct(c.shape, F32) for c in windows_t]
                  + [jax.ShapeDtypeStruct((GROUP_WIDTH, n), F32)],
        compiler_params=pltpu.CompilerParams(dimension_semantics=("arbitrary",),
                                             vmem_limit_bytes=_vmem_limit(pipelined, 0, 16 << 20)),
        name="cache_step",
    )(*qkv_cols, *windows_t)
    return res[:-1], res[-1]


def _rope_tables(pos):
    half = HEAD_DIM // 2
    inv = ROPE_THETA ** (-jnp.arange(half, dtype=F32) * (2.0 / HEAD_DIM))
    ang = pos.astype(F32)[:, None] * inv[None, :]
    cos, sin = jnp.cos(ang), jnp.sin(ang)
    reps = V7X_LANES // HEAD_DIM
    return (jnp.tile(jnp.concatenate([cos, cos], axis=1), (1, reps)),
            jnp.tile(jnp.concatenate([-sin, sin], axis=1), (1, reps)))


def _window_to_lanes(w):
    n, width = w.shape[:2]
    return jnp.transpose(w, (0, 2, 3, 4, 1)).reshape(n, 2 * GROUP_WIDTH, width)


def _window_from_lanes(wt):
    n, _, width = wt.shape
    return jnp.transpose(wt.reshape(n, 2, HEADS, HEAD_DIM, width), (0, 4, 1, 2, 3))[None]


def kernel(x_prompt, x_sample, state_pool, cache_kv_w128, cache_kv_w512, cache_kv_w2048, norm_ffn1, ffn1_w_gate, ffn1_w_up, ffn1_w_down, norm_mix, w_in, w_pool_grp, pool_scale, w_pool_br, w_attn_br, w_o, norm_ffn2, ffn2_w_gate, ffn2_w_up, ffn2_w_down, norm_final):
    b, s, d = x_prompt.shape
    n, dec_seq, _ = x_sample.shape
    caches = (cache_kv_w128, cache_kv_w512, cache_kv_w2048)
    assert norm_ffn1.shape[0] == 1 and dec_seq == 1, "single layer, single decode token"
    ts = PROMPT_ROW_TILE
    assert s % ts == 0 and ts >= POOL_CARRY
    for cache, (window, dil) in zip(caches, ATTN_GROUPS):
        assert window // dil == KEYS_PER_QUERY and s % (dil * KEYS_PER_QUERY) == 0 and ts % (16 * dil) == 0
        assert cache.shape[2] == window, "decode path expects full windows of history"

    row = lambda v: v.reshape(1, -1)
    bf = lambda w: w.astype(BF16)
    g1, g2, gm, gf = row(norm_ffn1[0]), row(norm_ffn2[0]), row(norm_mix[0]), row(norm_final)
    f1 = (bf(ffn1_w_gate[0]), bf(ffn1_w_up[0]), bf(ffn1_w_down[0]))
    pscale = row(pool_scale[0])

    cos_p, sin_p = _rope_tables(jnp.arange(s, dtype=jnp.int32))
    later = (ffn2_w_gate[0], ffn2_w_up[0], ffn2_w_down[0], w_in[0],
             w_pool_grp[0].reshape(-1, POOL_GROUP), w_pool_br[0], w_attn_br[0], w_o[0])
    x1, xs1, (f2g, f2u, f2d, win, wgrp, wpbr, wabr, wo) = _ffn_call(
        x_prompt.reshape(b * s, d), x_sample.reshape(n, d), g1, *f1, tm=FFN_ROW_TILE, cast=later)
    f2 = (f2g, f2u, f2d)
    wgrp = wgrp.reshape(w_pool_grp.shape[1:])
    (st0, st1, st2, kvt0, kvt1, kvt2, tail, pp, ga) = _prompt_mix_in_call(
        x1.reshape(b, s, d), gm, win, cos_p, sin_p, wgrp, pscale, wpbr, ts)
    group_planes = []
    for streams, (_, dil) in zip((st0, st1, st2), ATTN_GROUPS):
        group_planes += _stream_attn_call(streams, dil, s)
    pool_prompt = tail[None, :, POOL_CARRY - POOL_STATE:, :]

    cos_s, sin_s = _rope_tables(PAST_LEN + jnp.arange(dec_seq, dtype=jnp.int32))
    state_t = jnp.swapaxes(state_pool[0], 0, 1)
    (sqkv0, sqkv1, sqkv2, u_new, spp, sga) = _sample_mix_in_call(
        xs1, gm, win, state_t, cos_s, sin_s, wgrp, pscale, wpbr)
    new_windows, attn_t = _cache_step_call((sqkv0, sqkv1, sqkv2),
                                           [_window_to_lanes(c[0]) for c in caches])
    pool_sample = jnp.swapaxes(jnp.concatenate([state_t[1:], u_new[None]], axis=0), 0, 1)[None]

    y_prompt, y_sample = _mix_out_call(x1, pp.reshape(b * s, d), ga.reshape(b * s, d), tuple(group_planes),
                                       (xs1, spp, sga, attn_t.T.astype(BF16)), wabr, wo, g2, *f2, gf, tm=ts)
    y_prompt = y_prompt.reshape(b, s, d)
    y_sample = y_sample.reshape(n, dec_seq, d)

    return (y_prompt, y_sample, pool_prompt,
            _window_from_lanes(kvt0), _window_from_lanes(kvt1), _window_from_lanes(kvt2),
            pool_sample, *[_window_from_lanes(w) for w in new_windows])
```

```python
import functools

import jax
import jax.numpy as jnp
from jax import lax
from jax.experimental import pallas as pl
from jax.experimental.pallas import tpu as pltpu

F32 = jnp.float32
BF16 = jnp.bfloat16

PAST_LEN = 16384
POOL_WINDOWS = (2, 4, 8, 16)
POOL_GROUP = 128
POOL_WIDTH = POOL_GROUP * len(POOL_WINDOWS)
POOL_STATE = max(POOL_WINDOWS) - 1
POOL_CARRY = 16
HEAD_DIM = 64
HEADS = 4
GROUP_WIDTH = HEADS * HEAD_DIM
ATTN_GROUPS = ((128, 1), (512, 4), (2048, 16))
KEYS_PER_QUERY = 128
QKV_GROUP_WIDTH = 3 * GROUP_WIDTH
ROPE_THETA = 10000.0
RMS_EPS = 1e-6
NEG_INF = -1e30
ATTN_SCALE = HEAD_DIM ** -0.5
LOG2_E = 1.4426950408889634

V7X_VMEM_BYTES = 64 * 1024 * 1024
VMEM_LEFT_TO_COMPILER = 4 * 1024 * 1024
V7X_LANES = 128
V7X_MXU_COLS = 256
MAX_FAST_SUBLANE_STRIDE = 4

PROMPT_ROW_TILE = 512
FFN_ROW_TILE = 1024
FFN_CHUNK = 4 * V7X_MXU_COLS
CACHE_STEP_REQUESTS = 2


def _vmem_limit(pipelined_bytes, resident_bytes, temp_bytes):
    need = 2 * pipelined_bytes + resident_bytes + temp_bytes
    return int(min(need, V7X_VMEM_BYTES - VMEM_LEFT_TO_COMPILER))


def _nbytes(shape, dtype):
    n = 1
    for s in shape:
        n *= s
    return n * jnp.dtype(dtype).itemsize


def _resident(shape):
    zeros = (0,) * len(shape)
    return pl.BlockSpec(shape, lambda *_: zeros, pipeline_mode=pl.Buffered(1))


def _rmsnorm(x, g):
    ms = jnp.mean(x * x, axis=-1, keepdims=True)
    return x * lax.rsqrt(ms + RMS_EPS) * g


def _ffn_chunks(width):
    return [(lo, min(lo + FFN_CHUNK, width)) for lo in range(0, width, FFN_CHUNK)]


def _swiglu(xn, wg_ref, wu_ref, wd_ref):
    acc = None
    for lo, hi in _ffn_chunks(wg_ref.shape[1]):
        gate = jnp.dot(xn, wg_ref[:, lo:hi], preferred_element_type=F32)
        up = jnp.dot(xn, wu_ref[:, lo:hi], preferred_element_type=F32)
        hid = (gate * jax.nn.sigmoid(gate) * up).astype(BF16)
        part = jnp.dot(hid, wd_ref[lo:hi, :], preferred_element_type=F32)
        acc = part if acc is None else acc + part
    return acc


def _macaron_ffn(x, g_ref, wg_ref, wu_ref, wd_ref):
    xn = _rmsnorm(x, g_ref[...]).astype(BF16)
    return x + 0.5 * _swiglu(xn, wg_ref, wu_ref, wd_ref)


def _ffn_kernel(x_ref, xs_ref, g_ref, wg_ref, wu_ref, wd_ref, *rest):
    n_cast = (len(rest) - 2) // 2
    o_ref, os_ref = rest[n_cast], rest[n_cast + 1]

    @pl.when(pl.program_id(0) == 0)
    def _():
        os_ref[...] = _macaron_ffn(xs_ref[...], g_ref, wg_ref, wu_ref, wd_ref)

    o_ref[...] = _macaron_ffn(x_ref[...], g_ref, wg_ref, wu_ref, wd_ref)
    for src_ref, dst_ref in zip(rest[:n_cast], rest[n_cast + 2:]):
        dst_ref[...] = src_ref[...].astype(BF16)


BF16_SUBLANE_ROWS = 16


def _ffn_call(x, xs, g, wg, wu, wd, tm, cast=()):
    m, d = x.shape
    f = wg.shape[1]
    steps = m // tm
    row = pl.BlockSpec((tm, d), lambda i: (i, 0))
    dec = pl.BlockSpec(xs.shape, lambda i: (0, 0))
    cast_specs, cast_bytes = [], 0
    for w in cast:
        rows = w.shape[0]
        per_step = next(r for r in range(BF16_SUBLANE_ROWS, rows + 1, BF16_SUBLANE_ROWS)
                        if rows % r == 0 and r * steps >= rows)
        last = rows // per_step - 1
        cast_specs.append(pl.BlockSpec((per_step, w.shape[1]), lambda i, last=last: (jnp.minimum(i, last), 0)))
        cast_bytes += _nbytes((per_step, w.shape[1]), F32) + _nbytes((per_step, w.shape[1]), BF16)
    limit = _vmem_limit(2 * _nbytes((tm, d), F32) + 2 * _nbytes(xs.shape, F32) + cast_bytes,
                        3 * _nbytes((d, f), BF16),
                        4 * _nbytes((tm, FFN_CHUNK), F32) + 2 * _nbytes((tm, d), F32))
    res = pl.pallas_call(
        _ffn_kernel,
        grid=(steps,),
        in_specs=[row, dec, _resident((1, d)), _resident((d, f)), _resident((d, f)), _resident((f, d))]
                 + cast_specs,
        out_specs=[row, dec] + cast_specs,
        out_shape=[jax.ShapeDtypeStruct((m, d), F32), jax.ShapeDtypeStruct(xs.shape, F32)]
                  + [jax.ShapeDtypeStruct(w.shape, BF16) for w in cast],
        compiler_params=pltpu.CompilerParams(dimension_semantics=("arbitrary",), vmem_limit_bytes=limit),
        name="ffn",
    )(x, xs, g, wg, wu, wd, *cast)
    return res[0], res[1], res[2:]


def _rope(t, cos, sin_signed, first_half):
    outs = []
    for c in range(t.shape[1] // V7X_LANES):
        tc = t[:, c * V7X_LANES:(c + 1) * V7X_LANES]
        partner = jnp.where(first_half,
                            pltpu.roll(tc, V7X_LANES - HEAD_DIM // 2, 1),
                            pltpu.roll(tc, HEAD_DIM // 2, 1))
        outs.append(tc * cos + partner * sin_signed)
    return jnp.concatenate(outs, axis=1)


def _first_half_mask():
    lane = lax.broadcasted_iota(jnp.int32, (1, V7X_LANES), 1)
    return (lane % HEAD_DIM) < (HEAD_DIM // 2)


def _qkv_group(h, win_ref, g, cos, sin_signed, first_half, q_scale):
    base = POOL_WIDTH + g * QKV_GROUP_WIDTH
    qkv = jnp.dot(h, win_ref[:, base:base + QKV_GROUP_WIDTH], preferred_element_type=F32)
    q = _rope(qkv[:, :GROUP_WIDTH], cos, sin_signed, first_half) * q_scale
    k = _rope(qkv[:, GROUP_WIDTH:2 * GROUP_WIDTH], cos, sin_signed, first_half)
    return jnp.concatenate([q, k, qkv[:, 2 * GROUP_WIDTH:]], axis=1)


def _pool_branch(window_means_minus_tok, wgrp_ref, pscale_ref, wpbr_ref):
    mixed = [jnp.dot(p.astype(BF16), wgrp_ref[gi], preferred_element_type=F32)
             for gi, p in enumerate(window_means_minus_tok)]
    pool = jnp.concatenate(mixed, axis=1) * pscale_ref[...]
    return jnp.dot(pool.astype(BF16), wpbr_ref[...], preferred_element_type=F32)


def _gates(h, win_ref):
    d = win_ref.shape[0]
    base = POOL_WIDTH + len(ATTN_GROUPS) * QKV_GROUP_WIDTH
    gate_pool = jax.nn.sigmoid(jnp.dot(h, win_ref[:, base:base + d], preferred_element_type=F32))
    gate_attn = jax.nn.sigmoid(jnp.dot(h, win_ref[:, base + d:base + 2 * d], preferred_element_type=F32))
    return gate_pool, gate_attn


def _prompt_mix_in_kernel(x_ref, g_ref, win_ref, cos_ref, sin_ref, wgrp_ref, pscale_ref, wpbr_ref,
                          s0_ref, s1_ref, s2_ref,
                          kvt0_ref, kvt1_ref, kvt2_ref, tail_ref, pp_ref, ga_ref, uext_ref, qkv_ref, mid_ref,
                          *, ts):
    j = pl.program_id(1)

    @pl.when(j == 0)
    def _():
        uext_ref[0:POOL_CARRY, :] = jnp.zeros((POOL_CARRY, POOL_WIDTH), F32)

    h = _rmsnorm(x_ref[0], g_ref[...]).astype(BF16)

    cos = cos_ref[...]
    sin_signed = sin_ref[...]
    first_half = _first_half_mask()
    outs = ((s0_ref, kvt0_ref), (s1_ref, kvt1_ref), (s2_ref, kvt2_ref))
    planes = QKV_GROUP_WIDTH // V7X_LANES
    slot = 0
    for g in sorted(range(len(ATTN_GROUPS)), key=lambda g: -ATTN_GROUPS[g][1]):
        stream_ref, kvt_ref = outs[g]
        dil = ATTN_GROUPS[g][1]
        qkv = _qkv_group(h, win_ref, g, cos, sin_signed, first_half, ATTN_SCALE * LOG2_E)
        if dil == 1:
            stream_ref[0] = qkv.astype(BF16)
        else:
            for c in range(planes):
                qkv_ref[slot + c] = qkv[:, c * V7X_LANES:(c + 1) * V7X_LANES]
            inner = min(dil, MAX_FAST_SUBLANE_STRIDE)
            outer = dil // inner
            assert outer <= MAX_FAST_SUBLANE_STRIDE and outer * inner == dil
            if outer > 1:
                for c in range(planes):
                    for bb in range(inner):
                        mid_ref[c, bb * (ts // inner):(bb + 1) * (ts // inner), :] = (
                            qkv_ref[slot + c, pl.ds(bb, ts // inner, stride=inner), :])
            for r in range(dil):
                if outer > 1:
                    a, bb = r // inner, r % inner
                    rows = jnp.concatenate(
                        [mid_ref[c, pl.ds(bb * (ts // inner) + a, ts // dil, stride=outer), :]
                         for c in range(planes)], axis=1)
                else:
                    rows = jnp.concatenate([qkv_ref[slot + c, pl.ds(r, ts // dil, stride=dil), :]
                                            for c in range(planes)], axis=1)
                stream_ref[0, :, r * QKV_GROUP_WIDTH:(r + 1) * QKV_GROUP_WIDTH] = rows.astype(BF16)
            slot += planes
        keep = kvt_ref.shape[2]
        kvt_ref[0] = qkv[ts - keep:, GROUP_WIDTH:].T

    u = jnp.dot(h, win_ref[:, :POOL_WIDTH], preferred_element_type=F32)
    uext_ref[POOL_CARRY:POOL_CARRY + ts, :] = u
    pos = j * ts + lax.broadcasted_iota(jnp.int32, (ts, 1), 0)
    pooled = []
    for gi, w in enumerate(POOL_WINDOWS):
        lo, hi = gi * POOL_GROUP, (gi + 1) * POOL_GROUP
        tok = u[:, lo:hi]
        total = tok
        for back in range(1, w):
            total = total + uext_ref[POOL_CARRY - back:POOL_CARRY - back + ts, lo:hi]
        cnt = jnp.minimum(pos + 1, w).astype(F32)
        pooled.append(total / cnt - tok)
    pool_br = _pool_branch(pooled, wgrp_ref, pscale_ref, wpbr_ref)
    uext_ref[0:POOL_CARRY, :] = u[ts - POOL_CARRY:, :]
    tail_ref[0] = u[ts - POOL_CARRY:, :]
    gate_pool, gate_attn = _gates(h, win_ref)
    pp_ref[0] = gate_pool * pool_br
    ga_ref[0] = gate_attn


def _prompt_mix_in_call(x, g, win, cos, sin, wgrp, pscale, wpbr, ts):
    b, s, d = x.shape
    seq_tiles = s // ts
    zw = win.shape[1]

    def tile(width):
        return pl.BlockSpec((1, ts, width), lambda bi, j: (bi, j, 0))

    stream_specs, stream_shapes, kvt_specs, kvt_shapes = [], [], [], []
    for window, dil in ATTN_GROUPS:
        stream_specs.append(pl.BlockSpec((1, ts // dil, dil * QKV_GROUP_WIDTH), lambda bi, j: (bi, j, 0)))
        stream_shapes.append(jax.ShapeDtypeStruct((b, s // dil, dil * QKV_GROUP_WIDTH), BF16))
        keep = min(window, s)
        if keep >= ts:
            first = seq_tiles - keep // ts
            kvt_specs.append(pl.BlockSpec((1, 2 * GROUP_WIDTH, ts),
                                          lambda bi, j, first=first: (bi, 0, jnp.maximum(j - first, 0))))
        else:
            kvt_specs.append(pl.BlockSpec((1, 2 * GROUP_WIDTH, keep), lambda bi, j: (bi, 0, 0)))
        kvt_shapes.append(jax.ShapeDtypeStruct((b, 2 * GROUP_WIDTH, keep), F32))
    n_g = len(ATTN_GROUPS)
    out_specs = (stream_specs + kvt_specs
                 + [pl.BlockSpec((1, POOL_CARRY, POOL_WIDTH), lambda bi, j: (bi, 0, 0)), tile(d), tile(d)])
    out_shape = (stream_shapes + kvt_shapes
                 + [jax.ShapeDtypeStruct((b, POOL_CARRY, POOL_WIDTH), F32),
                    jax.ShapeDtypeStruct((b, s, d), F32), jax.ShapeDtypeStruct((b, s, d), F32)])
    table = pl.BlockSpec((ts, V7X_LANES), lambda bi, j: (j, 0))
    pipelined = (3 * _nbytes((ts, d), F32) + n_g * _nbytes((ts, QKV_GROUP_WIDTH), BF16)
                 + n_g * _nbytes((ts, 2 * GROUP_WIDTH), F32) + 2 * _nbytes((ts, V7X_LANES), F32))
    resident = _nbytes((d, zw), BF16) + _nbytes(wpbr.shape, BF16) + _nbytes(wgrp.shape, BF16)
    n_dilated = sum(1 for _, dil in ATTN_GROUPS if dil != 1)
    scratch = [pltpu.VMEM((ts + POOL_CARRY, POOL_WIDTH), F32),
               pltpu.VMEM((n_dilated * QKV_GROUP_WIDTH // V7X_LANES, ts, V7X_LANES), F32),
               pltpu.VMEM((QKV_GROUP_WIDTH // V7X_LANES, ts, V7X_LANES), F32)]
    temps = (_nbytes((ts + POOL_CARRY, POOL_WIDTH), F32) + (n_dilated + 1) * _nbytes((ts, QKV_GROUP_WIDTH), F32)
             + 6 * _nbytes((ts, d), F32))
    return pl.pallas_call(
        functools.partial(_prompt_mix_in_kernel, ts=ts),
        grid=(b, seq_tiles),
        in_specs=[tile(d), _resident((1, d)), _resident((d, zw)), table, table,
                  _resident(wgrp.shape), _resident((1, POOL_WIDTH)), _resident(wpbr.shape)],
        out_specs=out_specs,
        out_shape=out_shape,
        scratch_shapes=scratch,
        compiler_params=pltpu.CompilerParams(dimension_semantics=("arbitrary", "arbitrary"),
                                             vmem_limit_bytes=_vmem_limit(pipelined, resident, temps)),
        name="prompt_mix_in",
    )(x, g, win, cos, sin, wgrp, pscale, wpbr)


def _head_lane_masks():
    lane = lax.broadcasted_iota(jnp.int32, (1, GROUP_WIDTH), 1)
    return [(lane // HEAD_DIM) == hd for hd in range(HEADS)]


def _band_scores(q, kt, head_ones):
    q_heads = jnp.concatenate([q * one for one in head_ones], axis=0)
    return jnp.dot(q_heads, kt, preferred_element_type=F32)


def _band_softmax(sc, bias_ref, keys):
    blk = sc.shape[0] // HEADS
    heads = []
    for hd in range(HEADS):
        s_h = sc[hd * blk:(hd + 1) * blk, :] + bias_ref[:, :keys]
        mx = jnp.max(s_h, axis=1, keepdims=True)
        e = jnp.exp2(s_h - mx)
        heads.append((e.astype(BF16), mx, jnp.sum(e, axis=1, keepdims=True)))
    return heads


def _band_outputs(heads, v, low_head):
    outs, lses = [], []
    for hd, (e, mx, den) in enumerate(heads):
        tile = slice((hd * HEAD_DIM // V7X_LANES) * V7X_LANES, (hd * HEAD_DIM // V7X_LANES + 1) * V7X_LANES)
        o_un = jnp.dot(e, v[:, tile], preferred_element_type=F32)
        outs.append(o_un * (1.0 / den))
        lses.append(jnp.broadcast_to(mx + jnp.log2(den), (e.shape[0], V7X_LANES)))
    pairs = range(0, HEADS, V7X_LANES // HEAD_DIM)
    return ([jnp.where(low_head, outs[hd], outs[hd + 1]) for hd in pairs],
            [jnp.where(low_head, lses[hd], lses[hd + 1]) for hd in pairs])


ATTN_PLANES = GROUP_WIDTH // V7X_LANES
ATTN_BLOCKS_SIDE_BY_SIDE = {1: 1, 4: 3, 16: 2}


def _stream_attn_kernel(s_ref, o_ref, l_ref, first_bias_ref, band_bias_ref, *, dil, seq):
    blk = KEYS_PER_QUERY

    @pl.when(pl.program_id(0) == 0)
    def _():
        qi = lax.broadcasted_iota(jnp.int32, (blk, 1), 0)
        first = lax.broadcasted_iota(jnp.int32, (1, blk), 1) <= qi
        dist = qi + blk - lax.broadcasted_iota(jnp.int32, (1, 2 * blk), 1)
        first_bias_ref[...] = jnp.where(first, 0.0, NEG_INF)
        band_bias_ref[...] = jnp.where((dist >= 0) & (dist <= blk), 0.0, NEG_INF)

    head_ones = [jnp.where(m, 1.0, 0.0).astype(BF16) for m in _head_lane_masks()]
    low_head = lax.broadcasted_iota(jnp.int32, (1, V7X_LANES), 1) < HEAD_DIM
    blocks = [(r, i) for r in range(dil) for i in range(seq // (dil * blk))]
    side = ATTN_BLOCKS_SIDE_BY_SIDE.get(dil, 1)
    for first in range(0, len(blocks), side):
        operands = []
        for r, i in blocks[first:first + side]:
            base = r * QKV_GROUP_WIDTH
            krows = slice(max(i - 1, 0) * blk, (i + 1) * blk)
            operands.append((s_ref[0, i * blk:(i + 1) * blk, base:base + GROUP_WIDTH],
                             s_ref[0, krows, base + GROUP_WIDTH:base + 2 * GROUP_WIDTH].T,
                             s_ref[0, krows, base + 2 * GROUP_WIDTH:base + 3 * GROUP_WIDTH]))
        scores = [_band_scores(q, kt, head_ones) for q, kt, _ in operands]
        softmaxes = [_band_softmax(sc, first_bias_ref if i == 0 else band_bias_ref, v.shape[0])
                     for sc, (_, _, v), (_, i) in zip(scores, operands, blocks[first:first + side])]
        for heads, (_, _, v), (r, i) in zip(softmaxes, operands, blocks[first:first + side]):
            outs, lses = _band_outputs(heads, v, low_head)
            for c in range(ATTN_PLANES):
                o_ref[0, c, r, i * blk:(i + 1) * blk, :] = outs[c]
                l_ref[0, c, r, i * blk:(i + 1) * blk, :] = lses[c]


def _merge_groups(outs, lses, exp_fn):
    mx = functools.reduce(jnp.maximum, lses)
    ws = [exp_fn(l - mx) for l in lses]
    num = functools.reduce(lambda a, b: a + b, [w * o for w, o in zip(ws, outs)])
    return num / functools.reduce(lambda a, b: a + b, ws)


def _stream_attn_call(streams, dil, seq):
    b = streams.shape[0]

    def whole(a):
        zeros = (0,) * (len(a.shape) - 1)
        return pl.BlockSpec((1,) + tuple(a.shape[1:]), lambda bi: (bi,) + zeros)

    planes = jax.ShapeDtypeStruct((b, ATTN_PLANES, dil, seq // dil, V7X_LANES), F32)
    pipelined = _nbytes((seq, QKV_GROUP_WIDTH), BF16) + 2 * _nbytes((seq, GROUP_WIDTH), F32)
    scratch = [pltpu.VMEM((KEYS_PER_QUERY, KEYS_PER_QUERY), F32),
               pltpu.VMEM((KEYS_PER_QUERY, 2 * KEYS_PER_QUERY), F32)]
    scratch_bytes = 3 * _nbytes((KEYS_PER_QUERY, KEYS_PER_QUERY), F32)
    return pl.pallas_call(
        functools.partial(_stream_attn_kernel, dil=dil, seq=seq),
        grid=(b,),
        in_specs=[whole(streams)],
        out_specs=[whole(planes), whole(planes)],
        out_shape=[planes, planes],
        scratch_shapes=scratch,
        compiler_params=pltpu.CompilerParams(dimension_semantics=("arbitrary",),
                                             vmem_limit_bytes=_vmem_limit(pipelined, scratch_bytes, 16 << 20)),
        name=f"stream_attn_d{dil}",
    )(streams)


def _token_order(ref, plane, stage_ref, slot):
    dil, rows = ref.shape[2], ref.shape[3]
    if dil == 1:
        return ref[0, plane, 0]
    if dil <= MAX_FAST_SUBLANE_STRIDE:
        for r in range(dil):
            stage_ref[slot, pl.ds(r, rows, stride=dil), :] = ref[0, plane, r]
        return stage_ref[slot]
    inner = MAX_FAST_SUBLANE_STRIDE
    outer = dil // inner
    assert outer <= MAX_FAST_SUBLANE_STRIDE and outer * inner == dil
    tmp = stage_ref.shape[0] - 1
    mid = rows * outer
    for b in range(inner):
        for a in range(outer):
            stage_ref[tmp, pl.ds(b * mid + a, rows, stride=outer), :] = ref[0, plane, inner * a + b]
    for b in range(inner):
        stage_ref[slot, pl.ds(b, mid, stride=inner), :] = stage_ref[tmp, b * mid:(b + 1) * mid, :]
    return stage_ref[slot]


def _mix_out_rows(x, pp, ga, attn, wabr_ref, wo_ref, g2_ref, wg_ref, wu_ref, wd_ref, gf_ref):
    attn_br = jnp.dot(attn, wabr_ref[...], preferred_element_type=F32)
    merged = pp + ga * attn_br
    x = x + jnp.dot(merged.astype(BF16), wo_ref[...], preferred_element_type=F32)
    x = _macaron_ffn(x, g2_ref, wg_ref, wu_ref, wd_ref)
    return _rmsnorm(x, gf_ref[...])


def _mix_out_kernel(x_ref, pp_ref, ga_ref, *rest, n_groups):
    group_refs, rest = rest[:2 * n_groups], rest[2 * n_groups:]
    weights, rest = rest[:7], rest[7:]
    xs_ref, spp_ref, sga_ref, sattn_ref, y_ref, ys_ref, stage_ref = rest

    @pl.when(pl.program_id(0) == 0)
    def _():
        ys_ref[...] = _mix_out_rows(xs_ref[...], spp_ref[...], sga_ref[...], sattn_ref[...], *weights)

    tiles = []
    for c in range(ATTN_PLANES):
        outs = [_token_order(group_refs[2 * g], c, stage_ref, 2 * g) for g in range(n_groups)]
        lses = [_token_order(group_refs[2 * g + 1], c, stage_ref, 2 * g + 1) for g in range(n_groups)]
        tiles.append(_merge_groups(outs, lses, jnp.exp2).astype(BF16))
    attn = jnp.concatenate(tiles, axis=1)
    y_ref[...] = _mix_out_rows(x_ref[...], pp_ref[...], ga_ref[...], attn, *weights)


def _mix_out_call(x, pp, ga, group_planes, decode, wabr, wo, g2, wg, wu, wd, gf, tm):
    m, d = x.shape
    f = wg.shape[1]
    n_tiles = m // tm
    row = pl.BlockSpec((tm, d), lambda i: (i, 0))
    plane_specs = []
    for p in group_planes:
        dil, seq_rows = p.shape[2], p.shape[3]
        tiles = seq_rows * dil // tm
        plane_specs.append(pl.BlockSpec((1, ATTN_PLANES, dil, tm // dil, V7X_LANES),
                                        lambda i, tiles=tiles: (i // tiles, 0, 0, i % tiles, 0)))

    def whole(a):
        return pl.BlockSpec(a.shape, lambda i: (0, 0))

    n_dec = decode[0].shape[0]
    pipelined = (4 * _nbytes((tm, d), F32) + len(group_planes) * _nbytes((tm, GROUP_WIDTH), F32)
                 + 4 * _nbytes((n_dec, d), F32))
    scratch = [pltpu.VMEM((len(group_planes) + 1, tm, V7X_LANES), F32)]
    resident = 3 * _nbytes((d, f), BF16) + _nbytes((d, d), BF16) + _nbytes((GROUP_WIDTH, d), BF16)
    temps = 4 * _nbytes((tm, FFN_CHUNK), F32) + 4 * _nbytes((tm, d), F32)
    return pl.pallas_call(
        functools.partial(_mix_out_kernel, n_groups=len(group_planes) // 2),
        grid=(n_tiles,),
        in_specs=[row, row, row] + plane_specs
                 + [_resident((GROUP_WIDTH, d)), _resident((d, d)), _resident((1, d)),
                    _resident((d, f)), _resident((d, f)), _resident((f, d)), _resident((1, d))]
                 + [whole(a) for a in decode],
        out_specs=[row, pl.BlockSpec((n_dec, d), lambda i: (0, 0))],
        out_shape=[jax.ShapeDtypeStruct((m, d), F32), jax.ShapeDtypeStruct((n_dec, d), F32)],
        scratch_shapes=scratch,
        compiler_params=pltpu.CompilerParams(dimension_semantics=("arbitrary",),
                                             vmem_limit_bytes=_vmem_limit(pipelined, resident, temps)),
        name="mix_out",
    )(x, pp, ga, *group_planes, wabr, wo, g2, wg, wu, wd, gf, *decode)


def _sample_mix_in_kernel(x_ref, g_ref, win_ref, state_ref, cos_ref, sin_ref, wgrp_ref, pscale_ref, wpbr_ref,
                          qkv0_ref, qkv1_ref, qkv2_ref, u_ref, pp_ref, ga_ref):
    h = _rmsnorm(x_ref[...], g_ref[...]).astype(BF16)
    u = jnp.dot(h, win_ref[:, :POOL_WIDTH], preferred_element_type=F32)
    u_ref[...] = u
    pooled = []
    for gi, w in enumerate(POOL_WINDOWS):
        lo, hi = gi * POOL_GROUP, (gi + 1) * POOL_GROUP
        tok = u[:, lo:hi]
        total = tok
        for back in range(1, w):
            total = total + state_ref[POOL_STATE - back, :, lo:hi]
        pooled.append(total / float(min(PAST_LEN + 1, w)) - tok)
    pool_br = _pool_branch(pooled, wgrp_ref, pscale_ref, wpbr_ref)
    gate_pool, gate_attn = _gates(h, win_ref)
    pp_ref[...] = gate_pool * pool_br
    ga_ref[...] = gate_attn
    first_half = _first_half_mask()
    n = x_ref.shape[0]
    for g, qkv_ref in enumerate((qkv0_ref, qkv1_ref, qkv2_ref)):
        qkv = _qkv_group(h, win_ref, g, cos_ref[...], sin_ref[...], first_half, ATTN_SCALE)
        padded = jnp.concatenate([qkv, jnp.zeros((V7X_LANES - n, QKV_GROUP_WIDTH), F32)], axis=0)
        qkv_ref[...] = jnp.transpose(padded)[:, :n]


def _sample_mix_in_call(x, g, win, state_t, cos, sin, wgrp, pscale, wpbr):
    n, d = x.shape
    n_g = len(ATTN_GROUPS)

    def full(shape):
        zeros = (0,) * len(shape)
        return pl.BlockSpec(shape, lambda i: zeros)

    assert n <= V7X_LANES
    out_shape = ([jax.ShapeDtypeStruct((QKV_GROUP_WIDTH, n), F32)] * n_g
                 + [jax.ShapeDtypeStruct((n, POOL_WIDTH), F32),
                    jax.ShapeDtypeStruct((n, d), F32), jax.ShapeDtypeStruct((n, d), F32)])
    ins = (x, g, win, state_t, cos, sin, wgrp, pscale, wpbr)
    resident = sum(_nbytes(a.shape, a.dtype) for a in ins)
    return pl.pallas_call(
        _sample_mix_in_kernel,
        grid=(1,),
        in_specs=[full(a.shape) for a in ins],
        out_specs=[full(o.shape) for o in out_shape],
        out_shape=out_shape,
        compiler_params=pltpu.CompilerParams(dimension_semantics=("arbitrary",),
                                             vmem_limit_bytes=_vmem_limit(resident, 0, 8 << 20)),
        name="sample_mix_in",
    )(*ins)


def _head_sums(x):
    return jnp.concatenate([jnp.sum(x[hd * HEAD_DIM:(hd + 1) * HEAD_DIM], axis=0, keepdims=True)
                            for hd in range(HEADS)], axis=0)


def _head_expand(y):
    return jnp.concatenate([jnp.broadcast_to(y[hd:hd + 1], (HEAD_DIM, y.shape[1])) for hd in range(HEADS)],
                           axis=0)


def _cache_step_kernel(qkv0_ref, qkv1_ref, qkv2_ref, c0_ref, c1_ref, c2_ref,
                       n0_ref, n1_ref, n2_ref, attn_ref):
    step = pl.program_id(0)

    @pl.when(step == 0)
    def _():
        attn_ref[...] = jnp.zeros(attn_ref.shape, F32)

    groups = ((qkv0_ref, c0_ref, n0_ref), (qkv1_ref, c1_ref, n1_ref), (qkv2_ref, c2_ref, n2_ref))
    per_step = c0_ref.shape[0]
    attn = attn_ref[...]
    for j in range(per_step):
        n = step * per_step + j
        request = lax.broadcasted_iota(jnp.int32, qkv0_ref.shape, 1) == n
        cols = [jnp.sum(jnp.where(request, qkv_ref[...], 0.0), axis=1, keepdims=True) for qkv_ref, _, _ in groups]
        for qkv_col, (_, c_ref, new_ref), (window, _) in zip(cols, groups, ATTN_GROUPS):
            lane = lax.broadcasted_iota(jnp.int32, (1, window), 1)
            new_ref[j] = jnp.where(lane == window - 1, qkv_col[GROUP_WIDTH:], pltpu.roll(c_ref[j], window - 1, 1))
        outs, lses = [], []
        for qkv_col, (_, c_ref, _), (window, dil) in zip(cols, groups, ATTN_GROUPS):
            q = qkv_col[:GROUP_WIDTH]
            k_new = qkv_col[GROUP_WIDTH:2 * GROUP_WIDTH]
            v_new = qkv_col[2 * GROUP_WIDTH:]
            kt = c_ref[j, :GROUP_WIDTH, :]
            vt = c_ref[j, GROUP_WIDTH:, :]
            lane = lax.broadcasted_iota(jnp.int32, (1, window), 1)
            sc = jnp.where(lane % dil == 0, _head_sums(kt * q), NEG_INF)
            sc_new = _head_sums(k_new * q)
            mx = jnp.maximum(jnp.max(sc, axis=1, keepdims=True), sc_new)
            e = jnp.exp(sc - mx)
            e_new = jnp.exp(sc_new - mx)
            den = jnp.sum(e, axis=1, keepdims=True) + e_new
            o_un = jnp.sum(vt * _head_expand(e), axis=1, keepdims=True) + _head_expand(e_new) * v_new
            outs.append(o_un / _head_expand(den))
            lses.append(_head_expand(mx + jnp.log(den)))
        attn_col = _merge_groups(outs, lses, jnp.exp)
        attn = jnp.where(lax.broadcasted_iota(jnp.int32, attn_ref.shape, 1) == n, attn_col, attn)
    attn_ref[...] = attn


def _cache_step_call(qkv_cols, windows_t):
    n = qkv_cols[0].shape[1]
    per_step = CACHE_STEP_REQUESTS
    assert n % per_step == 0
    qspec = pl.BlockSpec((QKV_GROUP_WIDTH, n), lambda i: (0, 0))
    cspecs = [pl.BlockSpec((per_step,) + c.shape[1:], lambda i: (i, 0, 0)) for c in windows_t]
    pipelined = (2 * per_step * sum(_nbytes(c.shape[1:], F32) for c in windows_t)
                 + 3 * _nbytes((QKV_GROUP_WIDTH, V7X_LANES), F32))
    res = pl.pallas_call(
        _cache_step_kernel,
        grid=(n // per_step,),
        in_specs=[qspec] * len(qkv_cols) + cspecs,
        out_specs=cspecs + [pl.BlockSpec((GROUP_WIDTH, n), lambda i: (0, 0))],
        out_shape=[jax.ShapeDtypeStruct(c.shape, F32) for c in windows_t]
                  + [jax.ShapeDtypeStruct((GROUP_WIDTH, n), F32)],
        compiler_params=pltpu.CompilerParams(dimension_semantics=("arbitrary",),
                                             vmem_limit_bytes=_vmem_limit(pipelined, 0, 16 << 20)),
        name="cache_step",
    )(*qkv_cols, *windows_t)
    return res[:-1], res[-1]


def _rope_tables(pos):
    half = HEAD_DIM // 2
    inv = ROPE_THETA ** (-jnp.arange(half, dtype=F32) * (2.0 / HEAD_DIM))
    ang = pos.astype(F32)[:, None] * inv[None, :]
    cos, sin = jnp.cos(ang), jnp.sin(ang)
    reps = V7X_LANES // HEAD_DIM
    return (jnp.tile(jnp.concatenate([cos, cos], axis=1), (1, reps)),
            jnp.tile(jnp.concatenate([-sin, sin], axis=1), (1, reps)))


def _window_to_lanes(w):
    n, width = w.shape[:2]
    return jnp.transpose(w, (0, 2, 3, 4, 1)).reshape(n, 2 * GROUP_WIDTH, width)


def _window_from_lanes(wt):
    n, _, width = wt.shape
    return jnp.transpose(wt.reshape(n, 2, HEADS, HEAD_DIM, width), (0, 4, 1, 2, 3))[None]


def kernel(x_prompt, x_sample, state_pool, cache_kv_w128, cache_kv_w512, cache_kv_w2048, norm_ffn1, ffn1_w_gate, ffn1_w_up, ffn1_w_down, norm_mix, w_in, w_pool_grp, pool_scale, w_pool_br, w_attn_br, w_o, norm_ffn2, ffn2_w_gate, ffn2_w_up, ffn2_w_down, norm_final):
    b, s, d = x_prompt.shape
    n, dec_seq, _ = x_sample.shape
    caches = (cache_kv_w128, cache_kv_w512, cache_kv_w2048)
    assert norm_ffn1.shape[0] == 1 and dec_seq == 1, "single layer, single decode token"
    ts = PROMPT_ROW_TILE
    assert s % ts == 0 and ts >= POOL_CARRY
    for cache, (window, dil) in zip(caches, ATTN_GROUPS):
        assert window // dil == KEYS_PER_QUERY and s % (dil * KEYS_PER_QUERY) == 0 and ts % (16 * dil) == 0
        assert cache.shape[2] == window, "decode path expects full windows of history"

    row = lambda v: v.reshape(1, -1)
    bf = lambda w: w.astype(BF16)
    g1, g2, gm, gf = row(norm_ffn1[0]), row(norm_ffn2[0]), row(norm_mix[0]), row(norm_final)
    f1 = (bf(ffn1_w_gate[0]), bf(ffn1_w_up[0]), bf(ffn1_w_down[0]))
    pscale = row(pool_scale[0])

    cos_p, sin_p = _rope_tables(jnp.arange(s, dtype=jnp.int32))
    later = (ffn2_w_gate[0], ffn2_w_up[0], ffn2_w_down[0], w_in[0],
             w_pool_grp[0].reshape(-1, POOL_GROUP), w_pool_br[0], w_attn_br[0], w_o[0])
    x1, xs1, (f2g, f2u, f2d, win, wgrp, wpbr, wabr, wo) = _ffn_call(
        x_prompt.reshape(b * s, d), x_sample.reshape(n, d), g1, *f1, tm=FFN_ROW_TILE, cast=later)
    f2 = (f2g, f2u, f2d)
    wgrp = wgrp.reshape(w_pool_grp.shape[1:])
    (st0, st1, st2, kvt0, kvt1, kvt2, tail, pp, ga) = _prompt_mix_in_call(
        x1.reshape(b, s, d), gm, win, cos_p, sin_p, wgrp, pscale, wpbr, ts)
    group_planes = []
    for streams, (_, dil) in zip((st0, st1, st2), ATTN_GROUPS):
        group_planes += _stream_attn_call(streams, dil, s)
    pool_prompt = tail[None, :, POOL_CARRY - POOL_STATE:, :]

    cos_s, sin_s = _rope_tables(PAST_LEN + jnp.arange(dec_seq, dtype=jnp.int32))
    state_t = jnp.swapaxes(state_pool[0], 0, 1)
    (sqkv0, sqkv1, sqkv2, u_new, spp, sga) = _sample_mix_in_call(
        xs1, gm, win, state_t, cos_s, sin_s, wgrp, pscale, wpbr)
    new_windows, attn_t = _cache_step_call((sqkv0, sqkv1, sqkv2),
                                           [_window_to_lanes(c[0]) for c in caches])
    pool_sample = jnp.swapaxes(jnp.concatenate([state_t[1:], u_new[None]], axis=0), 0, 1)[None]

    y_prompt, y_sample = _mix_out_call(x1, pp.reshape(b * s, d), ga.reshape(b * s, d), tuple(group_planes),
                                       (xs1, spp, sga, attn_t.T.astype(BF16)), wabr, wo, g2, *f2, gf, tm=ts)
    y_prompt = y_prompt.reshape(b, s, d)
    y_sample = y_sample.reshape(n, dec_seq, d)

    return (y_prompt, y_sample, pool_prompt,
            _window_from_lanes(kvt0), _window_from_lanes(kvt1), _window_from_lanes(kvt2),
            pool_sample, *[_window_from_lanes(w) for w in new_windows])
```

```python
import functools

import jax
import jax.numpy as jnp
from jax import lax
from jax.experimental import pallas as pl
from jax.experimental.pallas import tpu as pltpu

F32 = jnp.float32
BF16 = jnp.bfloat16

PAST_LEN = 16384
POOL_WINDOWS = (2, 4, 8, 16)
POOL_GROUP = 128
POOL_WIDTH = POOL_GROUP * len(POOL_WINDOWS)
POOL_STATE = max(POOL_WINDOWS) - 1
POOL_CARRY = 16
POOL_PAD = 8
HEAD_DIM = 64
HEADS = 4
GROUP_WIDTH = HEADS * HEAD_DIM
ATTN_GROUPS = ((128, 1), (512, 4), (2048, 16))
KEYS_PER_QUERY = 128
QKV_GROUP_WIDTH = 3 * GROUP_WIDTH
ROPE_THETA = 10000.0
RMS_EPS = 1e-6
NEG_INF = -1e30
ATTN_SCALE = HEAD_DIM ** -0.5
LOG2_E = 1.4426950408889634

V7X_VMEM_BYTES = 64 * 1024 * 1024
VMEM_LEFT_TO_COMPILER = 4 * 1024 * 1024
V7X_LANES = 128
V7X_MXU_COLS = 256
MAX_FAST_SUBLANE_STRIDE = 4

PROMPT_ROW_TILE = 512
FFN_ROW_TILE = 1024
FFN_CHUNK = 4 * V7X_MXU_COLS
CACHE_STEP_REQUESTS = 2


def _vmem_limit(pipelined_bytes, resident_bytes, temp_bytes):
    need = 2 * pipelined_bytes + resident_bytes + temp_bytes
    return int(min(need, V7X_VMEM_BYTES - VMEM_LEFT_TO_COMPILER))


def _nbytes(shape, dtype):
    n = 1
    for s in shape:
        n *= s
    return n * jnp.dtype(dtype).itemsize


def _resident(shape):
    zeros = (0,) * len(shape)
    return pl.BlockSpec(shape, lambda *_: zeros, pipeline_mode=pl.Buffered(1))


def _rmsnorm(x, g):
    ms = jnp.mean(x * x, axis=-1, keepdims=True)
    return x * lax.rsqrt(ms + RMS_EPS) * g


def _ffn_chunks(width):
    return [(lo, min(lo + FFN_CHUNK, width)) for lo in range(0, width, FFN_CHUNK)]


def _swiglu(xn, wg_ref, wu_ref, wd_ref):
    acc = None
    for lo, hi in _ffn_chunks(wg_ref.shape[1]):
        gate = jnp.dot(xn, wg_ref[:, lo:hi], preferred_element_type=F32)
        up = jnp.dot(xn, wu_ref[:, lo:hi], preferred_element_type=F32)
        hid = (gate * jax.nn.sigmoid(gate) * up).astype(BF16)
        part = jnp.dot(hid, wd_ref[lo:hi, :], preferred_element_type=F32)
        acc = part if acc is None else acc + part
    return acc


def _macaron_ffn(x, g_ref, wg_ref, wu_ref, wd_ref):
    xn = _rmsnorm(x, g_ref[...]).astype(BF16)
    return x + 0.5 * _swiglu(xn, wg_ref, wu_ref, wd_ref)


def _ffn_kernel(x_ref, xs_ref, g_ref, wg_ref, wu_ref, wd_ref, *rest):
    n_cast = (len(rest) - 2) // 2
    o_ref, os_ref = rest[n_cast], rest[n_cast + 1]

    @pl.when(pl.program_id(0) == 0)
    def _():
        os_ref[...] = _macaron_ffn(xs_ref[...], g_ref, wg_ref, wu_ref, wd_ref)

    o_ref[...] = _macaron_ffn(x_ref[...], g_ref, wg_ref, wu_ref, wd_ref)
    for src_ref, dst_ref in zip(rest[:n_cast], rest[n_cast + 2:]):
        dst_ref[...] = src_ref[...].astype(BF16)


BF16_SUBLANE_ROWS = 16


def _ffn_call(x, xs, g, wg, wu, wd, tm, cast=()):
    m, d = x.shape
    f = wg.shape[1]
    steps = m // tm
    row = pl.BlockSpec((tm, d), lambda i: (i, 0))
    dec = pl.BlockSpec(xs.shape, lambda i: (0, 0))
    cast_specs, cast_bytes = [], 0
    for w in cast:
        rows = w.shape[0]
        per_step = next(r for r in range(BF16_SUBLANE_ROWS, rows + 1, BF16_SUBLANE_ROWS)
                        if rows % r == 0 and r * steps >= rows)
        last = rows // per_step - 1
        cast_specs.append(pl.BlockSpec((per_step, w.shape[1]), lambda i, last=last: (jnp.minimum(i, last), 0)))
        cast_bytes += _nbytes((per_step, w.shape[1]), F32) + _nbytes((per_step, w.shape[1]), BF16)
    limit = _vmem_limit(2 * _nbytes((tm, d), F32) + 2 * _nbytes(xs.shape, F32) + cast_bytes,
                        3 * _nbytes((d, f), BF16),
                        4 * _nbytes((tm, FFN_CHUNK), F32) + 2 * _nbytes((tm, d), F32))
    res = pl.pallas_call(
        _ffn_kernel,
        grid=(steps,),
        in_specs=[row, dec, _resident((1, d)), _resident((d, f)), _resident((d, f)), _resident((f, d))]
                 + cast_specs,
        out_specs=[row, dec] + cast_specs,
        out_shape=[jax.ShapeDtypeStruct((m, d), F32), jax.ShapeDtypeStruct(xs.shape, F32)]
                  + [jax.ShapeDtypeStruct(w.shape, BF16) for w in cast],
        compiler_params=pltpu.CompilerParams(dimension_semantics=("arbitrary",), vmem_limit_bytes=limit),
        name="ffn",
    )(x, xs, g, wg, wu, wd, *cast)
    return res[0], res[1], res[2:]


def _rope(t, cos, sin_signed, first_half):
    outs = []
    for c in range(t.shape[1] // V7X_LANES):
        tc = t[:, c * V7X_LANES:(c + 1) * V7X_LANES]
        partner = jnp.where(first_half,
                            pltpu.roll(tc, V7X_LANES - HEAD_DIM // 2, 1),
                            pltpu.roll(tc, HEAD_DIM // 2, 1))
        outs.append(tc * cos + partner * sin_signed)
    return jnp.concatenate(outs, axis=1)


def _first_half_mask():
    lane = lax.broadcasted_iota(jnp.int32, (1, V7X_LANES), 1)
    return (lane % HEAD_DIM) < (HEAD_DIM // 2)


def _qkv_group(h, win_ref, g, cos, sin_signed, first_half, q_scale):
    base = POOL_WIDTH + g * QKV_GROUP_WIDTH
    qkv = jnp.dot(h, win_ref[:, base:base + QKV_GROUP_WIDTH], preferred_element_type=F32)
    q = _rope(qkv[:, :GROUP_WIDTH], cos, sin_signed, first_half) * q_scale
    k = _rope(qkv[:, GROUP_WIDTH:2 * GROUP_WIDTH], cos, sin_signed, first_half)
    return jnp.concatenate([q, k, qkv[:, 2 * GROUP_WIDTH:]], axis=1)


def _pool_branch(window_means_minus_tok, wgrp_ref, pscale_ref, wpbr_ref):
    mixed = [jnp.dot(p.astype(BF16), wgrp_ref[gi], preferred_element_type=F32)
             for gi, p in enumerate(window_means_minus_tok)]
    pool = jnp.concatenate(mixed, axis=1) * pscale_ref[...]
    return jnp.dot(pool.astype(BF16), wpbr_ref[...], preferred_element_type=F32)


def _gates(h, win_ref):
    d = win_ref.shape[0]
    base = POOL_WIDTH + len(ATTN_GROUPS) * QKV_GROUP_WIDTH
    gate_pool = jax.nn.sigmoid(jnp.dot(h, win_ref[:, base:base + d], preferred_element_type=F32))
    gate_attn = jax.nn.sigmoid(jnp.dot(h, win_ref[:, base + d:base + 2 * d], preferred_element_type=F32))
    return gate_pool, gate_attn


def _prompt_mix_in_kernel(x_ref, g_ref, win_ref, cos_ref, sin_ref, wgrp_ref, pscale_ref, wpbr_ref,
                          s0_ref, s1_ref, s2_ref,
                          kvt0_ref, kvt1_ref, kvt2_ref, tail_ref, pp_ref, ga_ref,
                          uext_ref, sums_ref, qkv_ref, mid_ref, *, ts):
    j = pl.program_id(1)

    @pl.when(j == 0)
    def _():
        uext_ref[0:POOL_PAD + POOL_CARRY, :] = jnp.zeros((POOL_PAD + POOL_CARRY, POOL_WIDTH), F32)
        sums_ref[:, 0:POOL_PAD, :] = jnp.zeros((sums_ref.shape[0], POOL_PAD, POOL_WIDTH), F32)

    h = _rmsnorm(x_ref[0], g_ref[...]).astype(BF16)

    cos = cos_ref[...]
    sin_signed = sin_ref[...]
    first_half = _first_half_mask()
    outs = ((s0_ref, kvt0_ref), (s1_ref, kvt1_ref), (s2_ref, kvt2_ref))
    planes = QKV_GROUP_WIDTH // V7X_LANES
    slot = 0
    for g in sorted(range(len(ATTN_GROUPS)), key=lambda g: -ATTN_GROUPS[g][1]):
        stream_ref, kvt_ref = outs[g]
        dil = ATTN_GROUPS[g][1]
        qkv = _qkv_group(h, win_ref, g, cos, sin_signed, first_half, ATTN_SCALE * LOG2_E)
        if dil == 1:
            stream_ref[0] = qkv.astype(BF16)
        else:
            for c in range(planes):
                qkv_ref[slot + c] = qkv[:, c * V7X_LANES:(c + 1) * V7X_LANES]
            inner = min(dil, MAX_FAST_SUBLANE_STRIDE)
            outer = dil // inner
            assert outer <= MAX_FAST_SUBLANE_STRIDE and outer * inner == dil
            if outer > 1:
                for c in range(planes):
                    for bb in range(inner):
                        mid_ref[c, bb * (ts // inner):(bb + 1) * (ts // inner), :] = (
                            qkv_ref[slot + c, pl.ds(bb, ts // inner, stride=inner), :])
            for r in range(dil):
                if outer > 1:
                    a, bb = r // inner, r % inner
                    rows = jnp.concatenate(
                        [mid_ref[c, pl.ds(bb * (ts // inner) + a, ts // dil, stride=outer), :]
                         for c in range(planes)], axis=1)
                else:
                    rows = jnp.concatenate([qkv_ref[slot + c, pl.ds(r, ts // dil, stride=dil), :]
                                            for c in range(planes)], axis=1)
                stream_ref[0, :, r * QKV_GROUP_WIDTH:(r + 1) * QKV_GROUP_WIDTH] = rows.astype(BF16)
            slot += planes
        keep = kvt_ref.shape[2]
        kvt_ref[0] = qkv[ts - keep:, GROUP_WIDTH:].T

    u = jnp.dot(h, win_ref[:, :POOL_WIDTH], preferred_element_type=F32)
    ext = POOL_CARRY + ts
    uext_ref[POOL_PAD + POOL_CARRY:POOL_PAD + ext, :] = u
    pos = j * ts + lax.broadcasted_iota(jnp.int32, (ts, 1), 0)
    pooled = []
    for gi, w in enumerate(POOL_WINDOWS):
        lo, hi = gi * POOL_GROUP, (gi + 1) * POOL_GROUP

        def trailing(level, shift, lo=lo, hi=hi):
            start = POOL_PAD - shift
            if level == 0:
                return uext_ref[start:start + ext, lo:hi]
            return sums_ref[level - 1, start:start + ext, lo:hi]

        span, level = 1, 0
        while span < w:
            total = trailing(level, 0) + trailing(level, span)
            span *= 2
            if span < w:
                sums_ref[level, POOL_PAD:POOL_PAD + ext, lo:hi] = total
                level += 1
        cnt = jnp.minimum(pos + 1, w).astype(F32)
        pooled.append(total[POOL_CARRY:, :] / cnt - u[:, lo:hi])
    pool_br = _pool_branch(pooled, wgrp_ref, pscale_ref, wpbr_ref)
    uext_ref[POOL_PAD:POOL_PAD + POOL_CARRY, :] = u[ts - POOL_CARRY:, :]
    tail_ref[0] = u[ts - POOL_CARRY:, :]
    gate_pool, gate_attn = _gates(h, win_ref)
    pp_ref[0] = gate_pool * pool_br
    ga_ref[0] = gate_attn


def _prompt_mix_in_call(x, g, win, cos, sin, wgrp, pscale, wpbr, ts):
    b, s, d = x.shape
    seq_tiles = s // ts
    zw = win.shape[1]

    def tile(width):
        return pl.BlockSpec((1, ts, width), lambda bi, j: (bi, j, 0))

    stream_specs, stream_shapes, kvt_specs, kvt_shapes = [], [], [], []
    for window, dil in ATTN_GROUPS:
        stream_specs.append(pl.BlockSpec((1, ts // dil, dil * QKV_GROUP_WIDTH), lambda bi, j: (bi, j, 0)))
        stream_shapes.append(jax.ShapeDtypeStruct((b, s // dil, dil * QKV_GROUP_WIDTH), BF16))
        keep = min(window, s)
        if keep >= ts:
            first = seq_tiles - keep // ts
            kvt_specs.append(pl.BlockSpec((1, 2 * GROUP_WIDTH, ts),
                                          lambda bi, j, first=first: (bi, 0, jnp.maximum(j - first, 0))))
        else:
            kvt_specs.append(pl.BlockSpec((1, 2 * GROUP_WIDTH, keep), lambda bi, j: (bi, 0, 0)))
        kvt_shapes.append(jax.ShapeDtypeStruct((b, 2 * GROUP_WIDTH, keep), F32))
    n_g = len(ATTN_GROUPS)
    out_specs = (stream_specs + kvt_specs
                 + [pl.BlockSpec((1, POOL_CARRY, POOL_WIDTH), lambda bi, j: (bi, 0, 0)), tile(d), tile(d)])
    out_shape = (stream_shapes + kvt_shapes
                 + [jax.ShapeDtypeStruct((b, POOL_CARRY, POOL_WIDTH), F32),
                    jax.ShapeDtypeStruct((b, s, d), F32), jax.ShapeDtypeStruct((b, s, d), F32)])
    table = pl.BlockSpec((ts, V7X_LANES), lambda bi, j: (j, 0))
    pipelined = (3 * _nbytes((ts, d), F32) + n_g * _nbytes((ts, QKV_GROUP_WIDTH), BF16)
                 + n_g * _nbytes((ts, 2 * GROUP_WIDTH), F32) + 2 * _nbytes((ts, V7X_LANES), F32))
    resident = _nbytes((d, zw), BF16) + _nbytes(wpbr.shape, BF16) + _nbytes(wgrp.shape, BF16)
    n_dilated = sum(1 for _, dil in ATTN_GROUPS if dil != 1)
    pool_rows = POOL_PAD + POOL_CARRY + ts
    pool_levels = max(POOL_WINDOWS).bit_length() - 2
    assert all(w & (w - 1) == 0 for w in POOL_WINDOWS) and max(POOL_WINDOWS) // 2 <= POOL_PAD
    scratch = [pltpu.VMEM((pool_rows, POOL_WIDTH), F32),
               pltpu.VMEM((pool_levels, pool_rows, POOL_WIDTH), F32),
               pltpu.VMEM((n_dilated * QKV_GROUP_WIDTH // V7X_LANES, ts, V7X_LANES), F32),
               pltpu.VMEM((QKV_GROUP_WIDTH // V7X_LANES, ts, V7X_LANES), F32)]
    temps = ((1 + pool_levels) * _nbytes((pool_rows, POOL_WIDTH), F32)
             + (n_dilated + 1) * _nbytes((ts, QKV_GROUP_WIDTH), F32)
             + 6 * _nbytes((ts, d), F32))
    return pl.pallas_call(
        functools.partial(_prompt_mix_in_kernel, ts=ts),
        grid=(b, seq_tiles),
        in_specs=[tile(d), _resident((1, d)), _resident((d, zw)), table, table,
                  _resident(wgrp.shape), _resident((1, POOL_WIDTH)), _resident(wpbr.shape)],
        out_specs=out_specs,
        out_shape=out_shape,
        scratch_shapes=scratch,
        compiler_params=pltpu.CompilerParams(dimension_semantics=("arbitrary", "arbitrary"),
                                             vmem_limit_bytes=_vmem_limit(pipelined, resident, temps)),
        name="prompt_mix_in",
    )(x, g, win, cos, sin, wgrp, pscale, wpbr)


def _head_lane_masks():
    lane = lax.broadcasted_iota(jnp.int32, (1, GROUP_WIDTH), 1)
    return [(lane // HEAD_DIM) == hd for hd in range(HEADS)]


def _band_scores(q, kt, head_ones):
    q_heads = jnp.concatenate([q * one for one in head_ones], axis=0)
    return jnp.dot(q_heads, kt, preferred_element_type=F32)


def _band_softmax(sc, bias_ref, keys):
    blk = sc.shape[0] // HEADS
    heads = []
    for hd in range(HEADS):
        s_h = sc[hd * blk:(hd + 1) * blk, :] + bias_ref[:, :keys]
        mx = jnp.max(s_h, axis=1, keepdims=True)
        e = jnp.exp2(s_h - mx)
        heads.append((e.astype(BF16), mx, jnp.sum(e, axis=1, keepdims=True)))
    return heads


def _band_outputs(heads, v, low_head):
    outs, lses = [], []
    for hd, (e, mx, den) in enumerate(heads):
        tile = slice((hd * HEAD_DIM // V7X_LANES) * V7X_LANES, (hd * HEAD_DIM // V7X_LANES + 1) * V7X_LANES)
        o_un = jnp.dot(e, v[:, tile], preferred_element_type=F32)
        outs.append(o_un * (1.0 / den))
        lses.append(jnp.broadcast_to(mx + jnp.log2(den), (e.shape[0], V7X_LANES)))
    pairs = range(0, HEADS, V7X_LANES // HEAD_DIM)
    return ([jnp.where(low_head, outs[hd], outs[hd + 1]) for hd in pairs],
            [jnp.where(low_head, lses[hd], lses[hd + 1]) for hd in pairs])


ATTN_PLANES = GROUP_WIDTH // V7X_LANES
ATTN_BLOCKS_SIDE_BY_SIDE = {1: 1, 4: 3, 16: 2}


def _stream_attn_kernel(s_ref, o_ref, l_ref, first_bias_ref, band_bias_ref, *, dil, seq):
    blk = KEYS_PER_QUERY

    @pl.when(pl.program_id(0) == 0)
    def _():
        qi = lax.broadcasted_iota(jnp.int32, (blk, 1), 0)
        first = lax.broadcasted_iota(jnp.int32, (1, blk), 1) <= qi
        dist = qi + blk - lax.broadcasted_iota(jnp.int32, (1, 2 * blk), 1)
        first_bias_ref[...] = jnp.where(first, 0.0, NEG_INF)
        band_bias_ref[...] = jnp.where((dist >= 0) & (dist <= blk), 0.0, NEG_INF)

    head_ones = [jnp.where(m, 1.0, 0.0).astype(BF16) for m in _head_lane_masks()]
    low_head = lax.broadcasted_iota(jnp.int32, (1, V7X_LANES), 1) < HEAD_DIM
    blocks = [(r, i) for r in range(dil) for i in range(seq // (dil * blk))]
    side = ATTN_BLOCKS_SIDE_BY_SIDE.get(dil, 1)
    for first in range(0, len(blocks), side):
        operands = []
        for r, i in blocks[first:first + side]:
            base = r * QKV_GROUP_WIDTH
            krows = slice(max(i - 1, 0) * blk, (i + 1) * blk)
            operands.append((s_ref[0, i * blk:(i + 1) * blk, base:base + GROUP_WIDTH],
                             s_ref[0, krows, base + GROUP_WIDTH:base + 2 * GROUP_WIDTH].T,
                             s_ref[0, krows, base + 2 * GROUP_WIDTH:base + 3 * GROUP_WIDTH]))
        scores = [_band_scores(q, kt, head_ones) for q, kt, _ in operands]
        softmaxes = [_band_softmax(sc, first_bias_ref if i == 0 else band_bias_ref, v.shape[0])
                     for sc, (_, _, v), (_, i) in zip(scores, operands, blocks[first:first + side])]
        for heads, (_, _, v), (r, i) in zip(softmaxes, operands, blocks[first:first + side]):
            outs, lses = _band_outputs(heads, v, low_head)
            for c in range(ATTN_PLANES):
                o_ref[0, c, r, i * blk:(i + 1) * blk, :] = outs[c]
                l_ref[0, c, r, i * blk:(i + 1) * blk, :] = lses[c]


def _merge_groups(outs, lses, exp_fn):
    mx = functools.reduce(jnp.maximum, lses)
    ws = [exp_fn(l - mx) for l in lses]
    num = functools.reduce(lambda a, b: a + b, [w * o for w, o in zip(ws, outs)])
    return num / functools.reduce(lambda a, b: a + b, ws)


def _stream_attn_call(streams, dil, seq):
    b = streams.shape[0]

    def whole(a):
        zeros = (0,) * (len(a.shape) - 1)
        return pl.BlockSpec((1,) + tuple(a.shape[1:]), lambda bi: (bi,) + zeros)

    planes = jax.ShapeDtypeStruct((b, ATTN_PLANES, dil, seq // dil, V7X_LANES), F32)
    pipelined = _nbytes((seq, QKV_GROUP_WIDTH), BF16) + 2 * _nbytes((seq, GROUP_WIDTH), F32)
    scratch = [pltpu.VMEM((KEYS_PER_QUERY, KEYS_PER_QUERY), F32),
               pltpu.VMEM((KEYS_PER_QUERY, 2 * KEYS_PER_QUERY), F32)]
    scratch_bytes = 3 * _nbytes((KEYS_PER_QUERY, KEYS_PER_QUERY), F32)
    return pl.pallas_call(
        functools.partial(_stream_attn_kernel, dil=dil, seq=seq),
        grid=(b,),
        in_specs=[whole(streams)],
        out_specs=[whole(planes), whole(planes)],
        out_shape=[planes, planes],
        scratch_shapes=scratch,
        compiler_params=pltpu.CompilerParams(dimension_semantics=("arbitrary",),
                                             vmem_limit_bytes=_vmem_limit(pipelined, scratch_bytes, 16 << 20)),
        name=f"stream_attn_d{dil}",
    )(streams)


def _token_order(ref, plane, stage_ref, slot):
    dil, rows = ref.shape[2], ref.shape[3]
    if dil == 1:
        return ref[0, plane, 0]
    if dil <= MAX_FAST_SUBLANE_STRIDE:
        for r in range(dil):
            stage_ref[slot, pl.ds(r, rows, stride=dil), :] = ref[0, plane, r]
        return stage_ref[slot]
    inner = MAX_FAST_SUBLANE_STRIDE
    outer = dil // inner
    assert outer <= MAX_FAST_SUBLANE_STRIDE and outer * inner == dil
    tmp = stage_ref.shape[0] - 1
    mid = rows * outer
    for b in range(inner):
        for a in range(outer):
            stage_ref[tmp, pl.ds(b * mid + a, rows, stride=outer), :] = ref[0, plane, inner * a + b]
    for b in range(inner):
        stage_ref[slot, pl.ds(b, mid, stride=inner), :] = stage_ref[tmp, b * mid:(b + 1) * mid, :]
    return stage_ref[slot]


def _mix_out_rows(x, pp, ga, attn, wabr_ref, wo_ref, g2_ref, wg_ref, wu_ref, wd_ref, gf_ref):
    attn_br = jnp.dot(attn, wabr_ref[...], preferred_element_type=F32)
    merged = pp + ga * attn_br
    x = x + jnp.dot(merged.astype(BF16), wo_ref[...], preferred_element_type=F32)
    x = _macaron_ffn(x, g2_ref, wg_ref, wu_ref, wd_ref)
    return _rmsnorm(x, gf_ref[...])


def _mix_out_kernel(x_ref, pp_ref, ga_ref, *rest, n_groups):
    group_refs, rest = rest[:2 * n_groups], rest[2 * n_groups:]
    weights, rest = rest[:7], rest[7:]
    xs_ref, spp_ref, sga_ref, sattn_ref, y_ref, ys_ref, stage_ref = rest

    @pl.when(pl.program_id(0) == 0)
    def _():
        ys_ref[...] = _mix_out_rows(xs_ref[...], spp_ref[...], sga_ref[...], sattn_ref[...], *weights)

    tiles = []
    for c in range(ATTN_PLANES):
        outs = [_token_order(group_refs[2 * g], c, stage_ref, 2 * g) for g in range(n_groups)]
        lses = [_token_order(group_refs[2 * g + 1], c, stage_ref, 2 * g + 1) for g in range(n_groups)]
        tiles.append(_merge_groups(outs, lses, jnp.exp2).astype(BF16))
    attn = jnp.concatenate(tiles, axis=1)
    y_ref[...] = _mix_out_rows(x_ref[...], pp_ref[...], ga_ref[...], attn, *weights)


def _mix_out_call(x, pp, ga, group_planes, decode, wabr, wo, g2, wg, wu, wd, gf, tm):
    m, d = x.shape
    f = wg.shape[1]
    n_tiles = m // tm
    row = pl.BlockSpec((tm, d), lambda i: (i, 0))
    plane_specs = []
    for p in group_planes:
        dil, seq_rows = p.shape[2], p.shape[3]
        tiles = seq_rows * dil // tm
        plane_specs.append(pl.BlockSpec((1, ATTN_PLANES, dil, tm // dil, V7X_LANES),
                                        lambda i, tiles=tiles: (i // tiles, 0, 0, i % tiles, 0)))

    def whole(a):
        return pl.BlockSpec(a.shape, lambda i: (0, 0))

    n_dec = decode[0].shape[0]
    pipelined = (4 * _nbytes((tm, d), F32) + len(group_planes) * _nbytes((tm, GROUP_WIDTH), F32)
                 + 4 * _nbytes((n_dec, d), F32))
    scratch = [pltpu.VMEM((len(group_planes) + 1, tm, V7X_LANES), F32)]
    resident = 3 * _nbytes((d, f), BF16) + _nbytes((d, d), BF16) + _nbytes((GROUP_WIDTH, d), BF16)
    temps = 4 * _nbytes((tm, FFN_CHUNK), F32) + 4 * _nbytes((tm, d), F32)
    return pl.pallas_call(
        functools.partial(_mix_out_kernel, n_groups=len(group_planes) // 2),
        grid=(n_tiles,),
        in_specs=[row, row, row] + plane_specs
                 + [_resident((GROUP_WIDTH, d)), _resident((d, d)), _resident((1, d)),
                    _resident((d, f)), _resident((d, f)), _resident((f, d)), _resident((1, d))]
                 + [whole(a) for a in decode],
        out_specs=[row, pl.BlockSpec((n_dec, d), lambda i: (0, 0))],
        out_shape=[jax.ShapeDtypeStruct((m, d), F32), jax.ShapeDtypeStruct((n_dec, d), F32)],
        scratch_shapes=scratch,
        compiler_params=pltpu.CompilerParams(dimension_semantics=("arbitrary",),
                                             vmem_limit_bytes=_vmem_limit(pipelined, resident, temps)),
        name="mix_out",
    )(x, pp, ga, *group_planes, wabr, wo, g2, wg, wu, wd, gf, *decode)


def _sample_mix_in_kernel(x_ref, g_ref, win_ref, state_ref, cos_ref, sin_ref, wgrp_ref, pscale_ref, wpbr_ref,
                          qkv0_ref, qkv1_ref, qkv2_ref, u_ref, pp_ref, ga_ref):
    h = _rmsnorm(x_ref[...], g_ref[...]).astype(BF16)
    u = jnp.dot(h, win_ref[:, :POOL_WIDTH], preferred_element_type=F32)
    u_ref[...] = u
    pooled = []
    for gi, w in enumerate(POOL_WINDOWS):
        lo, hi = gi * POOL_GROUP, (gi + 1) * POOL_GROUP
        tok = u[:, lo:hi]
        total = tok
        for back in range(1, w):
            total = total + state_ref[POOL_STATE - back, :, lo:hi]
        pooled.append(total / float(min(PAST_LEN + 1, w)) - tok)
    pool_br = _pool_branch(pooled, wgrp_ref, pscale_ref, wpbr_ref)
    gate_pool, gate_attn = _gates(h, win_ref)
    pp_ref[...] = gate_pool * pool_br
    ga_ref[...] = gate_attn
    first_half = _first_half_mask()
    n = x_ref.shape[0]
    for g, qkv_ref in enumerate((qkv0_ref, qkv1_ref, qkv2_ref)):
        qkv = _qkv_group(h, win_ref, g, cos_ref[...], sin_ref[...], first_half, ATTN_SCALE)
        padded = jnp.concatenate([qkv, jnp.zeros((V7X_LANES - n, QKV_GROUP_WIDTH), F32)], axis=0)
        qkv_ref[...] = jnp.transpose(padded)[:, :n]


def _sample_mix_in_call(x, g, win, state_t, cos, sin, wgrp, pscale, wpbr):
    n, d = x.shape
    n_g = len(ATTN_GROUPS)

    def full(shape):
        zeros = (0,) * len(shape)
        return pl.BlockSpec(shape, lambda i: zeros)

    assert n <= V7X_LANES
    out_shape = ([jax.ShapeDtypeStruct((QKV_GROUP_WIDTH, n), F32)] * n_g
                 + [jax.ShapeDtypeStruct((n, POOL_WIDTH), F32),
                    jax.ShapeDtypeStruct((n, d), F32), jax.ShapeDtypeStruct((n, d), F32)])
    ins = (x, g, win, state_t, cos, sin, wgrp, pscale, wpbr)
    resident = sum(_nbytes(a.shape, a.dtype) for a in ins)
    return pl.pallas_call(
        _sample_mix_in_kernel,
        grid=(1,),
        in_specs=[full(a.shape) for a in ins],
        out_specs=[full(o.shape) for o in out_shape],
        out_shape=out_shape,
        compiler_params=pltpu.CompilerParams(dimension_semantics=("arbitrary",),
                                             vmem_limit_bytes=_vmem_limit(resident, 0, 8 << 20)),
        name="sample_mix_in",
    )(*ins)


def _head_sums(x):
    return jnp.concatenate([jnp.sum(x[hd * HEAD_DIM:(hd + 1) * HEAD_DIM], axis=0, keepdims=True)
                            for hd in range(HEADS)], axis=0)


def _head_expand(y):
    return jnp.concatenate([jnp.broadcast_to(y[hd:hd + 1], (HEAD_DIM, y.shape[1])) for hd in range(HEADS)],
                           axis=0)


def _cache_step_kernel(qkv0_ref, qkv1_ref, qkv2_ref, c0_ref, c1_ref, c2_ref,
                       n0_ref, n1_ref, n2_ref, attn_ref):
    step = pl.program_id(0)

    @pl.when(step == 0)
    def _():
        attn_ref[...] = jnp.zeros(attn_ref.shape, F32)

    groups = ((qkv0_ref, c0_ref, n0_ref), (qkv1_ref, c1_ref, n1_ref), (qkv2_ref, c2_ref, n2_ref))
    per_step = c0_ref.shape[0]
    attn = attn_ref[...]
    for j in range(per_step):
        n = step * per_step + j
        request = lax.broadcasted_iota(jnp.int32, qkv0_ref.shape, 1) == n
        cols = [jnp.sum(jnp.where(request, qkv_ref[...], 0.0), axis=1, keepdims=True) for qkv_ref, _, _ in groups]
        for qkv_col, (_, c_ref, new_ref), (window, _) in zip(cols, groups, ATTN_GROUPS):
            lane = lax.broadcasted_iota(jnp.int32, (1, window), 1)
            new_ref[j] = jnp.where(lane == window - 1, qkv_col[GROUP_WIDTH:], pltpu.roll(c_ref[j], window - 1, 1))
        outs, lses = [], []
        for qkv_col, (_, c_ref, _), (window, dil) in zip(cols, groups, ATTN_GROUPS):
            q = qkv_col[:GROUP_WIDTH]
            k_new = qkv_col[GROUP_WIDTH:2 * GROUP_WIDTH]
            v_new = qkv_col[2 * GROUP_WIDTH:]
            kt = c_ref[j, :GROUP_WIDTH, :]
            vt = c_ref[j, GROUP_WIDTH:, :]
            lane = lax.broadcasted_iota(jnp.int32, (1, window), 1)
            sc = jnp.where(lane % dil == 0, _head_sums(kt * q), NEG_INF)
            sc_new = _head_sums(k_new * q)
            mx = jnp.maximum(jnp.max(sc, axis=1, keepdims=True), sc_new)
            e = jnp.exp(sc - mx)
            e_new = jnp.exp(sc_new - mx)
            den = jnp.sum(e, axis=1, keepdims=True) + e_new
            o_un = jnp.sum(vt * _head_expand(e), axis=1, keepdims=True) + _head_expand(e_new) * v_new
            outs.append(o_un / _head_expand(den))
            lses.append(_head_expand(mx + jnp.log(den)))
        attn_col = _merge_groups(outs, lses, jnp.exp)
        attn = jnp.where(lax.broadcasted_iota(jnp.int32, attn_ref.shape, 1) == n, attn_col, attn)
    attn_ref[...] = attn


def _cache_step_call(qkv_cols, windows_t):
    n = qkv_cols[0].shape[1]
    per_step = CACHE_STEP_REQUESTS
    assert n % per_step == 0
    qspec = pl.BlockSpec((QKV_GROUP_WIDTH, n), lambda i: (0, 0))
    cspecs = [pl.BlockSpec((per_step,) + c.shape[1:], lambda i: (i, 0, 0)) for c in windows_t]
    pipelined = (2 * per_step * sum(_nbytes(c.shape[1:], F32) for c in windows_t)
                 + 3 * _nbytes((QKV_GROUP_WIDTH, V7X_LANES), F32))
    res = pl.pallas_call(
        _cache_step_kernel,
        grid=(n // per_step,),
        in_specs=[qspec] * len(qkv_cols) + cspecs,
        out_specs=cspecs + [pl.BlockSpec((GROUP_WIDTH, n), lambda i: (0, 0))],
        out_shape=[jax.ShapeDtypeStruct(c.shape, F32) for c in windows_t]
                  + [jax.ShapeDtypeStruct((GROUP_WIDTH, n), F32)],
        compiler_params=pltpu.CompilerParams(dimension_semantics=("arbitrary",),
                                             vmem_limit_bytes=_vmem_limit(pipelined, 0, 16 << 20)),
        name="cache_step",
    )(*qkv_cols, *windows_t)
    return res[:-1], res[-1]


def _rope_tables(pos):
    half = HEAD_DIM // 2
    inv = ROPE_THETA ** (-jnp.arange(half, dtype=F32) * (2.0 / HEAD_DIM))
    ang = pos.astype(F32)[:, None] * inv[None, :]
    cos, sin = jnp.cos(ang), jnp.sin(ang)
    reps = V7X_LANES // HEAD_DIM
    return (jnp.tile(jnp.concatenate([cos, cos], axis=1), (1, reps)),
            jnp.tile(jnp.concatenate([-sin, sin], axis=1), (1, reps)))


def _window_to_lanes(w):
    n, width = w.shape[:2]
    return jnp.transpose(w, (0, 2, 3, 4, 1)).reshape(n, 2 * GROUP_WIDTH, width)


def _window_from_lanes(wt):
    n, _, width = wt.shape
    return jnp.transpose(wt.reshape(n, 2, HEADS, HEAD_DIM, width), (0, 4, 1, 2, 3))[None]


def kernel(x_prompt, x_sample, state_pool, cache_kv_w128, cache_kv_w512, cache_kv_w2048, norm_ffn1, ffn1_w_gate, ffn1_w_up, ffn1_w_down, norm_mix, w_in, w_pool_grp, pool_scale, w_pool_br, w_attn_br, w_o, norm_ffn2, ffn2_w_gate, ffn2_w_up, ffn2_w_down, norm_final):
    b, s, d = x_prompt.shape
    n, dec_seq, _ = x_sample.shape
    caches = (cache_kv_w128, cache_kv_w512, cache_kv_w2048)
    assert norm_ffn1.shape[0] == 1 and dec_seq == 1, "single layer, single decode token"
    ts = PROMPT_ROW_TILE
    assert s % ts == 0 and ts >= POOL_CARRY
    for cache, (window, dil) in zip(caches, ATTN_GROUPS):
        assert window // dil == KEYS_PER_QUERY and s % (dil * KEYS_PER_QUERY) == 0 and ts % (16 * dil) == 0
        assert cache.shape[2] == window, "decode path expects full windows of history"

    row = lambda v: v.reshape(1, -1)
    bf = lambda w: w.astype(BF16)
    g1, g2, gm, gf = row(norm_ffn1[0]), row(norm_ffn2[0]), row(norm_mix[0]), row(norm_final)
    f1 = (bf(ffn1_w_gate[0]), bf(ffn1_w_up[0]), bf(ffn1_w_down[0]))
    pscale = row(pool_scale[0])

    cos_p, sin_p = _rope_tables(jnp.arange(s, dtype=jnp.int32))
    later = (ffn2_w_gate[0], ffn2_w_up[0], ffn2_w_down[0], w_in[0],
             w_pool_grp[0].reshape(-1, POOL_GROUP), w_pool_br[0], w_attn_br[0], w_o[0])
    x1, xs1, (f2g, f2u, f2d, win, wgrp, wpbr, wabr, wo) = _ffn_call(
        x_prompt.reshape(b * s, d), x_sample.reshape(n, d), g1, *f1, tm=FFN_ROW_TILE, cast=later)
    f2 = (f2g, f2u, f2d)
    wgrp = wgrp.reshape(w_pool_grp.shape[1:])
    (st0, st1, st2, kvt0, kvt1, kvt2, tail, pp, ga) = _prompt_mix_in_call(
        x1.reshape(b, s, d), gm, win, cos_p, sin_p, wgrp, pscale, wpbr, ts)
    group_planes = []
    for streams, (_, dil) in zip((st0, st1, st2), ATTN_GROUPS):
        group_planes += _stream_attn_call(streams, dil, s)
    pool_prompt = tail[None, :, POOL_CARRY - POOL_STATE:, :]

    cos_s, sin_s = _rope_tables(PAST_LEN + jnp.arange(dec_seq, dtype=jnp.int32))
    state_t = jnp.swapaxes(state_pool[0], 0, 1)
    (sqkv0, sqkv1, sqkv2, u_new, spp, sga) = _sample_mix_in_call(
        xs1, gm, win, state_t, cos_s, sin_s, wgrp, pscale, wpbr)
    new_windows, attn_t = _cache_step_call((sqkv0, sqkv1, sqkv2),
                                           [_window_to_lanes(c[0]) for c in caches])
    pool_sample = jnp.swapaxes(jnp.concatenate([state_t[1:], u_new[None]], axis=0), 0, 1)[None]

    y_prompt, y_sample = _mix_out_call(x1, pp.reshape(b * s, d), ga.reshape(b * s, d), tuple(group_planes),
                                       (xs1, spp, sga, attn_t.T.astype(BF16)), wabr, wo, g2, *f2, gf, tm=ts)
    y_prompt = y_prompt.reshape(b, s, d)
    y_sample = y_sample.reshape(n, dec_seq, d)

    return (y_prompt, y_sample, pool_prompt,
            _window_from_lanes(kvt0), _window_from_lanes(kvt1), _window_from_lanes(kvt2),
            pool_sample, *[_window_from_lanes(w) for w in new_windows])
```

```python
import functools

import jax
import jax.numpy as jnp
from jax import lax
from jax.experimental import pallas as pl
from jax.experimental.pallas import tpu as pltpu

F32 = jnp.float32
BF16 = jnp.bfloat16

PAST_LEN = 16384
POOL_WINDOWS = (2, 4, 8, 16)
POOL_GROUP = 128
POOL_WIDTH = POOL_GROUP * len(POOL_WINDOWS)
POOL_STATE = max(POOL_WINDOWS) - 1
POOL_CARRY = 16
POOL_PAD = 8
HEAD_DIM = 64
HEADS = 4
GROUP_WIDTH = HEADS * HEAD_DIM
ATTN_GROUPS = ((128, 1), (512, 4), (2048, 16))
KEYS_PER_QUERY = 128
QKV_GROUP_WIDTH = 3 * GROUP_WIDTH
ROPE_THETA = 10000.0
RMS_EPS = 1e-6
NEG_INF = -1e30
ATTN_SCALE = HEAD_DIM ** -0.5
LOG2_E = 1.4426950408889634

V7X_VMEM_BYTES = 64 * 1024 * 1024
VMEM_LEFT_TO_COMPILER = 4 * 1024 * 1024
V7X_LANES = 128
V7X_MXU_COLS = 256
MAX_FAST_SUBLANE_STRIDE = 4

PROMPT_ROW_TILE = 512
FFN_ROW_TILE = 1024
FFN_CHUNK = 6 * V7X_MXU_COLS
CACHE_STEP_REQUESTS = 2


def _vmem_limit(pipelined_bytes, resident_bytes, temp_bytes):
    need = 2 * pipelined_bytes + resident_bytes + temp_bytes
    return int(min(need, V7X_VMEM_BYTES - VMEM_LEFT_TO_COMPILER))


def _nbytes(shape, dtype):
    n = 1
    for s in shape:
        n *= s
    return n * jnp.dtype(dtype).itemsize


def _resident(shape):
    zeros = (0,) * len(shape)
    return pl.BlockSpec(shape, lambda *_: zeros, pipeline_mode=pl.Buffered(1))


def _rmsnorm(x, g):
    ms = jnp.mean(x * x, axis=-1, keepdims=True)
    return x * lax.rsqrt(ms + RMS_EPS) * g


def _ffn_chunks(width):
    return [(lo, min(lo + FFN_CHUNK, width)) for lo in range(0, width, FFN_CHUNK)]


def _swiglu(xn, wg_ref, wu_ref, wd_ref):
    acc = None
    for lo, hi in _ffn_chunks(wg_ref.shape[1]):
        gate = jnp.dot(xn, wg_ref[:, lo:hi], preferred_element_type=F32)
        up = jnp.dot(xn, wu_ref[:, lo:hi], preferred_element_type=F32)
        hid = (gate * jax.nn.sigmoid(gate) * up).astype(BF16)
        part = jnp.dot(hid, wd_ref[lo:hi, :], preferred_element_type=F32)
        acc = part if acc is None else acc + part
    return acc


def _macaron_ffn(x, g_ref, wg_ref, wu_ref, wd_ref):
    xn = _rmsnorm(x, g_ref[...]).astype(BF16)
    return x + 0.5 * _swiglu(xn, wg_ref, wu_ref, wd_ref)


def _ffn_kernel(x_ref, xs_ref, g_ref, wg_ref, wu_ref, wd_ref, *rest):
    n_cast = (len(rest) - 2) // 2
    o_ref, os_ref = rest[n_cast], rest[n_cast + 1]

    @pl.when(pl.program_id(0) == 0)
    def _():
        os_ref[...] = _macaron_ffn(xs_ref[...], g_ref, wg_ref, wu_ref, wd_ref)

    o_ref[...] = _macaron_ffn(x_ref[...], g_ref, wg_ref, wu_ref, wd_ref)
    for src_ref, dst_ref in zip(rest[:n_cast], rest[n_cast + 2:]):
        dst_ref[...] = src_ref[...].astype(BF16)


BF16_SUBLANE_ROWS = 16


def _ffn_call(x, xs, g, wg, wu, wd, tm, cast=()):
    m, d = x.shape
    f = wg.shape[1]
    steps = m // tm
    row = pl.BlockSpec((tm, d), lambda i: (i, 0))
    dec = pl.BlockSpec(xs.shape, lambda i: (0, 0))
    cast_specs, cast_bytes = [], 0
    for w in cast:
        rows = w.shape[0]
        per_step = next(r for r in range(BF16_SUBLANE_ROWS, rows + 1, BF16_SUBLANE_ROWS)
                        if rows % r == 0 and r * steps >= rows)
        last = rows // per_step - 1
        cast_specs.append(pl.BlockSpec((per_step, w.shape[1]), lambda i, last=last: (jnp.minimum(i, last), 0)))
        cast_bytes += _nbytes((per_step, w.shape[1]), F32) + _nbytes((per_step, w.shape[1]), BF16)
    limit = _vmem_limit(2 * _nbytes((tm, d), F32) + 2 * _nbytes(xs.shape, F32) + cast_bytes,
                        3 * _nbytes((d, f), BF16),
                        4 * _nbytes((tm, FFN_CHUNK), F32) + 2 * _nbytes((tm, d), F32))
    res = pl.pallas_call(
        _ffn_kernel,
        grid=(steps,),
        in_specs=[row, dec, _resident((1, d)), _resident((d, f)), _resident((d, f)), _resident((f, d))]
                 + cast_specs,
        out_specs=[row, dec] + cast_specs,
        out_shape=[jax.ShapeDtypeStruct((m, d), F32), jax.ShapeDtypeStruct(xs.shape, F32)]
                  + [jax.ShapeDtypeStruct(w.shape, BF16) for w in cast],
        compiler_params=pltpu.CompilerParams(dimension_semantics=("arbitrary",), vmem_limit_bytes=limit),
        name="ffn",
    )(x, xs, g, wg, wu, wd, *cast)
    return res[0], res[1], res[2:]


def _rope(t, cos, sin_signed, first_half):
    outs = []
    for c in range(t.shape[1] // V7X_LANES):
        tc = t[:, c * V7X_LANES:(c + 1) * V7X_LANES]
        partner = jnp.where(first_half,
                            pltpu.roll(tc, V7X_LANES - HEAD_DIM // 2, 1),
                            pltpu.roll(tc, HEAD_DIM // 2, 1))
        outs.append(tc * cos + partner * sin_signed)
    return jnp.concatenate(outs, axis=1)


def _first_half_mask():
    lane = lax.broadcasted_iota(jnp.int32, (1, V7X_LANES), 1)
    return (lane % HEAD_DIM) < (HEAD_DIM // 2)


def _qkv_group(h, win_ref, g, cos, sin_signed, first_half, q_scale):
    base = POOL_WIDTH + g * QKV_GROUP_WIDTH
    qkv = jnp.dot(h, win_ref[:, base:base + QKV_GROUP_WIDTH], preferred_element_type=F32)
    q = _rope(qkv[:, :GROUP_WIDTH], cos, sin_signed, first_half) * q_scale
    k = _rope(qkv[:, GROUP_WIDTH:2 * GROUP_WIDTH], cos, sin_signed, first_half)
    return jnp.concatenate([q, k, qkv[:, 2 * GROUP_WIDTH:]], axis=1)


def _pool_branch(window_means_minus_tok, wgrp_ref, pscale_ref, wpbr_ref):
    mixed = [jnp.dot(p.astype(BF16), wgrp_ref[gi], preferred_element_type=F32)
             for gi, p in enumerate(window_means_minus_tok)]
    pool = jnp.concatenate(mixed, axis=1) * pscale_ref[...]
    return jnp.dot(pool.astype(BF16), wpbr_ref[...], preferred_element_type=F32)


def _gates(h, win_ref):
    d = win_ref.shape[0]
    base = POOL_WIDTH + len(ATTN_GROUPS) * QKV_GROUP_WIDTH
    gate_pool = jax.nn.sigmoid(jnp.dot(h, win_ref[:, base:base + d], preferred_element_type=F32))
    gate_attn = jax.nn.sigmoid(jnp.dot(h, win_ref[:, base + d:base + 2 * d], preferred_element_type=F32))
    return gate_pool, gate_attn


def _prompt_mix_in_kernel(x_ref, g_ref, win_ref, cos_ref, sin_ref, wgrp_ref, pscale_ref, wpbr_ref,
                          s0_ref, s1_ref, s2_ref,
                          kvt0_ref, kvt1_ref, kvt2_ref, tail_ref, pp_ref, ga_ref,
                          uext_ref, sums_ref, qkv_ref, mid_ref, *, ts):
    j = pl.program_id(1)

    @pl.when(j == 0)
    def _():
        uext_ref[0:POOL_PAD + POOL_CARRY, :] = jnp.zeros((POOL_PAD + POOL_CARRY, POOL_WIDTH), F32)
        sums_ref[:, 0:POOL_PAD, :] = jnp.zeros((sums_ref.shape[0], POOL_PAD, POOL_WIDTH), F32)

    h = _rmsnorm(x_ref[0], g_ref[...]).astype(BF16)

    cos = cos_ref[...]
    sin_signed = sin_ref[...]
    first_half = _first_half_mask()
    outs = ((s0_ref, kvt0_ref), (s1_ref, kvt1_ref), (s2_ref, kvt2_ref))
    planes = QKV_GROUP_WIDTH // V7X_LANES
    slot = 0
    for g in sorted(range(len(ATTN_GROUPS)), key=lambda g: -ATTN_GROUPS[g][1]):
        stream_ref, kvt_ref = outs[g]
        dil = ATTN_GROUPS[g][1]
        qkv = _qkv_group(h, win_ref, g, cos, sin_signed, first_half, ATTN_SCALE * LOG2_E)
        if dil == 1:
            stream_ref[0] = qkv.astype(BF16)
        else:
            for c in range(planes):
                qkv_ref[slot + c] = qkv[:, c * V7X_LANES:(c + 1) * V7X_LANES]
            inner = min(dil, MAX_FAST_SUBLANE_STRIDE)
            outer = dil // inner
            assert outer <= MAX_FAST_SUBLANE_STRIDE and outer * inner == dil
            if outer > 1:
                for c in range(planes):
                    for bb in range(inner):
                        mid_ref[c, bb * (ts // inner):(bb + 1) * (ts // inner), :] = (
                            qkv_ref[slot + c, pl.ds(bb, ts // inner, stride=inner), :])
            for r in range(dil):
                if outer > 1:
                    a, bb = r // inner, r % inner
                    rows = jnp.concatenate(
                        [mid_ref[c, pl.ds(bb * (ts // inner) + a, ts // dil, stride=outer), :]
                         for c in range(planes)], axis=1)
                else:
                    rows = jnp.concatenate([qkv_ref[slot + c, pl.ds(r, ts // dil, stride=dil), :]
                                            for c in range(planes)], axis=1)
                stream_ref[0, :, r * QKV_GROUP_WIDTH:(r + 1) * QKV_GROUP_WIDTH] = rows.astype(BF16)
            slot += planes
        keep = kvt_ref.shape[2]
        kvt_ref[0] = qkv[ts - keep:, GROUP_WIDTH:].T

    u = jnp.dot(h, win_ref[:, :POOL_WIDTH], preferred_element_type=F32)
    ext = POOL_CARRY + ts
    uext_ref[POOL_PAD + POOL_CARRY:POOL_PAD + ext, :] = u
    pos = j * ts + lax.broadcasted_iota(jnp.int32, (ts, 1), 0)
    pooled = []
    for gi, w in enumerate(POOL_WINDOWS):
        lo, hi = gi * POOL_GROUP, (gi + 1) * POOL_GROUP

        def trailing(level, shift, lo=lo, hi=hi):
            start = POOL_PAD - shift
            if level == 0:
                return uext_ref[start:start + ext, lo:hi]
            return sums_ref[level - 1, start:start + ext, lo:hi]

        span, level = 1, 0
        while span < w:
            total = trailing(level, 0) + trailing(level, span)
            span *= 2
            if span < w:
                sums_ref[level, POOL_PAD:POOL_PAD + ext, lo:hi] = total
                level += 1
        cnt = jnp.minimum(pos + 1, w).astype(F32)
        pooled.append(total[POOL_CARRY:, :] / cnt - u[:, lo:hi])
    pool_br = _pool_branch(pooled, wgrp_ref, pscale_ref, wpbr_ref)
    uext_ref[POOL_PAD:POOL_PAD + POOL_CARRY, :] = u[ts - POOL_CARRY:, :]
    tail_ref[0] = u[ts - POOL_CARRY:, :]
    gate_pool, gate_attn = _gates(h, win_ref)
    pp_ref[0] = gate_pool * pool_br
    ga_ref[0] = gate_attn


def _prompt_mix_in_call(x, g, win, cos, sin, wgrp, pscale, wpbr, ts):
    b, s, d = x.shape
    seq_tiles = s // ts
    zw = win.shape[1]

    def tile(width):
        return pl.BlockSpec((1, ts, width), lambda bi, j: (bi, j, 0))

    stream_specs, stream_shapes, kvt_specs, kvt_shapes = [], [], [], []
    for window, dil in ATTN_GROUPS:
        stream_specs.append(pl.BlockSpec((1, ts // dil, dil * QKV_GROUP_WIDTH), lambda bi, j: (bi, j, 0)))
        stream_shapes.append(jax.ShapeDtypeStruct((b, s // dil, dil * QKV_GROUP_WIDTH), BF16))
        keep = min(window, s)
        if keep >= ts:
            first = seq_tiles - keep // ts
            kvt_specs.append(pl.BlockSpec((1, 2 * GROUP_WIDTH, ts),
                                          lambda bi, j, first=first: (bi, 0, jnp.maximum(j - first, 0))))
        else:
            kvt_specs.append(pl.BlockSpec((1, 2 * GROUP_WIDTH, keep), lambda bi, j: (bi, 0, 0)))
        kvt_shapes.append(jax.ShapeDtypeStruct((b, 2 * GROUP_WIDTH, keep), F32))
    n_g = len(ATTN_GROUPS)
    out_specs = (stream_specs + kvt_specs
                 + [pl.BlockSpec((1, POOL_CARRY, POOL_WIDTH), lambda bi, j: (bi, 0, 0)), tile(d), tile(d)])
    out_shape = (stream_shapes + kvt_shapes
                 + [jax.ShapeDtypeStruct((b, POOL_CARRY, POOL_WIDTH), F32),
                    jax.ShapeDtypeStruct((b, s, d), F32), jax.ShapeDtypeStruct((b, s, d), F32)])
    table = pl.BlockSpec((ts, V7X_LANES), lambda bi, j: (j, 0))
    pipelined = (3 * _nbytes((ts, d), F32) + n_g * _nbytes((ts, QKV_GROUP_WIDTH), BF16)
                 + n_g * _nbytes((ts, 2 * GROUP_WIDTH), F32) + 2 * _nbytes((ts, V7X_LANES), F32))
    resident = _nbytes((d, zw), BF16) + _nbytes(wpbr.shape, BF16) + _nbytes(wgrp.shape, BF16)
    n_dilated = sum(1 for _, dil in ATTN_GROUPS if dil != 1)
    pool_rows = POOL_PAD + POOL_CARRY + ts
    pool_levels = max(POOL_WINDOWS).bit_length() - 2
    assert all(w & (w - 1) == 0 for w in POOL_WINDOWS) and max(POOL_WINDOWS) // 2 <= POOL_PAD
    scratch = [pltpu.VMEM((pool_rows, POOL_WIDTH), F32),
               pltpu.VMEM((pool_levels, pool_rows, POOL_WIDTH), F32),
               pltpu.VMEM((n_dilated * QKV_GROUP_WIDTH // V7X_LANES, ts, V7X_LANES), F32),
               pltpu.VMEM((QKV_GROUP_WIDTH // V7X_LANES, ts, V7X_LANES), F32)]
    temps = ((1 + pool_levels) * _nbytes((pool_rows, POOL_WIDTH), F32)
             + (n_dilated + 1) * _nbytes((ts, QKV_GROUP_WIDTH), F32)
             + 6 * _nbytes((ts, d), F32))
    return pl.pallas_call(
        functools.partial(_prompt_mix_in_kernel, ts=ts),
        grid=(b, seq_tiles),
        in_specs=[tile(d), _resident((1, d)), _resident((d, zw)), table, table,
                  _resident(wgrp.shape), _resident((1, POOL_WIDTH)), _resident(wpbr.shape)],
        out_specs=out_specs,
        out_shape=out_shape,
        scratch_shapes=scratch,
        compiler_params=pltpu.CompilerParams(dimension_semantics=("arbitrary", "arbitrary"),
                                             vmem_limit_bytes=_vmem_limit(pipelined, resident, temps)),
        name="prompt_mix_in",
    )(x, g, win, cos, sin, wgrp, pscale, wpbr)


def _head_lane_masks():
    lane = lax.broadcasted_iota(jnp.int32, (1, GROUP_WIDTH), 1)
    return [(lane // HEAD_DIM) == hd for hd in range(HEADS)]


def _band_scores(q, kt, head_ones):
    q_heads = jnp.concatenate([q * one for one in head_ones], axis=0)
    return jnp.dot(q_heads, kt, preferred_element_type=F32)


def _band_softmax(sc, bias_ref, keys):
    blk = sc.shape[0] // HEADS
    heads = []
    for hd in range(HEADS):
        s_h = sc[hd * blk:(hd + 1) * blk, :] + bias_ref[:, :keys]
        mx = jnp.max(s_h, axis=1, keepdims=True)
        e = jnp.exp2(s_h - mx)
        heads.append((e.astype(BF16), mx, jnp.sum(e, axis=1, keepdims=True)))
    return heads


def _band_outputs(heads, v, low_head):
    outs, lses = [], []
    for hd, (e, mx, den) in enumerate(heads):
        tile = slice((hd * HEAD_DIM // V7X_LANES) * V7X_LANES, (hd * HEAD_DIM // V7X_LANES + 1) * V7X_LANES)
        o_un = jnp.dot(e, v[:, tile], preferred_element_type=F32)
        outs.append(o_un * (1.0 / den))
        lses.append(jnp.broadcast_to(mx + jnp.log2(den), (e.shape[0], V7X_LANES)))
    pairs = range(0, HEADS, V7X_LANES // HEAD_DIM)
    return ([jnp.where(low_head, outs[hd], outs[hd + 1]) for hd in pairs],
            [jnp.where(low_head, lses[hd], lses[hd + 1]) for hd in pairs])


ATTN_PLANES = GROUP_WIDTH // V7X_LANES
ATTN_BLOCKS_SIDE_BY_SIDE = {1: 1, 4: 3, 16: 2}


def _stream_attn_kernel(s_ref, o_ref, l_ref, first_bias_ref, band_bias_ref, *, dil, seq):
    blk = KEYS_PER_QUERY

    @pl.when(pl.program_id(0) == 0)
    def _():
        qi = lax.broadcasted_iota(jnp.int32, (blk, 1), 0)
        first = lax.broadcasted_iota(jnp.int32, (1, blk), 1) <= qi
        dist = qi + blk - lax.broadcasted_iota(jnp.int32, (1, 2 * blk), 1)
        first_bias_ref[...] = jnp.where(first, 0.0, NEG_INF)
        band_bias_ref[...] = jnp.where((dist >= 0) & (dist <= blk), 0.0, NEG_INF)

    head_ones = [jnp.where(m, 1.0, 0.0).astype(BF16) for m in _head_lane_masks()]
    low_head = lax.broadcasted_iota(jnp.int32, (1, V7X_LANES), 1) < HEAD_DIM
    blocks = [(r, i) for r in range(dil) for i in range(seq // (dil * blk))]
    side = ATTN_BLOCKS_SIDE_BY_SIDE.get(dil, 1)
    for first in range(0, len(blocks), side):
        operands = []
        for r, i in blocks[first:first + side]:
            base = r * QKV_GROUP_WIDTH
            krows = slice(max(i - 1, 0) * blk, (i + 1) * blk)
            operands.append((s_ref[0, i * blk:(i + 1) * blk, base:base + GROUP_WIDTH],
                             s_ref[0, krows, base + GROUP_WIDTH:base + 2 * GROUP_WIDTH].T,
                             s_ref[0, krows, base + 2 * GROUP_WIDTH:base + 3 * GROUP_WIDTH]))
        scores = [_band_scores(q, kt, head_ones) for q, kt, _ in operands]
        softmaxes = [_band_softmax(sc, first_bias_ref if i == 0 else band_bias_ref, v.shape[0])
                     for sc, (_, _, v), (_, i) in zip(scores, operands, blocks[first:first + side])]
        for heads, (_, _, v), (r, i) in zip(softmaxes, operands, blocks[first:first + side]):
            outs, lses = _band_outputs(heads, v, low_head)
            for c in range(ATTN_PLANES):
                o_ref[0, c, r, i * blk:(i + 1) * blk, :] = outs[c]
                l_ref[0, c, r, i * blk:(i + 1) * blk, :] = lses[c]


def _merge_groups(outs, lses, exp_fn):
    mx = functools.reduce(jnp.maximum, lses)
    ws = [exp_fn(l - mx) for l in lses]
    num = functools.reduce(lambda a, b: a + b, [w * o for w, o in zip(ws, outs)])
    return num / functools.reduce(lambda a, b: a + b, ws)


def _stream_attn_call(streams, dil, seq):
    b = streams.shape[0]

    def whole(a):
        zeros = (0,) * (len(a.shape) - 1)
        return pl.BlockSpec((1,) + tuple(a.shape[1:]), lambda bi: (bi,) + zeros)

    planes = jax.ShapeDtypeStruct((b, ATTN_PLANES, dil, seq // dil, V7X_LANES), F32)
    pipelined = _nbytes((seq, QKV_GROUP_WIDTH), BF16) + 2 * _nbytes((seq, GROUP_WIDTH), F32)
    scratch = [pltpu.VMEM((KEYS_PER_QUERY, KEYS_PER_QUERY), F32),
               pltpu.VMEM((KEYS_PER_QUERY, 2 * KEYS_PER_QUERY), F32)]
    scratch_bytes = 3 * _nbytes((KEYS_PER_QUERY, KEYS_PER_QUERY), F32)
    return pl.pallas_call(
        functools.partial(_stream_attn_kernel, dil=dil, seq=seq),
        grid=(b,),
        in_specs=[whole(streams)],
        out_specs=[whole(planes), whole(planes)],
        out_shape=[planes, planes],
        scratch_shapes=scratch,
        compiler_params=pltpu.CompilerParams(dimension_semantics=("arbitrary",),
                                             vmem_limit_bytes=_vmem_limit(pipelined, scratch_bytes, 16 << 20)),
        name=f"stream_attn_d{dil}",
    )(streams)


def _token_order(ref, plane, stage_ref, slot):
    dil, rows = ref.shape[2], ref.shape[3]
    if dil == 1:
        return ref[0, plane, 0]
    if dil <= MAX_FAST_SUBLANE_STRIDE:
        for r in range(dil):
            stage_ref[slot, pl.ds(r, rows, stride=dil), :] = ref[0, plane, r]
        return stage_ref[slot]
    inner = MAX_FAST_SUBLANE_STRIDE
    outer = dil // inner
    assert outer <= MAX_FAST_SUBLANE_STRIDE and outer * inner == dil
    tmp = stage_ref.shape[0] - 1
    mid = rows * outer
    for b in range(inner):
        for a in range(outer):
            stage_ref[tmp, pl.ds(b * mid + a, rows, stride=outer), :] = ref[0, plane, inner * a + b]
    for b in range(inner):
        stage_ref[slot, pl.ds(b, mid, stride=inner), :] = stage_ref[tmp, b * mid:(b + 1) * mid, :]
    return stage_ref[slot]


def _mix_out_rows(x, pp, ga, attn, wabr_ref, wo_ref, g2_ref, wg_ref, wu_ref, wd_ref, gf_ref):
    attn_br = jnp.dot(attn, wabr_ref[...], preferred_element_type=F32)
    merged = pp + ga * attn_br
    x = x + jnp.dot(merged.astype(BF16), wo_ref[...], preferred_element_type=F32)
    x = _macaron_ffn(x, g2_ref, wg_ref, wu_ref, wd_ref)
    return _rmsnorm(x, gf_ref[...])


def _mix_out_kernel(x_ref, pp_ref, ga_ref, *rest, n_groups):
    group_refs, rest = rest[:2 * n_groups], rest[2 * n_groups:]
    weights, rest = rest[:7], rest[7:]
    xs_ref, spp_ref, sga_ref, sattn_ref, y_ref, ys_ref, stage_ref = rest

    @pl.when(pl.program_id(0) == 0)
    def _():
        ys_ref[...] = _mix_out_rows(xs_ref[...], spp_ref[...], sga_ref[...], sattn_ref[...], *weights)

    tiles = []
    for c in range(ATTN_PLANES):
        outs = [_token_order(group_refs[2 * g], c, stage_ref, 2 * g) for g in range(n_groups)]
        lses = [_token_order(group_refs[2 * g + 1], c, stage_ref, 2 * g + 1) for g in range(n_groups)]
        tiles.append(_merge_groups(outs, lses, jnp.exp2).astype(BF16))
    attn = jnp.concatenate(tiles, axis=1)
    y_ref[...] = _mix_out_rows(x_ref[...], pp_ref[...], ga_ref[...], attn, *weights)


def _mix_out_call(x, pp, ga, group_planes, decode, wabr, wo, g2, wg, wu, wd, gf, tm):
    m, d = x.shape
    f = wg.shape[1]
    n_tiles = m // tm
    row = pl.BlockSpec((tm, d), lambda i: (i, 0))
    plane_specs = []
    for p in group_planes:
        dil, seq_rows = p.shape[2], p.shape[3]
        tiles = seq_rows * dil // tm
        plane_specs.append(pl.BlockSpec((1, ATTN_PLANES, dil, tm // dil, V7X_LANES),
                                        lambda i, tiles=tiles: (i // tiles, 0, 0, i % tiles, 0)))

    def whole(a):
        return pl.BlockSpec(a.shape, lambda i: (0, 0))

    n_dec = decode[0].shape[0]
    pipelined = (4 * _nbytes((tm, d), F32) + len(group_planes) * _nbytes((tm, GROUP_WIDTH), F32)
                 + 4 * _nbytes((n_dec, d), F32))
    scratch = [pltpu.VMEM((len(group_planes) + 1, tm, V7X_LANES), F32)]
    resident = 3 * _nbytes((d, f), BF16) + _nbytes((d, d), BF16) + _nbytes((GROUP_WIDTH, d), BF16)
    temps = 4 * _nbytes((tm, FFN_CHUNK), F32) + 4 * _nbytes((tm, d), F32)
    return pl.pallas_call(
        functools.partial(_mix_out_kernel, n_groups=len(group_planes) // 2),
        grid=(n_tiles,),
        in_specs=[row, row, row] + plane_specs
                 + [_resident((GROUP_WIDTH, d)), _resident((d, d)), _resident((1, d)),
                    _resident((d, f)), _resident((d, f)), _resident((f, d)), _resident((1, d))]
                 + [whole(a) for a in decode],
        out_specs=[row, pl.BlockSpec((n_dec, d), lambda i: (0, 0))],
        out_shape=[jax.ShapeDtypeStruct((m, d), F32), jax.ShapeDtypeStruct((n_dec, d), F32)],
        scratch_shapes=scratch,
        compiler_params=pltpu.CompilerParams(dimension_semantics=("arbitrary",),
                                             vmem_limit_bytes=_vmem_limit(pipelined, resident, temps)),
        name="mix_out",
    )(x, pp, ga, *group_planes, wabr, wo, g2, wg, wu, wd, gf, *decode)


def _sample_mix_in_kernel(x_ref, g_ref, win_ref, state_ref, cos_ref, sin_ref, wgrp_ref, pscale_ref, wpbr_ref,
                          qkv0_ref, qkv1_ref, qkv2_ref, u_ref, pp_ref, ga_ref):
    h = _rmsnorm(x_ref[...], g_ref[...]).astype(BF16)
    u = jnp.dot(h, win_ref[:, :POOL_WIDTH], preferred_element_type=F32)
    u_ref[...] = u
    pooled = []
    for gi, w in enumerate(POOL_WINDOWS):
        lo, hi = gi * POOL_GROUP, (gi + 1) * POOL_GROUP
        tok = u[:, lo:hi]
        total = tok
        for back in range(1, w):
            total = total + state_ref[POOL_STATE - back, :, lo:hi]
        pooled.append(total / float(min(PAST_LEN + 1, w)) - tok)
    pool_br = _pool_branch(pooled, wgrp_ref, pscale_ref, wpbr_ref)
    gate_pool, gate_attn = _gates(h, win_ref)
    pp_ref[...] = gate_pool * pool_br
    ga_ref[...] = gate_attn
    first_half = _first_half_mask()
    n = x_ref.shape[0]
    for g, qkv_ref in enumerate((qkv0_ref, qkv1_ref, qkv2_ref)):
        qkv = _qkv_group(h, win_ref, g, cos_ref[...], sin_ref[...], first_half, ATTN_SCALE)
        padded = jnp.concatenate([qkv, jnp.zeros((V7X_LANES - n, QKV_GROUP_WIDTH), F32)], axis=0)
        qkv_ref[...] = jnp.transpose(padded)[:, :n]


def _sample_mix_in_call(x, g, win, state_t, cos, sin, wgrp, pscale, wpbr):
    n, d = x.shape
    n_g = len(ATTN_GROUPS)

    def full(shape):
        zeros = (0,) * len(shape)
        return pl.BlockSpec(shape, lambda i: zeros)

    assert n <= V7X_LANES
    out_shape = ([jax.ShapeDtypeStruct((QKV_GROUP_WIDTH, n), F32)] * n_g
                 + [jax.ShapeDtypeStruct((n, POOL_WIDTH), F32),
                    jax.ShapeDtypeStruct((n, d), F32), jax.ShapeDtypeStruct((n, d), F32)])
    ins = (x, g, win, state_t, cos, sin, wgrp, pscale, wpbr)
    resident = sum(_nbytes(a.shape, a.dtype) for a in ins)
    return pl.pallas_call(
        _sample_mix_in_kernel,
        grid=(1,),
        in_specs=[full(a.shape) for a in ins],
        out_specs=[full(o.shape) for o in out_shape],
        out_shape=out_shape,
        compiler_params=pltpu.CompilerParams(dimension_semantics=("arbitrary",),
                                             vmem_limit_bytes=_vmem_limit(resident, 0, 8 << 20)),
        name="sample_mix_in",
    )(*ins)


def _head_sums(x):
    return jnp.concatenate([jnp.sum(x[hd * HEAD_DIM:(hd + 1) * HEAD_DIM], axis=0, keepdims=True)
                            for hd in range(HEADS)], axis=0)


def _head_expand(y):
    return jnp.concatenate([jnp.broadcast_to(y[hd:hd + 1], (HEAD_DIM, y.shape[1])) for hd in range(HEADS)],
                           axis=0)


def _cache_step_kernel(qkv0_ref, qkv1_ref, qkv2_ref, c0_ref, c1_ref, c2_ref,
                       n0_ref, n1_ref, n2_ref, attn_ref):
    step = pl.program_id(0)

    @pl.when(step == 0)
    def _():
        attn_ref[...] = jnp.zeros(attn_ref.shape, F32)

    groups = ((qkv0_ref, c0_ref, n0_ref), (qkv1_ref, c1_ref, n1_ref), (qkv2_ref, c2_ref, n2_ref))
    per_step = c0_ref.shape[0]
    attn = attn_ref[...]
    for j in range(per_step):
        n = step * per_step + j
        request = lax.broadcasted_iota(jnp.int32, qkv0_ref.shape, 1) == n
        cols = [jnp.sum(jnp.where(request, qkv_ref[...], 0.0), axis=1, keepdims=True) for qkv_ref, _, _ in groups]
        for qkv_col, (_, c_ref, new_ref), (window, _) in zip(cols, groups, ATTN_GROUPS):
            lane = lax.broadcasted_iota(jnp.int32, (1, window), 1)
            new_ref[j] = jnp.where(lane == window - 1, qkv_col[GROUP_WIDTH:], pltpu.roll(c_ref[j], window - 1, 1))
        outs, lses = [], []
        for qkv_col, (_, c_ref, _), (window, dil) in zip(cols, groups, ATTN_GROUPS):
            q = qkv_col[:GROUP_WIDTH]
            k_new = qkv_col[GROUP_WIDTH:2 * GROUP_WIDTH]
            v_new = qkv_col[2 * GROUP_WIDTH:]
            kt = c_ref[j, :GROUP_WIDTH, :]
            vt = c_ref[j, GROUP_WIDTH:, :]
            lane = lax.broadcasted_iota(jnp.int32, (1, window), 1)
            sc = jnp.where(lane % dil == 0, _head_sums(kt * q), NEG_INF)
            sc_new = _head_sums(k_new * q)
            mx = jnp.maximum(jnp.max(sc, axis=1, keepdims=True), sc_new)
            e = jnp.exp(sc - mx)
            e_new = jnp.exp(sc_new - mx)
            den = jnp.sum(e, axis=1, keepdims=True) + e_new
            o_un = jnp.sum(vt * _head_expand(e), axis=1, keepdims=True) + _head_expand(e_new) * v_new
            outs.append(o_un / _head_expand(den))
            lses.append(_head_expand(mx + jnp.log(den)))
        attn_col = _merge_groups(outs, lses, jnp.exp)
        attn = jnp.where(lax.broadcasted_iota(jnp.int32, attn_ref.shape, 1) == n, attn_col, attn)
    attn_ref[...] = attn


def _cache_step_call(qkv_cols, windows_t):
    n = qkv_cols[0].shape[1]
    per_step = CACHE_STEP_REQUESTS
    assert n % per_step == 0
    qspec = pl.BlockSpec((QKV_GROUP_WIDTH, n), lambda i: (0, 0))
    cspecs = [pl.BlockSpec((per_step,) + c.shape[1:], lambda i: (i, 0, 0)) for c in windows_t]
    pipelined = (2 * per_step * sum(_nbytes(c.shape[1:], F32) for c in windows_t)
                 + 3 * _nbytes((QKV_GROUP_WIDTH, V7X_LANES), F32))
    res = pl.pallas_call(
        _cache_step_kernel,
        grid=(n // per_step,),
        in_specs=[qspec] * len(qkv_cols) + cspecs,
        out_specs=cspecs + [pl.BlockSpec((GROUP_WIDTH, n), lambda i: (0, 0))],
        out_shape=[jax.ShapeDtypeStruct(c.shape, F32) for c in windows_t]
                  + [jax.ShapeDtypeStruct((GROUP_WIDTH, n), F32)],
        compiler_params=pltpu.CompilerParams(dimension_semantics=("arbitrary",),
                                             vmem_limit_bytes=_vmem_limit(pipelined, 0, 16 << 20)),
        name="cache_step",
    )(*qkv_cols, *windows_t)
    return res[:-1], res[-1]


def _rope_tables(pos):
    half = HEAD_DIM // 2
    inv = ROPE_THETA ** (-jnp.arange(half, dtype=F32) * (2.0 / HEAD_DIM))
    ang = pos.astype(F32)[:, None] * inv[None, :]
    cos, sin = jnp.cos(ang), jnp.sin(ang)
    reps = V7X_LANES // HEAD_DIM
    return (jnp.tile(jnp.concatenate([cos, cos], axis=1), (1, reps)),
            jnp.tile(jnp.concatenate([-sin, sin], axis=1), (1, reps)))


def _window_to_lanes(w):
    n, width = w.shape[:2]
    return jnp.transpose(w, (0, 2, 3, 4, 1)).reshape(n, 2 * GROUP_WIDTH, width)


def _window_from_lanes(wt):
    n, _, width = wt.shape
    return jnp.transpose(wt.reshape(n, 2, HEADS, HEAD_DIM, width), (0, 4, 1, 2, 3))[None]


def kernel(x_prompt, x_sample, state_pool, cache_kv_w128, cache_kv_w512, cache_kv_w2048, norm_ffn1, ffn1_w_gate, ffn1_w_up, ffn1_w_down, norm_mix, w_in, w_pool_grp, pool_scale, w_pool_br, w_attn_br, w_o, norm_ffn2, ffn2_w_gate, ffn2_w_up, ffn2_w_down, norm_final):
    b, s, d = x_prompt.shape
    n, dec_seq, _ = x_sample.shape
    caches = (cache_kv_w128, cache_kv_w512, cache_kv_w2048)
    assert norm_ffn1.shape[0] == 1 and dec_seq == 1, "single layer, single decode token"
    ts = PROMPT_ROW_TILE
    assert s % ts == 0 and ts >= POOL_CARRY
    for cache, (window, dil) in zip(caches, ATTN_GROUPS):
        assert window // dil == KEYS_PER_QUERY and s % (dil * KEYS_PER_QUERY) == 0 and ts % (16 * dil) == 0
        assert cache.shape[2] == window, "decode path expects full windows of history"

    row = lambda v: v.reshape(1, -1)
    bf = lambda w: w.astype(BF16)
    g1, g2, gm, gf = row(norm_ffn1[0]), row(norm_ffn2[0]), row(norm_mix[0]), row(norm_final)
    f1 = (bf(ffn1_w_gate[0]), bf(ffn1_w_up[0]), bf(ffn1_w_down[0]))
    pscale = row(pool_scale[0])

    cos_p, sin_p = _rope_tables(jnp.arange(s, dtype=jnp.int32))
    later = (ffn2_w_gate[0], ffn2_w_up[0], ffn2_w_down[0], w_in[0],
             w_pool_grp[0].reshape(-1, POOL_GROUP), w_pool_br[0], w_attn_br[0], w_o[0])
    x1, xs1, (f2g, f2u, f2d, win, wgrp, wpbr, wabr, wo) = _ffn_call(
        x_prompt.reshape(b * s, d), x_sample.reshape(n, d), g1, *f1, tm=FFN_ROW_TILE, cast=later)
    f2 = (f2g, f2u, f2d)
    wgrp = wgrp.reshape(w_pool_grp.shape[1:])
    (st0, st1, st2, kvt0, kvt1, kvt2, tail, pp, ga) = _prompt_mix_in_call(
        x1.reshape(b, s, d), gm, win, cos_p, sin_p, wgrp, pscale, wpbr, ts)
    group_planes = []
    for streams, (_, dil) in zip((st0, st1, st2), ATTN_GROUPS):
        group_planes += _stream_attn_call(streams, dil, s)
    pool_prompt = tail[None, :, POOL_CARRY - POOL_STATE:, :]

    cos_s, sin_s = _rope_tables(PAST_LEN + jnp.arange(dec_seq, dtype=jnp.int32))
    state_t = jnp.swapaxes(state_pool[0], 0, 1)
    (sqkv0, sqkv1, sqkv2, u_new, spp, sga) = _sample_mix_in_call(
        xs1, gm, win, state_t, cos_s, sin_s, wgrp, pscale, wpbr)
    new_windows, attn_t = _cache_step_call((sqkv0, sqkv1, sqkv2),
                                           [_window_to_lanes(c[0]) for c in caches])
    pool_sample = jnp.swapaxes(jnp.concatenate([state_t[1:], u_new[None]], axis=0), 0, 1)[None]

    y_prompt, y_sample = _mix_out_call(x1, pp.reshape(b * s, d), ga.reshape(b * s, d), tuple(group_planes),
                                       (xs1, spp, sga, attn_t.T.astype(BF16)), wabr, wo, g2, *f2, gf, tm=ts)
    y_prompt = y_prompt.reshape(b, s, d)
    y_sample = y_sample.reshape(n, dec_seq, d)

    return (y_prompt, y_sample, pool_prompt,
            _window_from_lanes(kvt0), _window_from_lanes(kvt1), _window_from_lanes(kvt2),
            pool_sample, *[_window_from_lanes(w) for w in new_windows])
```

```python
import functools

import jax
import jax.numpy as jnp
from jax import lax
from jax.experimental import pallas as pl
from jax.experimental.pallas import tpu as pltpu

F32 = jnp.float32
BF16 = jnp.bfloat16

PAST_LEN = 16384
POOL_WINDOWS = (2, 4, 8, 16)
POOL_GROUP = 128
POOL_WIDTH = POOL_GROUP * len(POOL_WINDOWS)
POOL_STATE = max(POOL_WINDOWS) - 1
POOL_CARRY = 16
POOL_PAD = 8
HEAD_DIM = 64
HEADS = 4
GROUP_WIDTH = HEADS * HEAD_DIM
ATTN_GROUPS = ((128, 1), (512, 4), (2048, 16))
KEYS_PER_QUERY = 128
QKV_GROUP_WIDTH = 3 * GROUP_WIDTH
ROPE_THETA = 10000.0
RMS_EPS = 1e-6
NEG_INF = -1e30
ATTN_SCALE = HEAD_DIM ** -0.5
LOG2_E = 1.4426950408889634

V7X_VMEM_BYTES = 64 * 1024 * 1024
VMEM_LEFT_TO_COMPILER = 4 * 1024 * 1024
V7X_LANES = 128
V7X_MXU_COLS = 256
MAX_FAST_SUBLANE_STRIDE = 4

PROMPT_ROW_TILE = 512
FFN_ROW_TILE = 1024
FFN_CHUNK = 6 * V7X_MXU_COLS
CACHE_STEP_REQUESTS = 2
ATTN_SEQS_PER_STEP = 2


def _vmem_limit(pipelined_bytes, resident_bytes, temp_bytes):
    need = 2 * pipelined_bytes + resident_bytes + temp_bytes
    return int(min(need, V7X_VMEM_BYTES - VMEM_LEFT_TO_COMPILER))


def _nbytes(shape, dtype):
    n = 1
    for s in shape:
        n *= s
    return n * jnp.dtype(dtype).itemsize


def _resident(shape):
    zeros = (0,) * len(shape)
    return pl.BlockSpec(shape, lambda *_: zeros, pipeline_mode=pl.Buffered(1))


def _rmsnorm(x, g):
    ms = jnp.mean(x * x, axis=-1, keepdims=True)
    return x * lax.rsqrt(ms + RMS_EPS) * g


def _ffn_chunks(width):
    return [(lo, min(lo + FFN_CHUNK, width)) for lo in range(0, width, FFN_CHUNK)]


def _swiglu(xn, wg_ref, wu_ref, wd_ref):
    acc = None
    for lo, hi in _ffn_chunks(wg_ref.shape[1]):
        gate = jnp.dot(xn, wg_ref[:, lo:hi], preferred_element_type=F32)
        up = jnp.dot(xn, wu_ref[:, lo:hi], preferred_element_type=F32)
        hid = (gate * jax.nn.sigmoid(gate) * up).astype(BF16)
        part = jnp.dot(hid, wd_ref[lo:hi, :], preferred_element_type=F32)
        acc = part if acc is None else acc + part
    return acc


def _macaron_ffn(x, g_ref, wg_ref, wu_ref, wd_ref):
    xn = _rmsnorm(x, g_ref[...]).astype(BF16)
    return x + 0.5 * _swiglu(xn, wg_ref, wu_ref, wd_ref)


def _ffn_kernel(x_ref, xs_ref, g_ref, wg_ref, wu_ref, wd_ref, *rest):
    n_cast = (len(rest) - 2) // 2
    o_ref, os_ref = rest[n_cast], rest[n_cast + 1]

    @pl.when(pl.program_id(0) == 0)
    def _():
        os_ref[...] = _macaron_ffn(xs_ref[...], g_ref, wg_ref, wu_ref, wd_ref)

    o_ref[...] = _macaron_ffn(x_ref[...], g_ref, wg_ref, wu_ref, wd_ref)
    for src_ref, dst_ref in zip(rest[:n_cast], rest[n_cast + 2:]):
        dst_ref[...] = src_ref[...].astype(BF16)


BF16_SUBLANE_ROWS = 16


def _ffn_call(x, xs, g, wg, wu, wd, tm, cast=()):
    m, d = x.shape
    f = wg.shape[1]
    steps = m // tm
    row = pl.BlockSpec((tm, d), lambda i: (i, 0))
    dec = pl.BlockSpec(xs.shape, lambda i: (0, 0))
    cast_specs, cast_bytes = [], 0
    for w in cast:
        rows = w.shape[0]
        per_step = next(r for r in range(BF16_SUBLANE_ROWS, rows + 1, BF16_SUBLANE_ROWS)
                        if rows % r == 0 and r * steps >= rows)
        last = rows // per_step - 1
        cast_specs.append(pl.BlockSpec((per_step, w.shape[1]), lambda i, last=last: (jnp.minimum(i, last), 0)))
        cast_bytes += _nbytes((per_step, w.shape[1]), F32) + _nbytes((per_step, w.shape[1]), BF16)
    limit = _vmem_limit(2 * _nbytes((tm, d), F32) + 2 * _nbytes(xs.shape, F32) + cast_bytes,
                        3 * _nbytes((d, f), BF16),
                        4 * _nbytes((tm, FFN_CHUNK), F32) + 2 * _nbytes((tm, d), F32))
    res = pl.pallas_call(
        _ffn_kernel,
        grid=(steps,),
        in_specs=[row, dec, _resident((1, d)), _resident((d, f)), _resident((d, f)), _resident((f, d))]
                 + cast_specs,
        out_specs=[row, dec] + cast_specs,
        out_shape=[jax.ShapeDtypeStruct((m, d), F32), jax.ShapeDtypeStruct(xs.shape, F32)]
                  + [jax.ShapeDtypeStruct(w.shape, BF16) for w in cast],
        compiler_params=pltpu.CompilerParams(dimension_semantics=("arbitrary",), vmem_limit_bytes=limit),
        name="ffn",
    )(x, xs, g, wg, wu, wd, *cast)
    return res[0], res[1], res[2:]


def _rope(t, cos, sin_signed, first_half):
    outs = []
    for c in range(t.shape[1] // V7X_LANES):
        tc = t[:, c * V7X_LANES:(c + 1) * V7X_LANES]
        partner = jnp.where(first_half,
                            pltpu.roll(tc, V7X_LANES - HEAD_DIM // 2, 1),
                            pltpu.roll(tc, HEAD_DIM // 2, 1))
        outs.append(tc * cos + partner * sin_signed)
    return jnp.concatenate(outs, axis=1)


def _first_half_mask():
    lane = lax.broadcasted_iota(jnp.int32, (1, V7X_LANES), 1)
    return (lane % HEAD_DIM) < (HEAD_DIM // 2)


def _qkv_group(h, win_ref, g, cos, sin_signed, first_half, q_scale):
    base = POOL_WIDTH + g * QKV_GROUP_WIDTH
    qkv = jnp.dot(h, win_ref[:, base:base + QKV_GROUP_WIDTH], preferred_element_type=F32)
    q = _rope(qkv[:, :GROUP_WIDTH], cos, sin_signed, first_half) * q_scale
    k = _rope(qkv[:, GROUP_WIDTH:2 * GROUP_WIDTH], cos, sin_signed, first_half)
    return jnp.concatenate([q, k, qkv[:, 2 * GROUP_WIDTH:]], axis=1)


def _pool_branch(window_means_minus_tok, wgrp_ref, pscale_ref, wpbr_ref):
    mixed = [jnp.dot(p.astype(BF16), wgrp_ref[gi], preferred_element_type=F32)
             for gi, p in enumerate(window_means_minus_tok)]
    pool = jnp.concatenate(mixed, axis=1) * pscale_ref[...]
    return jnp.dot(pool.astype(BF16), wpbr_ref[...], preferred_element_type=F32)


def _gates(h, win_ref):
    d = win_ref.shape[0]
    base = POOL_WIDTH + len(ATTN_GROUPS) * QKV_GROUP_WIDTH
    gate_pool = jax.nn.sigmoid(jnp.dot(h, win_ref[:, base:base + d], preferred_element_type=F32))
    gate_attn = jax.nn.sigmoid(jnp.dot(h, win_ref[:, base + d:base + 2 * d], preferred_element_type=F32))
    return gate_pool, gate_attn


def _prompt_mix_in_kernel(x_ref, g_ref, win_ref, cos_ref, sin_ref, wgrp_ref, pscale_ref, wpbr_ref,
                          s0_ref, s1_ref, s2_ref,
                          kvt0_ref, kvt1_ref, kvt2_ref, tail_ref, pp_ref, ga_ref,
                          uext_ref, sums_ref, qkv_ref, mid_ref, *, ts):
    j = pl.program_id(1)

    @pl.when(j == 0)
    def _():
        uext_ref[0:POOL_PAD + POOL_CARRY, :] = jnp.zeros((POOL_PAD + POOL_CARRY, POOL_WIDTH), F32)
        sums_ref[:, 0:POOL_PAD, :] = jnp.zeros((sums_ref.shape[0], POOL_PAD, POOL_WIDTH), F32)

    h = _rmsnorm(x_ref[0], g_ref[...]).astype(BF16)

    cos = cos_ref[...]
    sin_signed = sin_ref[...]
    first_half = _first_half_mask()
    outs = ((s0_ref, kvt0_ref), (s1_ref, kvt1_ref), (s2_ref, kvt2_ref))
    planes = QKV_GROUP_WIDTH // V7X_LANES
    slot = 0
    for g in sorted(range(len(ATTN_GROUPS)), key=lambda g: -ATTN_GROUPS[g][1]):
        stream_ref, kvt_ref = outs[g]
        dil = ATTN_GROUPS[g][1]
        qkv = _qkv_group(h, win_ref, g, cos, sin_signed, first_half, ATTN_SCALE * LOG2_E)
        if dil == 1:
            stream_ref[0] = qkv.astype(BF16)
        else:
            for c in range(planes):
                qkv_ref[slot + c] = qkv[:, c * V7X_LANES:(c + 1) * V7X_LANES]
            inner = min(dil, MAX_FAST_SUBLANE_STRIDE)
            outer = dil // inner
            assert outer <= MAX_FAST_SUBLANE_STRIDE and outer * inner == dil
            if outer > 1:
                for c in range(planes):
                    for bb in range(inner):
                        mid_ref[c, bb * (ts // inner):(bb + 1) * (ts // inner), :] = (
                            qkv_ref[slot + c, pl.ds(bb, ts // inner, stride=inner), :])
            for r in range(dil):
                if outer > 1:
                    a, bb = r // inner, r % inner
                    rows = jnp.concatenate(
                        [mid_ref[c, pl.ds(bb * (ts // inner) + a, ts // dil, stride=outer), :]
                         for c in range(planes)], axis=1)
                else:
                    rows = jnp.concatenate([qkv_ref[slot + c, pl.ds(r, ts // dil, stride=dil), :]
                                            for c in range(planes)], axis=1)
                stream_ref[0, :, r * QKV_GROUP_WIDTH:(r + 1) * QKV_GROUP_WIDTH] = rows.astype(BF16)
            slot += planes
        keep = kvt_ref.shape[2]
        kvt_ref[0] = qkv[ts - keep:, GROUP_WIDTH:].T

    u = jnp.dot(h, win_ref[:, :POOL_WIDTH], preferred_element_type=F32)
    ext = POOL_CARRY + ts
    uext_ref[POOL_PAD + POOL_CARRY:POOL_PAD + ext, :] = u
    pos = j * ts + lax.broadcasted_iota(jnp.int32, (ts, 1), 0)
    pooled = []
    for gi, w in enumerate(POOL_WINDOWS):
        lo, hi = gi * POOL_GROUP, (gi + 1) * POOL_GROUP

        def trailing(level, shift, lo=lo, hi=hi):
            start = POOL_PAD - shift
            if level == 0:
                return uext_ref[start:start + ext, lo:hi]
            return sums_ref[level - 1, start:start + ext, lo:hi]

        span, level = 1, 0
        while span < w:
            total = trailing(level, 0) + trailing(level, span)
            span *= 2
            if span < w:
                sums_ref[level, POOL_PAD:POOL_PAD + ext, lo:hi] = total
                level += 1
        cnt = jnp.minimum(pos + 1, w).astype(F32)
        pooled.append(total[POOL_CARRY:, :] / cnt - u[:, lo:hi])
    pool_br = _pool_branch(pooled, wgrp_ref, pscale_ref, wpbr_ref)
    uext_ref[POOL_PAD:POOL_PAD + POOL_CARRY, :] = u[ts - POOL_CARRY:, :]
    tail_ref[0] = u[ts - POOL_CARRY:, :]
    gate_pool, gate_attn = _gates(h, win_ref)
    pp_ref[0] = gate_pool * pool_br
    ga_ref[0] = gate_attn


def _prompt_mix_in_call(x, g, win, cos, sin, wgrp, pscale, wpbr, ts):
    b, s, d = x.shape
    seq_tiles = s // ts
    zw = win.shape[1]

    def tile(width):
        return pl.BlockSpec((1, ts, width), lambda bi, j: (bi, j, 0))

    stream_specs, stream_shapes, kvt_specs, kvt_shapes = [], [], [], []
    for window, dil in ATTN_GROUPS:
        stream_specs.append(pl.BlockSpec((1, ts // dil, dil * QKV_GROUP_WIDTH), lambda bi, j: (bi, j, 0)))
        stream_shapes.append(jax.ShapeDtypeStruct((b, s // dil, dil * QKV_GROUP_WIDTH), BF16))
        keep = min(window, s)
        if keep >= ts:
            first = seq_tiles - keep // ts
            kvt_specs.append(pl.BlockSpec((1, 2 * GROUP_WIDTH, ts),
                                          lambda bi, j, first=first: (bi, 0, jnp.maximum(j - first, 0))))
        else:
            kvt_specs.append(pl.BlockSpec((1, 2 * GROUP_WIDTH, keep), lambda bi, j: (bi, 0, 0)))
        kvt_shapes.append(jax.ShapeDtypeStruct((b, 2 * GROUP_WIDTH, keep), F32))
    n_g = len(ATTN_GROUPS)
    out_specs = (stream_specs + kvt_specs
                 + [pl.BlockSpec((1, POOL_CARRY, POOL_WIDTH), lambda bi, j: (bi, 0, 0)), tile(d), tile(d)])
    out_shape = (stream_shapes + kvt_shapes
                 + [jax.ShapeDtypeStruct((b, POOL_CARRY, POOL_WIDTH), F32),
                    jax.ShapeDtypeStruct((b, s, d), F32), jax.ShapeDtypeStruct((b, s, d), F32)])
    table = pl.BlockSpec((ts, V7X_LANES), lambda bi, j: (j, 0))
    pipelined = (3 * _nbytes((ts, d), F32) + n_g * _nbytes((ts, QKV_GROUP_WIDTH), BF16)
                 + n_g * _nbytes((ts, 2 * GROUP_WIDTH), F32) + 2 * _nbytes((ts, V7X_LANES), F32))
    resident = _nbytes((d, zw), BF16) + _nbytes(wpbr.shape, BF16) + _nbytes(wgrp.shape, BF16)
    n_dilated = sum(1 for _, dil in ATTN_GROUPS if dil != 1)
    pool_rows = POOL_PAD + POOL_CARRY + ts
    pool_levels = max(POOL_WINDOWS).bit_length() - 2
    assert all(w & (w - 1) == 0 for w in POOL_WINDOWS) and max(POOL_WINDOWS) // 2 <= POOL_PAD
    scratch = [pltpu.VMEM((pool_rows, POOL_WIDTH), F32),
               pltpu.VMEM((pool_levels, pool_rows, POOL_WIDTH), F32),
               pltpu.VMEM((n_dilated * QKV_GROUP_WIDTH // V7X_LANES, ts, V7X_LANES), F32),
               pltpu.VMEM((QKV_GROUP_WIDTH // V7X_LANES, ts, V7X_LANES), F32)]
    temps = ((1 + pool_levels) * _nbytes((pool_rows, POOL_WIDTH), F32)
             + (n_dilated + 1) * _nbytes((ts, QKV_GROUP_WIDTH), F32)
             + 6 * _nbytes((ts, d), F32))
    return pl.pallas_call(
        functools.partial(_prompt_mix_in_kernel, ts=ts),
        grid=(b, seq_tiles),
        in_specs=[tile(d), _resident((1, d)), _resident((d, zw)), table, table,
                  _resident(wgrp.shape), _resident((1, POOL_WIDTH)), _resident(wpbr.shape)],
        out_specs=out_specs,
        out_shape=out_shape,
        scratch_shapes=scratch,
        compiler_params=pltpu.CompilerParams(dimension_semantics=("arbitrary", "arbitrary"),
                                             vmem_limit_bytes=_vmem_limit(pipelined, resident, temps)),
        name="prompt_mix_in",
    )(x, g, win, cos, sin, wgrp, pscale, wpbr)


def _head_lane_masks():
    lane = lax.broadcasted_iota(jnp.int32, (1, GROUP_WIDTH), 1)
    return [(lane // HEAD_DIM) == hd for hd in range(HEADS)]


def _band_scores(q, kt, head_ones):
    q_heads = jnp.concatenate([q * one for one in head_ones], axis=0)
    return jnp.dot(q_heads, kt, preferred_element_type=F32)


def _band_softmax(sc, bias_ref, keys):
    blk = sc.shape[0] // HEADS
    heads = []
    for hd in range(HEADS):
        s_h = sc[hd * blk:(hd + 1) * blk, :] + bias_ref[:, :keys]
        mx = jnp.max(s_h, axis=1, keepdims=True)
        e = jnp.exp2(s_h - mx)
        heads.append((e.astype(BF16), mx, jnp.sum(e, axis=1, keepdims=True)))
    return heads


def _band_outputs(heads, v, low_head):
    outs, lses = [], []
    for hd, (e, mx, den) in enumerate(heads):
        tile = slice((hd * HEAD_DIM // V7X_LANES) * V7X_LANES, (hd * HEAD_DIM // V7X_LANES + 1) * V7X_LANES)
        o_un = jnp.dot(e, v[:, tile], preferred_element_type=F32)
        outs.append(o_un * (1.0 / den))
        lses.append(jnp.broadcast_to(mx + jnp.log2(den), (e.shape[0], V7X_LANES)))
    pairs = range(0, HEADS, V7X_LANES // HEAD_DIM)
    return ([jnp.where(low_head, outs[hd], outs[hd + 1]) for hd in pairs],
            [jnp.where(low_head, lses[hd], lses[hd + 1]) for hd in pairs])


ATTN_PLANES = GROUP_WIDTH // V7X_LANES
ATTN_BLOCKS_SIDE_BY_SIDE = {1: 1, 4: 3, 16: 2}


def _stream_attn_kernel(s_ref, o_ref, l_ref, first_bias_ref, band_bias_ref, *, dil, seq):
    blk = KEYS_PER_QUERY

    @pl.when(pl.program_id(0) == 0)
    def _():
        qi = lax.broadcasted_iota(jnp.int32, (blk, 1), 0)
        first = lax.broadcasted_iota(jnp.int32, (1, blk), 1) <= qi
        dist = qi + blk - lax.broadcasted_iota(jnp.int32, (1, 2 * blk), 1)
        first_bias_ref[...] = jnp.where(first, 0.0, NEG_INF)
        band_bias_ref[...] = jnp.where((dist >= 0) & (dist <= blk), 0.0, NEG_INF)

    head_ones = [jnp.where(m, 1.0, 0.0).astype(BF16) for m in _head_lane_masks()]
    low_head = lax.broadcasted_iota(jnp.int32, (1, V7X_LANES), 1) < HEAD_DIM
    blocks = [(n, r, i) for n in range(s_ref.shape[0]) for r in range(dil) for i in range(seq // (dil * blk))]
    side = ATTN_BLOCKS_SIDE_BY_SIDE.get(dil, 1)
    for first in range(0, len(blocks), side):
        operands = []
        for n, r, i in blocks[first:first + side]:
            base = r * QKV_GROUP_WIDTH
            krows = slice(max(i - 1, 0) * blk, (i + 1) * blk)
            operands.append((s_ref[n, i * blk:(i + 1) * blk, base:base + GROUP_WIDTH],
                             s_ref[n, krows, base + GROUP_WIDTH:base + 2 * GROUP_WIDTH].T,
                             s_ref[n, krows, base + 2 * GROUP_WIDTH:base + 3 * GROUP_WIDTH]))
        scores = [_band_scores(q, kt, head_ones) for q, kt, _ in operands]
        softmaxes = [_band_softmax(sc, first_bias_ref if i == 0 else band_bias_ref, v.shape[0])
                     for sc, (_, _, v), (_, _, i) in zip(scores, operands, blocks[first:first + side])]
        for heads, (_, _, v), (n, r, i) in zip(softmaxes, operands, blocks[first:first + side]):
            outs, lses = _band_outputs(heads, v, low_head)
            for c in range(ATTN_PLANES):
                o_ref[n, c, r, i * blk:(i + 1) * blk, :] = outs[c]
                l_ref[n, c, r, i * blk:(i + 1) * blk, :] = lses[c]


def _merge_groups(outs, lses, exp_fn):
    mx = functools.reduce(jnp.maximum, lses)
    ws = [exp_fn(l - mx) for l in lses]
    num = functools.reduce(lambda a, b: a + b, [w * o for w, o in zip(ws, outs)])
    return num / functools.reduce(lambda a, b: a + b, ws)


def _stream_attn_call(streams, dil, seq):
    b = streams.shape[0]
    per_step = ATTN_SEQS_PER_STEP if b % ATTN_SEQS_PER_STEP == 0 else 1

    def whole(a):
        zeros = (0,) * (len(a.shape) - 1)
        return pl.BlockSpec((per_step,) + tuple(a.shape[1:]), lambda bi: (bi,) + zeros)

    planes = jax.ShapeDtypeStruct((b, ATTN_PLANES, dil, seq // dil, V7X_LANES), F32)
    pipelined = per_step * (_nbytes((seq, QKV_GROUP_WIDTH), BF16) + 2 * _nbytes((seq, GROUP_WIDTH), F32))
    scratch = [pltpu.VMEM((KEYS_PER_QUERY, KEYS_PER_QUERY), F32),
               pltpu.VMEM((KEYS_PER_QUERY, 2 * KEYS_PER_QUERY), F32)]
    scratch_bytes = 3 * _nbytes((KEYS_PER_QUERY, KEYS_PER_QUERY), F32)
    return pl.pallas_call(
        functools.partial(_stream_attn_kernel, dil=dil, seq=seq),
        grid=(b // per_step,),
        in_specs=[whole(streams)],
        out_specs=[whole(planes), whole(planes)],
        out_shape=[planes, planes],
        scratch_shapes=scratch,
        compiler_params=pltpu.CompilerParams(dimension_semantics=("arbitrary",),
                                             vmem_limit_bytes=_vmem_limit(pipelined, scratch_bytes, 16 << 20)),
        name=f"stream_attn_d{dil}",
    )(streams)


def _token_order(ref, plane, stage_ref, slot):
    dil, rows = ref.shape[2], ref.shape[3]
    if dil == 1:
        return ref[0, plane, 0]
    if dil <= MAX_FAST_SUBLANE_STRIDE:
        for r in range(dil):
            stage_ref[slot, pl.ds(r, rows, stride=dil), :] = ref[0, plane, r]
        return stage_ref[slot]
    inner = MAX_FAST_SUBLANE_STRIDE
    outer = dil // inner
    assert outer <= MAX_FAST_SUBLANE_STRIDE and outer * inner == dil
    tmp = stage_ref.shape[0] - 1
    mid = rows * outer
    for b in range(inner):
        for a in range(outer):
            stage_ref[tmp, pl.ds(b * mid + a, rows, stride=outer), :] = ref[0, plane, inner * a + b]
    for b in range(inner):
        stage_ref[slot, pl.ds(b, mid, stride=inner), :] = stage_ref[tmp, b * mid:(b + 1) * mid, :]
    return stage_ref[slot]


def _mix_out_rows(x, pp, ga, attn, wabr_ref, wo_ref, g2_ref, wg_ref, wu_ref, wd_ref, gf_ref):
    attn_br = jnp.dot(attn, wabr_ref[...], preferred_element_type=F32)
    merged = pp + ga * attn_br
    x = x + jnp.dot(merged.astype(BF16), wo_ref[...], preferred_element_type=F32)
    x = _macaron_ffn(x, g2_ref, wg_ref, wu_ref, wd_ref)
    return _rmsnorm(x, gf_ref[...])


def _mix_out_kernel(x_ref, pp_ref, ga_ref, *rest, n_groups):
    group_refs, rest = rest[:2 * n_groups], rest[2 * n_groups:]
    weights, rest = rest[:7], rest[7:]
    xs_ref, spp_ref, sga_ref, sattn_ref, y_ref, ys_ref, stage_ref = rest

    @pl.when(pl.program_id(0) == 0)
    def _():
        ys_ref[...] = _mix_out_rows(xs_ref[...], spp_ref[...], sga_ref[...], sattn_ref[...], *weights)

    tiles = []
    for c in range(ATTN_PLANES):
        outs = [_token_order(group_refs[2 * g], c, stage_ref, 2 * g) for g in range(n_groups)]
        lses = [_token_order(group_refs[2 * g + 1], c, stage_ref, 2 * g + 1) for g in range(n_groups)]
        tiles.append(_merge_groups(outs, lses, jnp.exp2).astype(BF16))
    attn = jnp.concatenate(tiles, axis=1)
    y_ref[...] = _mix_out_rows(x_ref[...], pp_ref[...], ga_ref[...], attn, *weights)


def _mix_out_call(x, pp, ga, group_planes, decode, wabr, wo, g2, wg, wu, wd, gf, tm):
    m, d = x.shape
    f = wg.shape[1]
    n_tiles = m // tm
    row = pl.BlockSpec((tm, d), lambda i: (i, 0))
    plane_specs = []
    for p in group_planes:
        dil, seq_rows = p.shape[2], p.shape[3]
        tiles = seq_rows * dil // tm
        plane_specs.append(pl.BlockSpec((1, ATTN_PLANES, dil, tm // dil, V7X_LANES),
                                        lambda i, tiles=tiles: (i // tiles, 0, 0, i % tiles, 0)))

    def whole(a):
        return pl.BlockSpec(a.shape, lambda i: (0, 0))

    n_dec = decode[0].shape[0]
    pipelined = (4 * _nbytes((tm, d), F32) + len(group_planes) * _nbytes((tm, GROUP_WIDTH), F32)
                 + 4 * _nbytes((n_dec, d), F32))
    scratch = [pltpu.VMEM((len(group_planes) + 1, tm, V7X_LANES), F32)]
    resident = 3 * _nbytes((d, f), BF16) + _nbytes((d, d), BF16) + _nbytes((GROUP_WIDTH, d), BF16)
    temps = 4 * _nbytes((tm, FFN_CHUNK), F32) + 4 * _nbytes((tm, d), F32)
    return pl.pallas_call(
        functools.partial(_mix_out_kernel, n_groups=len(group_planes) // 2),
        grid=(n_tiles,),
        in_specs=[row, row, row] + plane_specs
                 + [_resident((GROUP_WIDTH, d)), _resident((d, d)), _resident((1, d)),
                    _resident((d, f)), _resident((d, f)), _resident((f, d)), _resident((1, d))]
                 + [whole(a) for a in decode],
        out_specs=[row, pl.BlockSpec((n_dec, d), lambda i: (0, 0))],
        out_shape=[jax.ShapeDtypeStruct((m, d), F32), jax.ShapeDtypeStruct((n_dec, d), F32)],
        scratch_shapes=scratch,
        compiler_params=pltpu.CompilerParams(dimension_semantics=("arbitrary",),
                                             vmem_limit_bytes=_vmem_limit(pipelined, resident, temps)),
        name="mix_out",
    )(x, pp, ga, *group_planes, wabr, wo, g2, wg, wu, wd, gf, *decode)


def _sample_mix_in_kernel(x_ref, g_ref, win_ref, state_ref, cos_ref, sin_ref, wgrp_ref, pscale_ref, wpbr_ref,
                          qkv0_ref, qkv1_ref, qkv2_ref, u_ref, pp_ref, ga_ref):
    h = _rmsnorm(x_ref[...], g_ref[...]).astype(BF16)
    u = jnp.dot(h, win_ref[:, :POOL_WIDTH], preferred_element_type=F32)
    u_ref[...] = u
    pooled = []
    for gi, w in enumerate(POOL_WINDOWS):
        lo, hi = gi * POOL_GROUP, (gi + 1) * POOL_GROUP
        tok = u[:, lo:hi]
        total = tok
        for back in range(1, w):
            total = total + state_ref[POOL_STATE - back, :, lo:hi]
        pooled.append(total / float(min(PAST_LEN + 1, w)) - tok)
    pool_br = _pool_branch(pooled, wgrp_ref, pscale_ref, wpbr_ref)
    gate_pool, gate_attn = _gates(h, win_ref)
    pp_ref[...] = gate_pool * pool_br
    ga_ref[...] = gate_attn
    first_half = _first_half_mask()
    n = x_ref.shape[0]
    for g, qkv_ref in enumerate((qkv0_ref, qkv1_ref, qkv2_ref)):
        qkv = _qkv_group(h, win_ref, g, cos_ref[...], sin_ref[...], first_half, ATTN_SCALE)
        padded = jnp.concatenate([qkv, jnp.zeros((V7X_LANES - n, QKV_GROUP_WIDTH), F32)], axis=0)
        qkv_ref[...] = jnp.transpose(padded)[:, :n]


def _sample_mix_in_call(x, g, win, state_t, cos, sin, wgrp, pscale, wpbr):
    n, d = x.shape
    n_g = len(ATTN_GROUPS)

    def full(shape):
        zeros = (0,) * len(shape)
        return pl.BlockSpec(shape, lambda i: zeros)

    assert n <= V7X_LANES
    out_shape = ([jax.ShapeDtypeStruct((QKV_GROUP_WIDTH, n), F32)] * n_g
                 + [jax.ShapeDtypeStruct((n, POOL_WIDTH), F32),
                    jax.ShapeDtypeStruct((n, d), F32), jax.ShapeDtypeStruct((n, d), F32)])
    ins = (x, g, win, state_t, cos, sin, wgrp, pscale, wpbr)
    resident = sum(_nbytes(a.shape, a.dtype) for a in ins)
    return pl.pallas_call(
        _sample_mix_in_kernel,
        grid=(1,),
        in_specs=[full(a.shape) for a in ins],
        out_specs=[full(o.shape) for o in out_shape],
        out_shape=out_shape,
        compiler_params=pltpu.CompilerParams(dimension_semantics=("arbitrary",),
                                             vmem_limit_bytes=_vmem_limit(resident, 0, 8 << 20)),
        name="sample_mix_in",
    )(*ins)


def _head_sums(x):
    return jnp.concatenate([jnp.sum(x[hd * HEAD_DIM:(hd + 1) * HEAD_DIM], axis=0, keepdims=True)
                            for hd in range(HEADS)], axis=0)


def _head_expand(y):
    return jnp.concatenate([jnp.broadcast_to(y[hd:hd + 1], (HEAD_DIM, y.shape[1])) for hd in range(HEADS)],
                           axis=0)


def _cache_step_kernel(qkv0_ref, qkv1_ref, qkv2_ref, c0_ref, c1_ref, c2_ref,
                       n0_ref, n1_ref, n2_ref, attn_ref):
    step = pl.program_id(0)

    @pl.when(step == 0)
    def _():
        attn_ref[...] = jnp.zeros(attn_ref.shape, F32)

    groups = ((qkv0_ref, c0_ref, n0_ref), (qkv1_ref, c1_ref, n1_ref), (qkv2_ref, c2_ref, n2_ref))
    per_step = c0_ref.shape[0]
    attn = attn_ref[...]
    for j in range(per_step):
        n = step * per_step + j
        request = lax.broadcasted_iota(jnp.int32, qkv0_ref.shape, 1) == n
        cols = [jnp.sum(jnp.where(request, qkv_ref[...], 0.0), axis=1, keepdims=True) for qkv_ref, _, _ in groups]
        for qkv_col, (_, c_ref, new_ref), (window, _) in zip(cols, groups, ATTN_GROUPS):
            lane = lax.broadcasted_iota(jnp.int32, (1, window), 1)
            new_ref[j] = jnp.where(lane == window - 1, qkv_col[GROUP_WIDTH:], pltpu.roll(c_ref[j], window - 1, 1))
        outs, lses = [], []
        for qkv_col, (_, c_ref, _), (window, dil) in zip(cols, groups, ATTN_GROUPS):
            q = qkv_col[:GROUP_WIDTH]
            k_new = qkv_col[GROUP_WIDTH:2 * GROUP_WIDTH]
            v_new = qkv_col[2 * GROUP_WIDTH:]
            kt = c_ref[j, :GROUP_WIDTH, :]
            vt = c_ref[j, GROUP_WIDTH:, :]
            lane = lax.broadcasted_iota(jnp.int32, (1, window), 1)
            sc = jnp.where(lane % dil == 0, _head_sums(kt * q), NEG_INF)
            sc_new = _head_sums(k_new * q)
            mx = jnp.maximum(jnp.max(sc, axis=1, keepdims=True), sc_new)
            e = jnp.exp(sc - mx)
            e_new = jnp.exp(sc_new - mx)
            den = jnp.sum(e, axis=1, keepdims=True) + e_new
            o_un = jnp.sum(vt * _head_expand(e), axis=1, keepdims=True) + _head_expand(e_new) * v_new
            outs.append(o_un / _head_expand(den))
            lses.append(_head_expand(mx + jnp.log(den)))
        attn_col = _merge_groups(outs, lses, jnp.exp)
        attn = jnp.where(lax.broadcasted_iota(jnp.int32, attn_ref.shape, 1) == n, attn_col, attn)
    attn_ref[...] = attn


def _cache_step_call(qkv_cols, windows_t):
    n = qkv_cols[0].shape[1]
    per_step = CACHE_STEP_REQUESTS
    assert n % per_step == 0
    qspec = pl.BlockSpec((QKV_GROUP_WIDTH, n), lambda i: (0, 0))
    cspecs = [pl.BlockSpec((per_step,) + c.shape[1:], lambda i: (i, 0, 0)) for c in windows_t]
    pipelined = (2 * per_step * sum(_nbytes(c.shape[1:], F32) for c in windows_t)
                 + 3 * _nbytes((QKV_GROUP_WIDTH, V7X_LANES), F32))
    res = pl.pallas_call(
        _cache_step_kernel,
        grid=(n // per_step,),
        in_specs=[qspec] * len(qkv_cols) + cspecs,
        out_specs=cspecs + [pl.BlockSpec((GROUP_WIDTH, n), lambda i: (0, 0))],
        out_shape=[jax.ShapeDtypeStruct(c.shape, F32) for c in windows_t]
                  + [jax.ShapeDtypeStruct((GROUP_WIDTH, n), F32)],
        compiler_params=pltpu.CompilerParams(dimension_semantics=("arbitrary",),
                                             vmem_limit_bytes=_vmem_limit(pipelined, 0, 16 << 20)),
        name="cache_step",
    )(*qkv_cols, *windows_t)
    return res[:-1], res[-1]


def _rope_tables(pos):
    half = HEAD_DIM // 2
    inv = ROPE_THETA ** (-jnp.arange(half, dtype=F32) * (2.0 / HEAD_DIM))
    ang = pos.astype(F32)[:, None] * inv[None, :]
    cos, sin = jnp.cos(ang), jnp.sin(ang)
    reps = V7X_LANES // HEAD_DIM
    return (jnp.tile(jnp.concatenate([cos, cos], axis=1), (1, reps)),
            jnp.tile(jnp.concatenate([-sin, sin], axis=1), (1, reps)))


def _window_to_lanes(w):
    n, width = w.shape[:2]
    return jnp.transpose(w, (0, 2, 3, 4, 1)).reshape(n, 2 * GROUP_WIDTH, width)


def _window_from_lanes(wt):
    n, _, width = wt.shape
    return jnp.transpose(wt.reshape(n, 2, HEADS, HEAD_DIM, width), (0, 4, 1, 2, 3))[None]


def kernel(x_prompt, x_sample, state_pool, cache_kv_w128, cache_kv_w512, cache_kv_w2048, norm_ffn1, ffn1_w_gate, ffn1_w_up, ffn1_w_down, norm_mix, w_in, w_pool_grp, pool_scale, w_pool_br, w_attn_br, w_o, norm_ffn2, ffn2_w_gate, ffn2_w_up, ffn2_w_down, norm_final):
    b, s, d = x_prompt.shape
    n, dec_seq, _ = x_sample.shape
    caches = (cache_kv_w128, cache_kv_w512, cache_kv_w2048)
    assert norm_ffn1.shape[0] == 1 and dec_seq == 1, "single layer, single decode token"
    ts = PROMPT_ROW_TILE
    assert s % ts == 0 and ts >= POOL_CARRY
    for cache, (window, dil) in zip(caches, ATTN_GROUPS):
        assert window // dil == KEYS_PER_QUERY and s % (dil * KEYS_PER_QUERY) == 0 and ts % (16 * dil) == 0
        assert cache.shape[2] == window, "decode path expects full windows of history"

    row = lambda v: v.reshape(1, -1)
    bf = lambda w: w.astype(BF16)
    g1, g2, gm, gf = row(norm_ffn1[0]), row(norm_ffn2[0]), row(norm_mix[0]), row(norm_final)
    f1 = (bf(ffn1_w_gate[0]), bf(ffn1_w_up[0]), bf(ffn1_w_down[0]))
    pscale = row(pool_scale[0])

    cos_p, sin_p = _rope_tables(jnp.arange(s, dtype=jnp.int32))
    later = (ffn2_w_gate[0], ffn2_w_up[0], ffn2_w_down[0], w_in[0],
             w_pool_grp[0].reshape(-1, POOL_GROUP), w_pool_br[0], w_attn_br[0], w_o[0])
    x1, xs1, (f2g, f2u, f2d, win, wgrp, wpbr, wabr, wo) = _ffn_call(
        x_prompt.reshape(b * s, d), x_sample.reshape(n, d), g1, *f1, tm=FFN_ROW_TILE, cast=later)
    f2 = (f2g, f2u, f2d)
    wgrp = wgrp.reshape(w_pool_grp.shape[1:])
    (st0, st1, st2, kvt0, kvt1, kvt2, tail, pp, ga) = _prompt_mix_in_call(
        x1.reshape(b, s, d), gm, win, cos_p, sin_p, wgrp, pscale, wpbr, ts)
    group_planes = []
    for streams, (_, dil) in zip((st0, st1, st2), ATTN_GROUPS):
        group_planes += _stream_attn_call(streams, dil, s)
    pool_prompt = tail[None, :, POOL_CARRY - POOL_STATE:, :]

    cos_s, sin_s = _rope_tables(PAST_LEN + jnp.arange(dec_seq, dtype=jnp.int32))
    state_t = jnp.swapaxes(state_pool[0], 0, 1)
    (sqkv0, sqkv1, sqkv2, u_new, spp, sga) = _sample_mix_in_call(
        xs1, gm, win, state_t, cos_s, sin_s, wgrp, pscale, wpbr)
    new_windows, attn_t = _cache_step_call((sqkv0, sqkv1, sqkv2),
                                           [_window_to_lanes(c[0]) for c in caches])
    pool_sample = jnp.swapaxes(jnp.concatenate([state_t[1:], u_new[None]], axis=0), 0, 1)[None]

    y_prompt, y_sample = _mix_out_call(x1, pp.reshape(b * s, d), ga.reshape(b * s, d), tuple(group_planes),
                                       (xs1, spp, sga, attn_t.T.astype(BF16)), wabr, wo, g2, *f2, gf, tm=ts)
    y_prompt = y_prompt.reshape(b, s, d)
    y_sample = y_sample.reshape(n, dec_seq, d)

    return (y_prompt, y_sample, pool_prompt,
            _window_from_lanes(kvt0), _window_from_lanes(kvt1), _window_from_lanes(kvt2),
            pool_sample, *[_window_from_lanes(w) for w in new_windows])
```

```python
import functools

import jax
import jax.numpy as jnp
from jax import lax
from jax.experimental import pallas as pl
from jax.experimental.pallas import tpu as pltpu

F32 = jnp.float32
BF16 = jnp.bfloat16

PAST_LEN = 16384
POOL_WINDOWS = (2, 4, 8, 16)
POOL_GROUP = 128
POOL_WIDTH = POOL_GROUP * len(POOL_WINDOWS)
POOL_STATE = max(POOL_WINDOWS) - 1
POOL_CARRY = 16
POOL_PAD = 8
HEAD_DIM = 64
HEADS = 4
GROUP_WIDTH = HEADS * HEAD_DIM
ATTN_GROUPS = ((128, 1), (512, 4), (2048, 16))
KEYS_PER_QUERY = 128
QKV_GROUP_WIDTH = 3 * GROUP_WIDTH
ROPE_THETA = 10000.0
RMS_EPS = 1e-6
NEG_INF = -1e30
ATTN_SCALE = HEAD_DIM ** -0.5
LOG2_E = 1.4426950408889634

V7X_VMEM_BYTES = 64 * 1024 * 1024
VMEM_LEFT_TO_COMPILER = 4 * 1024 * 1024
V7X_LANES = 128
V7X_MXU_COLS = 256
MAX_FAST_SUBLANE_STRIDE = 4

PROMPT_ROW_TILE = 512
FFN_ROW_TILE = 1024
FFN_CHUNK = 6 * V7X_MXU_COLS
CACHE_STEP_REQUESTS = 2
ATTN_SEQS_PER_STEP = 2


def _vmem_limit(pipelined_bytes, resident_bytes, temp_bytes):
    need = 2 * pipelined_bytes + resident_bytes + temp_bytes
    return int(min(need, V7X_VMEM_BYTES - VMEM_LEFT_TO_COMPILER))


def _nbytes(shape, dtype):
    n = 1
    for s in shape:
        n *= s
    return n * jnp.dtype(dtype).itemsize


def _resident(shape):
    zeros = (0,) * len(shape)
    return pl.BlockSpec(shape, lambda *_: zeros, pipeline_mode=pl.Buffered(1))


def _rmsnorm(x, g):
    ms = jnp.mean(x * x, axis=-1, keepdims=True)
    return x * lax.rsqrt(ms + RMS_EPS) * g


def _ffn_chunks(width):
    return [(lo, min(lo + FFN_CHUNK, width)) for lo in range(0, width, FFN_CHUNK)]


def _swiglu(xn, wg_ref, wu_ref, wd_ref):
    acc = None
    for lo, hi in _ffn_chunks(wg_ref.shape[1]):
        gate = jnp.dot(xn, wg_ref[:, lo:hi], preferred_element_type=F32)
        up = jnp.dot(xn, wu_ref[:, lo:hi], preferred_element_type=F32)
        hid = (gate * jax.nn.sigmoid(gate) * up).astype(BF16)
        part = jnp.dot(hid, wd_ref[lo:hi, :], preferred_element_type=F32)
        acc = part if acc is None else acc + part
    return acc


def _macaron_ffn(x, g_ref, wg_ref, wu_ref, wd_ref):
    xn = _rmsnorm(x, g_ref[...]).astype(BF16)
    return x + 0.5 * _swiglu(xn, wg_ref, wu_ref, wd_ref)


WEIGHT_LOAD_ROWS = 128


def _load_as_bf16(w_hbm, w_ref, stage_ref, sem):
    rows = WEIGHT_LOAD_ROWS
    width = w_hbm.shape[1]
    chunks = w_hbm.shape[0] // rows

    def copy(k):
        return pltpu.make_async_copy(w_hbm.at[pl.ds(k * rows, rows), :], stage_ref.at[k % 2, :, pl.ds(0, width)],
                                     sem.at[k % 2])

    copy(0).start()
    for k in range(chunks):
        if k + 1 < chunks:
            copy(k + 1).start()
        copy(k).wait()
        w_ref[k * rows:(k + 1) * rows, :] = stage_ref[k % 2, :, 0:width].astype(BF16)


def _ffn_kernel(x_ref, xs_ref, g_ref, wg_hbm, wu_hbm, wd_hbm, *rest):
    wg_ref, wu_ref, wd_ref, stage_ref, sem = rest[-5:]
    rest = rest[:-5]
    n_cast = (len(rest) - 2) // 2
    o_ref, os_ref = rest[n_cast], rest[n_cast + 1]

    @pl.when(pl.program_id(0) == 0)
    def _():
        for w_hbm, w_ref in ((wg_hbm, wg_ref), (wu_hbm, wu_ref), (wd_hbm, wd_ref)):
            _load_as_bf16(w_hbm, w_ref, stage_ref, sem)
        os_ref[...] = _macaron_ffn(xs_ref[...], g_ref, wg_ref, wu_ref, wd_ref)

    o_ref[...] = _macaron_ffn(x_ref[...], g_ref, wg_ref, wu_ref, wd_ref)
    for src_ref, dst_ref in zip(rest[:n_cast], rest[n_cast + 2:]):
        dst_ref[...] = src_ref[...].astype(BF16)


BF16_SUBLANE_ROWS = 16


def _ffn_call(x, xs, g, wg, wu, wd, tm, cast=()):
    m, d = x.shape
    f = wg.shape[1]
    steps = m // tm
    row = pl.BlockSpec((tm, d), lambda i: (i, 0))
    dec = pl.BlockSpec(xs.shape, lambda i: (0, 0))
    cast_specs, cast_bytes = [], 0
    for w in cast:
        rows = w.shape[0]
        per_step = next(r for r in range(BF16_SUBLANE_ROWS, rows + 1, BF16_SUBLANE_ROWS)
                        if rows % r == 0 and r * steps >= rows)
        last = rows // per_step - 1
        cast_specs.append(pl.BlockSpec((per_step, w.shape[1]), lambda i, last=last: (jnp.minimum(i, last), 0)))
        cast_bytes += _nbytes((per_step, w.shape[1]), F32) + _nbytes((per_step, w.shape[1]), BF16)
    stage_shape = (2, WEIGHT_LOAD_ROWS, max(d, f))
    assert d % WEIGHT_LOAD_ROWS == 0 and f % WEIGHT_LOAD_ROWS == 0 and wg.dtype == F32
    limit = _vmem_limit(2 * _nbytes((tm, d), F32) + 2 * _nbytes(xs.shape, F32) + cast_bytes,
                        3 * _nbytes((d, f), BF16) + _nbytes(stage_shape, F32),
                        4 * _nbytes((tm, FFN_CHUNK), F32) + 2 * _nbytes((tm, d), F32))
    hbm = pl.BlockSpec(memory_space=pl.ANY)
    res = pl.pallas_call(
        _ffn_kernel,
        grid=(steps,),
        in_specs=[row, dec, _resident((1, d)), hbm, hbm, hbm] + cast_specs,
        out_specs=[row, dec] + cast_specs,
        out_shape=[jax.ShapeDtypeStruct((m, d), F32), jax.ShapeDtypeStruct(xs.shape, F32)]
                  + [jax.ShapeDtypeStruct(w.shape, BF16) for w in cast],
        scratch_shapes=[pltpu.VMEM((d, f), BF16), pltpu.VMEM((d, f), BF16), pltpu.VMEM((f, d), BF16),
                        pltpu.VMEM(stage_shape, F32), pltpu.SemaphoreType.DMA((2,))],
        compiler_params=pltpu.CompilerParams(dimension_semantics=("arbitrary",), vmem_limit_bytes=limit),
        name="ffn",
    )(x, xs, g, wg, wu, wd, *cast)
    return res[0], res[1], res[2:]


def _rope(t, cos, sin_signed, first_half):
    outs = []
    for c in range(t.shape[1] // V7X_LANES):
        tc = t[:, c * V7X_LANES:(c + 1) * V7X_LANES]
        partner = jnp.where(first_half,
                            pltpu.roll(tc, V7X_LANES - HEAD_DIM // 2, 1),
                            pltpu.roll(tc, HEAD_DIM // 2, 1))
        outs.append(tc * cos + partner * sin_signed)
    return jnp.concatenate(outs, axis=1)


def _first_half_mask():
    lane = lax.broadcasted_iota(jnp.int32, (1, V7X_LANES), 1)
    return (lane % HEAD_DIM) < (HEAD_DIM // 2)


def _qkv_group(h, win_ref, g, cos, sin_signed, first_half, q_scale):
    base = POOL_WIDTH + g * QKV_GROUP_WIDTH
    qkv = jnp.dot(h, win_ref[:, base:base + QKV_GROUP_WIDTH], preferred_element_type=F32)
    q = _rope(qkv[:, :GROUP_WIDTH], cos, sin_signed, first_half) * q_scale
    k = _rope(qkv[:, GROUP_WIDTH:2 * GROUP_WIDTH], cos, sin_signed, first_half)
    return jnp.concatenate([q, k, qkv[:, 2 * GROUP_WIDTH:]], axis=1)


def _pool_branch(window_means_minus_tok, wgrp_ref, pscale_ref, wpbr_ref):
    mixed = [jnp.dot(p.astype(BF16), wgrp_ref[gi], preferred_element_type=F32)
             for gi, p in enumerate(window_means_minus_tok)]
    pool = jnp.concatenate(mixed, axis=1) * pscale_ref[...]
    return jnp.dot(pool.astype(BF16), wpbr_ref[...], preferred_element_type=F32)


def _gates(h, win_ref):
    d = win_ref.shape[0]
    base = POOL_WIDTH + len(ATTN_GROUPS) * QKV_GROUP_WIDTH
    gate_pool = jax.nn.sigmoid(jnp.dot(h, win_ref[:, base:base + d], preferred_element_type=F32))
    gate_attn = jax.nn.sigmoid(jnp.dot(h, win_ref[:, base + d:base + 2 * d], preferred_element_type=F32))
    return gate_pool, gate_attn


def _prompt_mix_in_kernel(x_ref, g_ref, win_ref, cos_ref, sin_ref, wgrp_ref, pscale_ref, wpbr_ref,
                          s0_ref, s1_ref, s2_ref,
                          kvt0_ref, kvt1_ref, kvt2_ref, tail_ref, pp_ref, ga_ref,
                          uext_ref, sums_ref, qkv_ref, mid_ref, *, ts):
    j = pl.program_id(1)

    @pl.when(j == 0)
    def _():
        uext_ref[0:POOL_PAD + POOL_CARRY, :] = jnp.zeros((POOL_PAD + POOL_CARRY, POOL_WIDTH), F32)
        sums_ref[:, 0:POOL_PAD, :] = jnp.zeros((sums_ref.shape[0], POOL_PAD, POOL_WIDTH), F32)

    h = _rmsnorm(x_ref[0], g_ref[...]).astype(BF16)

    cos = cos_ref[...]
    sin_signed = sin_ref[...]
    first_half = _first_half_mask()
    outs = ((s0_ref, kvt0_ref), (s1_ref, kvt1_ref), (s2_ref, kvt2_ref))
    planes = QKV_GROUP_WIDTH // V7X_LANES
    slot = 0
    for g in sorted(range(len(ATTN_GROUPS)), key=lambda g: -ATTN_GROUPS[g][1]):
        stream_ref, kvt_ref = outs[g]
        dil = ATTN_GROUPS[g][1]
        qkv = _qkv_group(h, win_ref, g, cos, sin_signed, first_half, ATTN_SCALE * LOG2_E)
        if dil == 1:
            stream_ref[0] = qkv.astype(BF16)
        else:
            for c in range(planes):
                qkv_ref[slot + c] = qkv[:, c * V7X_LANES:(c + 1) * V7X_LANES]
            inner = min(dil, MAX_FAST_SUBLANE_STRIDE)
            outer = dil // inner
            assert outer <= MAX_FAST_SUBLANE_STRIDE and outer * inner == dil
            if outer > 1:
                for c in range(planes):
                    for bb in range(inner):
                        mid_ref[c, bb * (ts // inner):(bb + 1) * (ts // inner), :] = (
                            qkv_ref[slot + c, pl.ds(bb, ts // inner, stride=inner), :])
            for r in range(dil):
                if outer > 1:
                    a, bb = r // inner, r % inner
                    rows = jnp.concatenate(
                        [mid_ref[c, pl.ds(bb * (ts // inner) + a, ts // dil, stride=outer), :]
                         for c in range(planes)], axis=1)
                else:
                    rows = jnp.concatenate([qkv_ref[slot + c, pl.ds(r, ts // dil, stride=dil), :]
                                            for c in range(planes)], axis=1)
                stream_ref[0, :, r * QKV_GROUP_WIDTH:(r + 1) * QKV_GROUP_WIDTH] = rows.astype(BF16)
            slot += planes
        keep = kvt_ref.shape[2]
        kvt_ref[0] = qkv[ts - keep:, GROUP_WIDTH:].T

    u = jnp.dot(h, win_ref[:, :POOL_WIDTH], preferred_element_type=F32)
    ext = POOL_CARRY + ts
    uext_ref[POOL_PAD + POOL_CARRY:POOL_PAD + ext, :] = u
    pos = j * ts + lax.broadcasted_iota(jnp.int32, (ts, 1), 0)
    pooled = []
    for gi, w in enumerate(POOL_WINDOWS):
        lo, hi = gi * POOL_GROUP, (gi + 1) * POOL_GROUP

        def trailing(level, shift, lo=lo, hi=hi):
            start = POOL_PAD - shift
            if level == 0:
                return uext_ref[start:start + ext, lo:hi]
            return sums_ref[level - 1, start:start + ext, lo:hi]

        span, level = 1, 0
        while span < w:
            total = trailing(level, 0) + trailing(level, span)
            span *= 2
            if span < w:
                sums_ref[level, POOL_PAD:POOL_PAD + ext, lo:hi] = total
                level += 1
        cnt = jnp.minimum(pos + 1, w).astype(F32)
        pooled.append(total[POOL_CARRY:, :] / cnt - u[:, lo:hi])
    pool_br = _pool_branch(pooled, wgrp_ref, pscale_ref, wpbr_ref)
    uext_ref[POOL_PAD:POOL_PAD + POOL_CARRY, :] = u[ts - POOL_CARRY:, :]
    tail_ref[0] = u[ts - POOL_CARRY:, :]
    gate_pool, gate_attn = _gates(h, win_ref)
    pp_ref[0] = gate_pool * pool_br
    ga_ref[0] = gate_attn


def _prompt_mix_in_call(x, g, win, cos, sin, wgrp, pscale, wpbr, ts):
    b, s, d = x.shape
    seq_tiles = s // ts
    zw = win.shape[1]

    def tile(width):
        return pl.BlockSpec((1, ts, width), lambda bi, j: (bi, j, 0))

    stream_specs, stream_shapes, kvt_specs, kvt_shapes = [], [], [], []
    for window, dil in ATTN_GROUPS:
        stream_specs.append(pl.BlockSpec((1, ts // dil, dil * QKV_GROUP_WIDTH), lambda bi, j: (bi, j, 0)))
        stream_shapes.append(jax.ShapeDtypeStruct((b, s // dil, dil * QKV_GROUP_WIDTH), BF16))
        keep = min(window, s)
        if keep >= ts:
            first = seq_tiles - keep // ts
            kvt_specs.append(pl.BlockSpec((1, 2 * GROUP_WIDTH, ts),
                                          lambda bi, j, first=first: (bi, 0, jnp.maximum(j - first, 0))))
        else:
            kvt_specs.append(pl.BlockSpec((1, 2 * GROUP_WIDTH, keep), lambda bi, j: (bi, 0, 0)))
        kvt_shapes.append(jax.ShapeDtypeStruct((b, 2 * GROUP_WIDTH, keep), F32))
    n_g = len(ATTN_GROUPS)
    out_specs = (stream_specs + kvt_specs
                 + [pl.BlockSpec((1, POOL_CARRY, POOL_WIDTH), lambda bi, j: (bi, 0, 0)), tile(d), tile(d)])
    out_shape = (stream_shapes + kvt_shapes
                 + [jax.ShapeDtypeStruct((b, POOL_CARRY, POOL_WIDTH), F32),
                    jax.ShapeDtypeStruct((b, s, d), F32), jax.ShapeDtypeStruct((b, s, d), F32)])
    table = pl.BlockSpec((ts, V7X_LANES), lambda bi, j: (j, 0))
    pipelined = (3 * _nbytes((ts, d), F32) + n_g * _nbytes((ts, QKV_GROUP_WIDTH), BF16)
                 + n_g * _nbytes((ts, 2 * GROUP_WIDTH), F32) + 2 * _nbytes((ts, V7X_LANES), F32))
    resident = _nbytes((d, zw), BF16) + _nbytes(wpbr.shape, BF16) + _nbytes(wgrp.shape, BF16)
    n_dilated = sum(1 for _, dil in ATTN_GROUPS if dil != 1)
    pool_rows = POOL_PAD + POOL_CARRY + ts
    pool_levels = max(POOL_WINDOWS).bit_length() - 2
    assert all(w & (w - 1) == 0 for w in POOL_WINDOWS) and max(POOL_WINDOWS) // 2 <= POOL_PAD
    scratch = [pltpu.VMEM((pool_rows, POOL_WIDTH), F32),
               pltpu.VMEM((pool_levels, pool_rows, POOL_WIDTH), F32),
               pltpu.VMEM((n_dilated * QKV_GROUP_WIDTH // V7X_LANES, ts, V7X_LANES), F32),
               pltpu.VMEM((QKV_GROUP_WIDTH // V7X_LANES, ts, V7X_LANES), F32)]
    temps = ((1 + pool_levels) * _nbytes((pool_rows, POOL_WIDTH), F32)
             + (n_dilated + 1) * _nbytes((ts, QKV_GROUP_WIDTH), F32)
             + 6 * _nbytes((ts, d), F32))
    return pl.pallas_call(
        functools.partial(_prompt_mix_in_kernel, ts=ts),
        grid=(b, seq_tiles),
        in_specs=[tile(d), _resident((1, d)), _resident((d, zw)), table, table,
                  _resident(wgrp.shape), _resident((1, POOL_WIDTH)), _resident(wpbr.shape)],
        out_specs=out_specs,
        out_shape=out_shape,
        scratch_shapes=scratch,
        compiler_params=pltpu.CompilerParams(dimension_semantics=("arbitrary", "arbitrary"),
                                             vmem_limit_bytes=_vmem_limit(pipelined, resident, temps)),
        name="prompt_mix_in",
    )(x, g, win, cos, sin, wgrp, pscale, wpbr)


def _head_lane_masks():
    lane = lax.broadcasted_iota(jnp.int32, (1, GROUP_WIDTH), 1)
    return [(lane // HEAD_DIM) == hd for hd in range(HEADS)]


def _band_scores(q, kt, head_ones):
    q_heads = jnp.concatenate([q * one for one in head_ones], axis=0)
    return jnp.dot(q_heads, kt, preferred_element_type=F32)


def _band_softmax(sc, bias_ref, keys):
    blk = sc.shape[0] // HEADS
    heads = []
    for hd in range(HEADS):
        s_h = sc[hd * blk:(hd + 1) * blk, :] + bias_ref[:, :keys]
        mx = jnp.max(s_h, axis=1, keepdims=True)
        e = jnp.exp2(s_h - mx)
        heads.append((e.astype(BF16), mx, jnp.sum(e, axis=1, keepdims=True)))
    return heads


def _band_outputs(heads, v, low_head):
    outs, lses = [], []
    for hd, (e, mx, den) in enumerate(heads):
        tile = slice((hd * HEAD_DIM // V7X_LANES) * V7X_LANES, (hd * HEAD_DIM // V7X_LANES + 1) * V7X_LANES)
        o_un = jnp.dot(e, v[:, tile], preferred_element_type=F32)
        outs.append(o_un * (1.0 / den))
        lses.append(jnp.broadcast_to(mx + jnp.log2(den), (e.shape[0], V7X_LANES)))
    pairs = range(0, HEADS, V7X_LANES // HEAD_DIM)
    return ([jnp.where(low_head, outs[hd], outs[hd + 1]) for hd in pairs],
            [jnp.where(low_head, lses[hd], lses[hd + 1]) for hd in pairs])


ATTN_PLANES = GROUP_WIDTH // V7X_LANES
ATTN_BLOCKS_SIDE_BY_SIDE = {1: 1, 4: 3, 16: 2}


def _stream_attn_kernel(s_ref, o_ref, l_ref, first_bias_ref, band_bias_ref, *, dil, seq):
    blk = KEYS_PER_QUERY

    @pl.when(pl.program_id(0) == 0)
    def _():
        qi = lax.broadcasted_iota(jnp.int32, (blk, 1), 0)
        first = lax.broadcasted_iota(jnp.int32, (1, blk), 1) <= qi
        dist = qi + blk - lax.broadcasted_iota(jnp.int32, (1, 2 * blk), 1)
        first_bias_ref[...] = jnp.where(first, 0.0, NEG_INF)
        band_bias_ref[...] = jnp.where((dist >= 0) & (dist <= blk), 0.0, NEG_INF)

    head_ones = [jnp.where(m, 1.0, 0.0).astype(BF16) for m in _head_lane_masks()]
    low_head = lax.broadcasted_iota(jnp.int32, (1, V7X_LANES), 1) < HEAD_DIM
    blocks = [(n, r, i) for n in range(s_ref.shape[0]) for r in range(dil) for i in range(seq // (dil * blk))]
    side = ATTN_BLOCKS_SIDE_BY_SIDE.get(dil, 1)
    for first in range(0, len(blocks), side):
        operands = []
        for n, r, i in blocks[first:first + side]:
            base = r * QKV_GROUP_WIDTH
            krows = slice(max(i - 1, 0) * blk, (i + 1) * blk)
            operands.append((s_ref[n, i * blk:(i + 1) * blk, base:base + GROUP_WIDTH],
                             s_ref[n, krows, base + GROUP_WIDTH:base + 2 * GROUP_WIDTH].T,
                             s_ref[n, krows, base + 2 * GROUP_WIDTH:base + 3 * GROUP_WIDTH]))
        scores = [_band_scores(q, kt, head_ones) for q, kt, _ in operands]
        softmaxes = [_band_softmax(sc, first_bias_ref if i == 0 else band_bias_ref, v.shape[0])
                     for sc, (_, _, v), (_, _, i) in zip(scores, operands, blocks[first:first + side])]
        for heads, (_, _, v), (n, r, i) in zip(softmaxes, operands, blocks[first:first + side]):
            outs, lses = _band_outputs(heads, v, low_head)
            for c in range(ATTN_PLANES):
                o_ref[n, c, r, i * blk:(i + 1) * blk, :] = outs[c]
                l_ref[n, c, r, i * blk:(i + 1) * blk, :] = lses[c]


def _merge_groups(outs, lses, exp_fn):
    mx = functools.reduce(jnp.maximum, lses)
    ws = [exp_fn(l - mx) for l in lses]
    num = functools.reduce(lambda a, b: a + b, [w * o for w, o in zip(ws, outs)])
    return num / functools.reduce(lambda a, b: a + b, ws)


def _stream_attn_call(streams, dil, seq):
    b = streams.shape[0]
    per_step = ATTN_SEQS_PER_STEP if b % ATTN_SEQS_PER_STEP == 0 else 1

    def whole(a):
        zeros = (0,) * (len(a.shape) - 1)
        return pl.BlockSpec((per_step,) + tuple(a.shape[1:]), lambda bi: (bi,) + zeros)

    planes = jax.ShapeDtypeStruct((b, ATTN_PLANES, dil, seq // dil, V7X_LANES), F32)
    pipelined = per_step * (_nbytes((seq, QKV_GROUP_WIDTH), BF16) + 2 * _nbytes((seq, GROUP_WIDTH), F32))
    scratch = [pltpu.VMEM((KEYS_PER_QUERY, KEYS_PER_QUERY), F32),
               pltpu.VMEM((KEYS_PER_QUERY, 2 * KEYS_PER_QUERY), F32)]
    scratch_bytes = 3 * _nbytes((KEYS_PER_QUERY, KEYS_PER_QUERY), F32)
    return pl.pallas_call(
        functools.partial(_stream_attn_kernel, dil=dil, seq=seq),
        grid=(b // per_step,),
        in_specs=[whole(streams)],
        out_specs=[whole(planes), whole(planes)],
        out_shape=[planes, planes],
        scratch_shapes=scratch,
        compiler_params=pltpu.CompilerParams(dimension_semantics=("arbitrary",),
                                             vmem_limit_bytes=_vmem_limit(pipelined, scratch_bytes, 16 << 20)),
        name=f"stream_attn_d{dil}",
    )(streams)


def _token_order(ref, plane, stage_ref, slot):
    dil, rows = ref.shape[2], ref.shape[3]
    if dil == 1:
        return ref[0, plane, 0]
    if dil <= MAX_FAST_SUBLANE_STRIDE:
        for r in range(dil):
            stage_ref[slot, pl.ds(r, rows, stride=dil), :] = ref[0, plane, r]
        return stage_ref[slot]
    inner = MAX_FAST_SUBLANE_STRIDE
    outer = dil // inner
    assert outer <= MAX_FAST_SUBLANE_STRIDE and outer * inner == dil
    tmp = stage_ref.shape[0] - 1
    mid = rows * outer
    for b in range(inner):
        for a in range(outer):
            stage_ref[tmp, pl.ds(b * mid + a, rows, stride=outer), :] = ref[0, plane, inner * a + b]
    for b in range(inner):
        stage_ref[slot, pl.ds(b, mid, stride=inner), :] = stage_ref[tmp, b * mid:(b + 1) * mid, :]
    return stage_ref[slot]


def _mix_out_rows(x, pp, ga, attn, wabr_ref, wo_ref, g2_ref, wg_ref, wu_ref, wd_ref, gf_ref):
    attn_br = jnp.dot(attn, wabr_ref[...], preferred_element_type=F32)
    merged = pp + ga * attn_br
    x = x + jnp.dot(merged.astype(BF16), wo_ref[...], preferred_element_type=F32)
    x = _macaron_ffn(x, g2_ref, wg_ref, wu_ref, wd_ref)
    return _rmsnorm(x, gf_ref[...])


def _mix_out_kernel(x_ref, pp_ref, ga_ref, *rest, n_groups):
    group_refs, rest = rest[:2 * n_groups], rest[2 * n_groups:]
    weights, rest = rest[:7], rest[7:]
    xs_ref, spp_ref, sga_ref, sattn_ref, y_ref, ys_ref, stage_ref = rest

    @pl.when(pl.program_id(0) == 0)
    def _():
        ys_ref[...] = _mix_out_rows(xs_ref[...], spp_ref[...], sga_ref[...], sattn_ref[...], *weights)

    tiles = []
    for c in range(ATTN_PLANES):
        outs = [_token_order(group_refs[2 * g], c, stage_ref, 2 * g) for g in range(n_groups)]
        lses = [_token_order(group_refs[2 * g + 1], c, stage_ref, 2 * g + 1) for g in range(n_groups)]
        tiles.append(_merge_groups(outs, lses, jnp.exp2).astype(BF16))
    attn = jnp.concatenate(tiles, axis=1)
    y_ref[...] = _mix_out_rows(x_ref[...], pp_ref[...], ga_ref[...], attn, *weights)


def _mix_out_call(x, pp, ga, group_planes, decode, wabr, wo, g2, wg, wu, wd, gf, tm):
    m, d = x.shape
    f = wg.shape[1]
    n_tiles = m // tm
    row = pl.BlockSpec((tm, d), lambda i: (i, 0))
    plane_specs = []
    for p in group_planes:
        dil, seq_rows = p.shape[2], p.shape[3]
        tiles = seq_rows * dil // tm
        plane_specs.append(pl.BlockSpec((1, ATTN_PLANES, dil, tm // dil, V7X_LANES),
                                        lambda i, tiles=tiles: (i // tiles, 0, 0, i % tiles, 0)))

    def whole(a):
        return pl.BlockSpec(a.shape, lambda i: (0, 0))

    n_dec = decode[0].shape[0]
    pipelined = (4 * _nbytes((tm, d), F32) + len(group_planes) * _nbytes((tm, GROUP_WIDTH), F32)
                 + 4 * _nbytes((n_dec, d), F32))
    scratch = [pltpu.VMEM((len(group_planes) + 1, tm, V7X_LANES), F32)]
    resident = 3 * _nbytes((d, f), BF16) + _nbytes((d, d), BF16) + _nbytes((GROUP_WIDTH, d), BF16)
    temps = 4 * _nbytes((tm, FFN_CHUNK), F32) + 4 * _nbytes((tm, d), F32)
    return pl.pallas_call(
        functools.partial(_mix_out_kernel, n_groups=len(group_planes) // 2),
        grid=(n_tiles,),
        in_specs=[row, row, row] + plane_specs
                 + [_resident((GROUP_WIDTH, d)), _resident((d, d)), _resident((1, d)),
                    _resident((d, f)), _resident((d, f)), _resident((f, d)), _resident((1, d))]
                 + [whole(a) for a in decode],
        out_specs=[row, pl.BlockSpec((n_dec, d), lambda i: (0, 0))],
        out_shape=[jax.ShapeDtypeStruct((m, d), F32), jax.ShapeDtypeStruct((n_dec, d), F32)],
        scratch_shapes=scratch,
        compiler_params=pltpu.CompilerParams(dimension_semantics=("arbitrary",),
                                             vmem_limit_bytes=_vmem_limit(pipelined, resident, temps)),
        name="mix_out",
    )(x, pp, ga, *group_planes, wabr, wo, g2, wg, wu, wd, gf, *decode)


def _sample_mix_in_kernel(x_ref, g_ref, win_ref, state_ref, cos_ref, sin_ref, wgrp_ref, pscale_ref, wpbr_ref,
                          qkv0_ref, qkv1_ref, qkv2_ref, u_ref, pp_ref, ga_ref):
    h = _rmsnorm(x_ref[...], g_ref[...]).astype(BF16)
    u = jnp.dot(h, win_ref[:, :POOL_WIDTH], preferred_element_type=F32)
    u_ref[...] = u
    pooled = []
    for gi, w in enumerate(POOL_WINDOWS):
        lo, hi = gi * POOL_GROUP, (gi + 1) * POOL_GROUP
        tok = u[:, lo:hi]
        total = tok
        for back in range(1, w):
            total = total + state_ref[POOL_STATE - back, :, lo:hi]
        pooled.append(total / float(min(PAST_LEN + 1, w)) - tok)
    pool_br = _pool_branch(pooled, wgrp_ref, pscale_ref, wpbr_ref)
    gate_pool, gate_attn = _gates(h, win_ref)
    pp_ref[...] = gate_pool * pool_br
    ga_ref[...] = gate_attn
    first_half = _first_half_mask()
    n = x_ref.shape[0]
    for g, qkv_ref in enumerate((qkv0_ref, qkv1_ref, qkv2_ref)):
        qkv = _qkv_group(h, win_ref, g, cos_ref[...], sin_ref[...], first_half, ATTN_SCALE)
        padded = jnp.concatenate([qkv, jnp.zeros((V7X_LANES - n, QKV_GROUP_WIDTH), F32)], axis=0)
        qkv_ref[...] = jnp.transpose(padded)[:, :n]


def _sample_mix_in_call(x, g, win, state_t, cos, sin, wgrp, pscale, wpbr):
    n, d = x.shape
    n_g = len(ATTN_GROUPS)

    def full(shape):
        zeros = (0,) * len(shape)
        return pl.BlockSpec(shape, lambda i: zeros)

    assert n <= V7X_LANES
    out_shape = ([jax.ShapeDtypeStruct((QKV_GROUP_WIDTH, n), F32)] * n_g
                 + [jax.ShapeDtypeStruct((n, POOL_WIDTH), F32),
                    jax.ShapeDtypeStruct((n, d), F32), jax.ShapeDtypeStruct((n, d), F32)])
    ins = (x, g, win, state_t, cos, sin, wgrp, pscale, wpbr)
    resident = sum(_nbytes(a.shape, a.dtype) for a in ins)
    return pl.pallas_call(
        _sample_mix_in_kernel,
        grid=(1,),
        in_specs=[full(a.shape) for a in ins],
        out_specs=[full(o.shape) for o in out_shape],
        out_shape=out_shape,
        compiler_params=pltpu.CompilerParams(dimension_semantics=("arbitrary",),
                                             vmem_limit_bytes=_vmem_limit(resident, 0, 8 << 20)),
        name="sample_mix_in",
    )(*ins)


def _head_sums(x):
    return jnp.concatenate([jnp.sum(x[hd * HEAD_DIM:(hd + 1) * HEAD_DIM], axis=0, keepdims=True)
                            for hd in range(HEADS)], axis=0)


def _head_expand(y):
    return jnp.concatenate([jnp.broadcast_to(y[hd:hd + 1], (HEAD_DIM, y.shape[1])) for hd in range(HEADS)],
                           axis=0)


def _cache_step_kernel(qkv0_ref, qkv1_ref, qkv2_ref, c0_ref, c1_ref, c2_ref,
                       n0_ref, n1_ref, n2_ref, attn_ref):
    step = pl.program_id(0)

    @pl.when(step == 0)
    def _():
        attn_ref[...] = jnp.zeros(attn_ref.shape, F32)

    groups = ((qkv0_ref, c0_ref, n0_ref), (qkv1_ref, c1_ref, n1_ref), (qkv2_ref, c2_ref, n2_ref))
    per_step = c0_ref.shape[0]
    attn = attn_ref[...]
    for j in range(per_step):
        n = step * per_step + j
        request = lax.broadcasted_iota(jnp.int32, qkv0_ref.shape, 1) == n
        cols = [jnp.sum(jnp.where(request, qkv_ref[...], 0.0), axis=1, keepdims=True) for qkv_ref, _, _ in groups]
        for qkv_col, (_, c_ref, new_ref), (window, _) in zip(cols, groups, ATTN_GROUPS):
            lane = lax.broadcasted_iota(jnp.int32, (1, window), 1)
            new_ref[j] = jnp.where(lane == window - 1, qkv_col[GROUP_WIDTH:], pltpu.roll(c_ref[j], window - 1, 1))
        outs, lses = [], []
        for qkv_col, (_, c_ref, _), (window, dil) in zip(cols, groups, ATTN_GROUPS):
            q = qkv_col[:GROUP_WIDTH]
            k_new = qkv_col[GROUP_WIDTH:2 * GROUP_WIDTH]
            v_new = qkv_col[2 * GROUP_WIDTH:]
            kt = c_ref[j, :GROUP_WIDTH, :]
            vt = c_ref[j, GROUP_WIDTH:, :]
            lane = lax.broadcasted_iota(jnp.int32, (1, window), 1)
            sc = jnp.where(lane % dil == 0, _head_sums(kt * q), NEG_INF)
            sc_new = _head_sums(k_new * q)
            mx = jnp.maximum(jnp.max(sc, axis=1, keepdims=True), sc_new)
            e = jnp.exp(sc - mx)
            e_new = jnp.exp(sc_new - mx)
            den = jnp.sum(e, axis=1, keepdims=True) + e_new
            o_un = jnp.sum(vt * _head_expand(e), axis=1, keepdims=True) + _head_expand(e_new) * v_new
            outs.append(o_un / _head_expand(den))
            lses.append(_head_expand(mx + jnp.log(den)))
        attn_col = _merge_groups(outs, lses, jnp.exp)
        attn = jnp.where(lax.broadcasted_iota(jnp.int32, attn_ref.shape, 1) == n, attn_col, attn)
    attn_ref[...] = attn


def _cache_step_call(qkv_cols, windows_t):
    n = qkv_cols[0].shape[1]
    per_step = CACHE_STEP_REQUESTS
    assert n % per_step == 0
    qspec = pl.BlockSpec((QKV_GROUP_WIDTH, n), lambda i: (0, 0))
    cspecs = [pl.BlockSpec((per_step,) + c.shape[1:], lambda i: (i, 0, 0)) for c in windows_t]
    pipelined = (2 * per_step * sum(_nbytes(c.shape[1:], F32) for c in windows_t)
                 + 3 * _nbytes((QKV_GROUP_WIDTH, V7X_LANES), F32))
    res = pl.pallas_call(
        _cache_step_kernel,
        grid=(n // per_step,),
        in_specs=[qspec] * len(qkv_cols) + cspecs,
        out_specs=cspecs + [pl.BlockSpec((GROUP_WIDTH, n), lambda i: (0, 0))],
        out_shape=[jax.ShapeDtypeStruct(c.shape, F32) for c in windows_t]
                  + [jax.ShapeDtypeStruct((GROUP_WIDTH, n), F32)],
        compiler_params=pltpu.CompilerParams(dimension_semantics=("arbitrary",),
                                             vmem_limit_bytes=_vmem_limit(pipelined, 0, 16 << 20)),
        name="cache_step",
    )(*qkv_cols, *windows_t)
    return res[:-1], res[-1]


def _rope_tables(pos):
    half = HEAD_DIM // 2
    inv = ROPE_THETA ** (-jnp.arange(half, dtype=F32) * (2.0 / HEAD_DIM))
    ang = pos.astype(F32)[:, None] * inv[None, :]
    cos, sin = jnp.cos(ang), jnp.sin(ang)
    reps = V7X_LANES // HEAD_DIM
    return (jnp.tile(jnp.concatenate([cos, cos], axis=1), (1, reps)),
            jnp.tile(jnp.concatenate([-sin, sin], axis=1), (1, reps)))


def _window_to_lanes(w):
    n, width = w.shape[:2]
    return jnp.transpose(w, (0, 2, 3, 4, 1)).reshape(n, 2 * GROUP_WIDTH, width)


def _window_from_lanes(wt):
    n, _, width = wt.shape
    return jnp.transpose(wt.reshape(n, 2, HEADS, HEAD_DIM, width), (0, 4, 1, 2, 3))[None]


def kernel(x_prompt, x_sample, state_pool, cache_kv_w128, cache_kv_w512, cache_kv_w2048, norm_ffn1, ffn1_w_gate, ffn1_w_up, ffn1_w_down, norm_mix, w_in, w_pool_grp, pool_scale, w_pool_br, w_attn_br, w_o, norm_ffn2, ffn2_w_gate, ffn2_w_up, ffn2_w_down, norm_final):
    b, s, d = x_prompt.shape
    n, dec_seq, _ = x_sample.shape
    caches = (cache_kv_w128, cache_kv_w512, cache_kv_w2048)
    assert norm_ffn1.shape[0] == 1 and dec_seq == 1, "single layer, single decode token"
    ts = PROMPT_ROW_TILE
    assert s % ts == 0 and ts >= POOL_CARRY
    for cache, (window, dil) in zip(caches, ATTN_GROUPS):
        assert window // dil == KEYS_PER_QUERY and s % (dil * KEYS_PER_QUERY) == 0 and ts % (16 * dil) == 0
        assert cache.shape[2] == window, "decode path expects full windows of history"

    row = lambda v: v.reshape(1, -1)
    g1, g2, gm, gf = row(norm_ffn1[0]), row(norm_ffn2[0]), row(norm_mix[0]), row(norm_final)
    f1 = (ffn1_w_gate[0], ffn1_w_up[0], ffn1_w_down[0])
    pscale = row(pool_scale[0])

    cos_p, sin_p = _rope_tables(jnp.arange(s, dtype=jnp.int32))
    later = (ffn2_w_gate[0], ffn2_w_up[0], ffn2_w_down[0], w_in[0],
             w_pool_grp[0].reshape(-1, POOL_GROUP), w_pool_br[0], w_attn_br[0], w_o[0])
    x1, xs1, (f2g, f2u, f2d, win, wgrp, wpbr, wabr, wo) = _ffn_call(
        x_prompt.reshape(b * s, d), x_sample.reshape(n, d), g1, *f1, tm=FFN_ROW_TILE, cast=later)
    f2 = (f2g, f2u, f2d)
    wgrp = wgrp.reshape(w_pool_grp.shape[1:])
    (st0, st1, st2, kvt0, kvt1, kvt2, tail, pp, ga) = _prompt_mix_in_call(
        x1.reshape(b, s, d), gm, win, cos_p, sin_p, wgrp, pscale, wpbr, ts)
    group_planes = []
    for streams, (_, dil) in zip((st0, st1, st2), ATTN_GROUPS):
        group_planes += _stream_attn_call(streams, dil, s)
    pool_prompt = tail[None, :, POOL_CARRY - POOL_STATE:, :]

    cos_s, sin_s = _rope_tables(PAST_LEN + jnp.arange(dec_seq, dtype=jnp.int32))
    state_t = jnp.swapaxes(state_pool[0], 0, 1)
    (sqkv0, sqkv1, sqkv2, u_new, spp, sga) = _sample_mix_in_call(
        xs1, gm, win, state_t, cos_s, sin_s, wgrp, pscale, wpbr)
    new_windows, attn_t = _cache_step_call((sqkv0, sqkv1, sqkv2),
                                           [_window_to_lanes(c[0]) for c in caches])
    pool_sample = jnp.swapaxes(jnp.concatenate([state_t[1:], u_new[None]], axis=0), 0, 1)[None]

    y_prompt, y_sample = _mix_out_call(x1, pp.reshape(b * s, d), ga.reshape(b * s, d), tuple(group_planes),
                                       (xs1, spp, sga, attn_t.T.astype(BF16)), wabr, wo, g2, *f2, gf, tm=ts)
    y_prompt = y_prompt.reshape(b, s, d)
    y_sample = y_sample.reshape(n, dec_seq, d)

    return (y_prompt, y_sample, pool_prompt,
            _window_from_lanes(kvt0), _window_from_lanes(kvt1), _window_from_lanes(kvt2),
            pool_sample, *[_window_from_lanes(w) for w in new_windows])
```
